```python
import jax, jax.numpy as jnp
from jax import lax
import numpy as np

D_MODEL = 1024
BATCH = 8
SEQ = 4096
DEPTH = 1

MEM_LEN = 256
CONV_WIDTH = D_MODEL // 2
CONV_K = 3
ATT_HEADS = 8
HEAD_DIM = 64
ATT_WIDTH = ATT_HEADS * HEAD_DIM
KV_GROUPS = 2
Q_PER_KV = ATT_HEADS // KV_GROUPS
KV_WIDTH = KV_GROUPS * HEAD_DIM
MIX_WIDTH = CONV_WIDTH + ATT_WIDTH
CMP_BLOCK = 32
CMP_STRIDE = 16
CMP_HIDDEN = 256
SLC_BLOCK = 64
N_SELECT = 16
WINDOW = 512
Q_BLOCK = 128
ROPE_THETA = 10000.0
X_HEADS = 4
X_HEAD_DIM = D_MODEL // X_HEADS
D_FF = -(-8 * D_MODEL // (3 * 256)) * 256
EPS = 1e-6
FORCE = 1e4
W_IN_COLS = 3 * CONV_WIDTH + ATT_WIDTH + 6 * KV_WIDTH + 3 * ATT_HEADS

kernel_name = "hymba_conv_nsa_sandwich_block"


def rmsnorm(x, g):
    xf = x.astype(jnp.float32)
    y = xf * lax.rsqrt(jnp.mean(xf * xf, axis=-1, keepdims=True) + EPS)
    return (y * g.astype(jnp.float32)).astype(x.dtype)


def rope(x, pos):
    half = HEAD_DIM // 2
    inv = ROPE_THETA ** (-jnp.arange(half, dtype=jnp.float32) / half)
    ang = pos.astype(jnp.float32)[..., None] * inv
    cos = jnp.cos(ang)[:, :, None, :]
    sin = jnp.sin(ang)[:, :, None, :]
    xf = x.astype(jnp.float32)
    x1, x2 = xf[..., :half], xf[..., half:]
    return jnp.concatenate([x1 * cos - x2 * sin, x2 * cos + x1 * sin], axis=-1).astype(x.dtype)


def masked_softmax(s, mask):
    s = jnp.where(mask, s.astype(jnp.float32), -1e30)
    e = jnp.exp(s - jnp.max(s, axis=-1, keepdims=True)) * mask
    return e / jnp.maximum(jnp.sum(e, axis=-1, keepdims=True), 1e-30)


def compress(k, pe, w1, w2, n_cmp):
    b = k.shape[0]
    idx = np.arange(n_cmp)[:, None] * CMP_STRIDE + np.arange(CMP_BLOCK)[None, :]
    blocks = k[:, idx] + pe[None, None, :, None, :]
    flat = blocks.transpose(0, 1, 3, 2, 4).reshape(b, n_cmp, KV_GROUPS, CMP_BLOCK * HEAD_DIM)
    return jax.nn.silu(flat @ w1) @ w2


def short_conv_mixer(bg, cg, xin, conv_w):
    z = cg * xin
    zc = lax.conv_general_dilated(z, conv_w[:, None, :].astype(z.dtype), window_strides=(1,),
                                  padding=((CONV_K - 1, 0),), dimension_numbers=("NWC", "WIO", "NWC"),
                                  feature_group_count=CONV_WIDTH)
    return bg * zc


def nsa_mixer(q, k_cmp_raw, v_cmp_raw, k_slc, v_slc, k_win, v_win, gates, positions,
              pe_kc, w1_kc, w2_kc, pe_vc, w1_vc, w2_vc):
    b, s = q.shape[0], q.shape[1]
    n_cmp = (s - CMP_BLOCK) // CMP_STRIDE + 1
    n_slc = s // SLC_BLOCK
    n_sel = min(N_SELECT, n_slc)
    scale = HEAD_DIM ** -0.5

    q = rope(q, positions)
    cmp_end_np = np.arange(n_cmp) * CMP_STRIDE + CMP_BLOCK - 1
    cmp_end = jnp.asarray(cmp_end_np, dtype=jnp.int32)
    k_cmp = rope(compress(k_cmp_raw, pe_kc, w1_kc, w2_kc, n_cmp), positions[:, cmp_end_np])
    v_cmp = compress(v_cmp_raw, pe_vc, w1_vc, w2_vc, n_cmp)

    ci = np.arange(n_cmp)[:, None] * CMP_STRIDE
    sj = np.arange(n_slc)[None, :] * SLC_BLOCK
    ov = np.clip(np.minimum(ci + CMP_BLOCK, sj + SLC_BLOCK) - np.maximum(ci, sj), 0, None)
    slc_map = jnp.asarray(ov / CMP_BLOCK, dtype=jnp.float32)

    k_blocks = rope(k_slc, positions).reshape(b, n_slc, SLC_BLOCK, KV_GROUPS, HEAD_DIM).transpose(0, 3, 1, 2, 4)
    v_blocks = v_slc.reshape(b, n_slc, SLC_BLOCK, KV_GROUPS, HEAD_DIM).transpose(0, 3, 1, 2, 4)
    pad = ((0, 0), (WINDOW, 0), (0, 0), (0, 0))
    k_win_pad = jnp.pad(rope(k_win, positions), pad)
    v_win_pad = jnp.pad(v_win, pad)
    bi = jnp.arange(b)[:, None, None, None]
    gi = jnp.arange(KV_GROUPS)[None, :, None, None]
    blk = jnp.arange(n_slc)

    def block(i):
        s0 = i * Q_BLOCK
        t = s0 + jnp.arange(Q_BLOCK)
        qb = lax.dynamic_slice_in_dim(q, s0, Q_BLOCK, axis=1).reshape(b, Q_BLOCK, KV_GROUPS, Q_PER_KV, HEAD_DIM)
        gb = lax.dynamic_slice_in_dim(gates, s0, Q_BLOCK, axis=1).reshape(b, Q_BLOCK, KV_GROUPS, Q_PER_KV, 3)

        s_c = jnp.einsum("bqgrd,bkgd->bgrqk", qb, k_cmp) * scale
        p_c = masked_softmax(s_c, cmp_end[None, :] <= t[:, None])
        o_c = jnp.einsum("bgrqk,bkgd->bqgrd", p_c, v_cmp)

        imp = jnp.einsum("bgrqk,ks->bgqs", p_c, slc_map)
        cur = t // SLC_BLOCK
        forced = (blk[None, :] == 0) | (blk[None, :] == cur[:, None]) | (blk[None, :] == cur[:, None] - 1)
        future = blk[None, :] * SLC_BLOCK > t[:, None]
        imp = jnp.where(forced, FORCE, jnp.where(future, -FORCE, imp))
        _, sel = lax.top_k(imp, n_sel)
        ks = k_blocks[bi, gi, sel].reshape(b, KV_GROUPS, Q_BLOCK, n_sel * SLC_BLOCK, HEAD_DIM)
        vs = v_blocks[bi, gi, sel].reshape(b, KV_GROUPS, Q_BLOCK, n_sel * SLC_BLOCK, HEAD_DIM)
        kpos = (sel[..., None] * SLC_BLOCK + jnp.arange(SLC_BLOCK)).reshape(b, KV_GROUPS, Q_BLOCK, n_sel * SLC_BLOCK)
        s_s = jnp.einsum("bqgrd,bgqkd->bgrqk", qb, ks) * scale
        p_s = masked_softmax(s_s, (kpos <= t[None, None, :, None])[:, :, None])
        o_s = jnp.einsum("bgrqk,bgqkd->bqgrd", p_s, vs)

        kw = lax.dynamic_slice_in_dim(k_win_pad, s0, Q_BLOCK + WINDOW, axis=1)
        vw = lax.dynamic_slice_in_dim(v_win_pad, s0, Q_BLOCK + WINDOW, axis=1)
        wpos = s0 - WINDOW + jnp.arange(Q_BLOCK + WINDOW)
        mask_w = (wpos[None, :] <= t[:, None]) & (wpos[None, :] > t[:, None] - WINDOW) & (wpos[None, :] >= 0)
        s_w = jnp.einsum("bqgrd,bkgd->bgrqk", qb, kw) * scale
        p_w = masked_softmax(s_w, mask_w)
        o_w = jnp.einsum("bgrqk,bkgd->bqgrd", p_w, vw)

        o = gb[..., 0:1] * o_c + gb[..., 1:2] * o_s + gb[..., 2:3] * o_w
        return o.reshape(b, Q_BLOCK, ATT_WIDTH).astype(q.dtype)

    out = lax.map(block, jnp.arange(s // Q_BLOCK))
    return out.transpose(1, 0, 2, 3).reshape(b, s, ATT_WIDTH)


def setup_inputs(seed: int = 0) -> dict:
    key = jax.random.key(seed)
    ks = jax.random.split(key, 32)

    def w(k, shape, fan_in):
        return jax.random.normal(k, shape, jnp.float32) * fan_in ** -0.5

    def gain(k, n):
        return 1.0 + 0.05 * jax.random.normal(k, (n,), jnp.float32)

    offs = jax.random.randint(ks[2], (BATCH, 1), 0, 2048, dtype=jnp.int32)
    return {
        "x": jax.random.normal(ks[0], (BATCH, SEQ, D_MODEL), jnp.float32),
        "mem": jax.random.normal(ks[1], (BATCH, MEM_LEN, D_MODEL), jnp.float32),
        "positions": jnp.arange(SEQ, dtype=jnp.int32)[None, :] + offs,
        "norm_mix_pre": gain(ks[3], D_MODEL),
        "w_in": w(ks[4], (D_MODEL, W_IN_COLS), D_MODEL),
        "conv_w": w(ks[5], (CONV_K, CONV_WIDTH), CONV_K),
        "pe_kc": 0.1 * jax.random.normal(ks[6], (CMP_BLOCK, HEAD_DIM), jnp.float32),
        "w1_kc": w(ks[7], (CMP_BLOCK * HEAD_DIM, CMP_HIDDEN), CMP_BLOCK * HEAD_DIM),
        "w2_kc": w(ks[8], (CMP_HIDDEN, HEAD_DIM), CMP_HIDDEN),
        "pe_vc": 0.1 * jax.random.normal(ks[9], (CMP_BLOCK, HEAD_DIM), jnp.float32),
        "w1_vc": w(ks[10], (CMP_BLOCK * HEAD_DIM, CMP_HIDDEN), CMP_BLOCK * HEAD_DIM),
        "w2_vc": w(ks[11], (CMP_HIDDEN, HEAD_DIM), CMP_HIDDEN),
        "norm_conv_out": gain(ks[12], CONV_WIDTH),
        "norm_attn_out": gain(ks[13], ATT_WIDTH),
        "w_out": w(ks[14], (MIX_WIDTH, D_MODEL), MIX_WIDTH),
        "norm_mix_post": gain(ks[15], D_MODEL),
        "norm_x_pre": gain(ks[16], D_MODEL),
        "norm_mem": gain(ks[17], D_MODEL),
        "w_q_x": w(ks[18], (D_MODEL, D_MODEL), D_MODEL),
        "w_kv_x": w(ks[19], (D_MODEL, 2 * D_MODEL), D_MODEL),
        "w_o_x": w(ks[20], (D_MODEL, D_MODEL), D_MODEL),
        "norm_x_post": gain(ks[21], D_MODEL),
        "norm_ffn_pre": gain(ks[22], D_MODEL),
        "w_gate_up": w(ks[23], (D_MODEL, 2 * D_FF), D_MODEL),
        "w_down": w(ks[24], (D_FF, D_MODEL), D_FF),
        "norm_ffn_post": gain(ks[25], D_MODEL),
    }


def reference(x, mem, positions, norm_mix_pre, w_in, conv_w, pe_kc, w1_kc, w2_kc, pe_vc, w1_vc, w2_vc,
              norm_conv_out, norm_attn_out, w_out, norm_mix_post, norm_x_pre, norm_mem, w_q_x, w_kv_x,
              w_o_x, norm_x_post, norm_ffn_pre, w_gate_up, w_down, norm_ffn_post):
    b, s, _ = x.shape
    h = x
    for _layer in range(DEPTH):
        u = rmsnorm(h, norm_mix_pre)
        proj = u @ w_in
        cuts = np.cumsum([CONV_WIDTH, CONV_WIDTH, CONV_WIDTH, ATT_WIDTH] + [KV_WIDTH] * 6)
        bg, cg, xin, q, kc, vc, ksl, vsl, kw, vw, gl = jnp.split(proj, cuts, axis=-1)
        kv_shape = (b, s, KV_GROUPS, HEAD_DIM)
        y_conv = short_conv_mixer(bg, cg, xin, conv_w)
        y_att = nsa_mixer(q.reshape(b, s, ATT_HEADS, HEAD_DIM), kc.reshape(kv_shape), vc.reshape(kv_shape),
                          ksl.reshape(kv_shape), vsl.reshape(kv_shape), kw.reshape(kv_shape), vw.reshape(kv_shape),
                          jax.nn.sigmoid(gl).reshape(b, s, ATT_HEADS, 3), positions,
                          pe_kc, w1_kc, w2_kc, pe_vc, w1_vc, w2_vc)
        mixed = jnp.concatenate([rmsnorm(y_conv, norm_conv_out), rmsnorm(y_att, norm_attn_out)], axis=-1)
        h = h + rmsnorm(mixed @ w_out, norm_mix_post)

        hn = rmsnorm(h, norm_x_pre)
        mn = rmsnorm(mem, norm_mem)
        qx = (hn @ w_q_x).reshape(b, s, X_HEADS, X_HEAD_DIM)
        kx, vx = jnp.split(mn @ w_kv_x, 2, axis=-1)
        kx = kx.reshape(b, MEM_LEN, X_HEADS, X_HEAD_DIM)
        vx = vx.reshape(b, MEM_LEN, X_HEADS, X_HEAD_DIM)
        sx = jnp.einsum("bshd,bmhd->bhsm", qx, kx).astype(jnp.float32) * X_HEAD_DIM ** -0.5
        px = jax.nn.softmax(sx, axis=-1)
        ox = jnp.einsum("bhsm,bmhd->bshd", px, vx).astype(h.dtype).reshape(b, s, D_MODEL)
        h = h + rmsnorm(ox @ w_o_x, norm_x_post)

        hn = rmsnorm(h, norm_ffn_pre)
        g, up = jnp.split(hn @ w_gate_up, 2, axis=-1)
        h = h + rmsnorm((jax.nn.silu(g) * up) @ w_down, norm_ffn_post)
    return h
```

```python
import functools

import numpy as np
import jax
import jax.numpy as jnp
from jax import lax
from jax.experimental import pallas as pl
from jax.experimental.pallas import tpu as pltpu

D_MODEL = 1024
CONV_WIDTH = 512
CONV_K = 3
ATT_HEADS = 8
HEAD_DIM = 64
HALF = HEAD_DIM // 2
ATT_WIDTH = ATT_HEADS * HEAD_DIM
KV_GROUPS = 2
Q_PER_KV = ATT_HEADS // KV_GROUPS
KV_WIDTH = KV_GROUPS * HEAD_DIM
GROUP_WIDTH = Q_PER_KV * HEAD_DIM
CMP_BLOCK = 32
CMP_STRIDE = 16
CMP_HIDDEN = 256
SLC_BLOCK = 64
N_SELECT = 16
WINDOW = 512
ROPE_THETA = 10000.0
X_HEADS = 4
X_HEAD_DIM = D_MODEL // X_HEADS
D_FF = 2816
EPS = 1e-6
FORCE = 1e4
NEG = -1e30
GATE_ROWS = 16

TM_PROJ = 512
TM_MIX = 512
TM_FFN = 256
TQ = 256
TK = 256
VMEM_LIMIT = 56 * 1024 * 1024

F32 = jnp.float32
BF16 = jnp.bfloat16

_NT = (((1,), (1,)), ((), ()))
_TN = (((0,), (0,)), ((), ()))


def _dot(a, b):
    return jnp.dot(a, b, preferred_element_type=F32)


def _dot_nt(a, b):
    return lax.dot_general(a, b, _NT, preferred_element_type=F32)


def _dot_tn(a, b):
    return lax.dot_general(a, b, _TN, preferred_element_type=F32)


def _rms_rows(x, g):
    return x * lax.rsqrt(jnp.mean(x * x, axis=-1, keepdims=True) + EPS) * g


def _params(*sem):
    return pltpu.CompilerParams(dimension_semantics=sem, vmem_limit_bytes=VMEM_LIMIT)


def _whole(shape):
    nd = len(shape)
    return pl.BlockSpec(shape, lambda *_: (0,) * nd)


def _in_proj_kernel(x_ref, pos_ref, inv_ref, g_ref, wrow_ref, wt_ref, convw_ref, gconv_ref,
                    yconv_ref, kvc_ref, qt_ref, ksl_ref, kw_ref, vt_ref, gt_ref, carry_ref):
    tm = x_ref.shape[1]
    u = _rms_rows(x_ref[0], g_ref[...]).astype(BF16)
    prow = _dot(u, wrow_ref[...])
    pt = _dot_nt(wt_ref[...], u)

    bg = prow[:, 0:CONV_WIDTH]
    z = prow[:, CONV_WIDTH:2 * CONV_WIDTH] * prow[:, 2 * CONV_WIDTH:3 * CONV_WIDTH]

    @pl.when(pl.program_id(1) == 0)
    def _():
        carry_ref[...] = jnp.zeros_like(carry_ref)

    prev = carry_ref[...]
    row = lax.broadcasted_iota(jnp.int32, (tm, CONV_WIDTH), 0)
    z1 = jnp.where(row == 0, prev[7:8, :], pltpu.roll(z, 1, axis=0))
    z2 = jnp.where(row == 0, prev[6:7, :], jnp.where(row == 1, prev[7:8, :], pltpu.roll(z, 2, axis=0)))
    carry_ref[...] = z[tm - 8:tm, :]
    cw = convw_ref[...]
    yc = bg * (cw[0:1, :] * z2 + cw[1:2, :] * z1 + cw[2:3, :] * z)
    yconv_ref[0] = _rms_rows(yc, gconv_ref[...]).astype(yconv_ref.dtype)

    kvc_ref[0] = prow[:, 3 * CONV_WIDTH:3 * CONV_WIDTH + 2 * KV_WIDTH]

    ang = pos_ref[0].astype(F32) * inv_ref[...]
    cos, sin = jnp.cos(ang), jnp.sin(ang)

    def rope_t(base):
        x1 = pt[base:base + HALF, :]
        x2 = pt[base + HALF:base + HEAD_DIM, :]
        return x1 * cos - x2 * sin, x2 * cos + x1 * sin

    scale = HEAD_DIM ** -0.5
    for h in range(ATT_HEADS):
        r1, r2 = rope_t(h * HEAD_DIM)
        qt_ref[0, h * HEAD_DIM:h * HEAD_DIM + HALF, :] = (r1 * scale).astype(qt_ref.dtype)
        qt_ref[0, h * HEAD_DIM + HALF:(h + 1) * HEAD_DIM, :] = (r2 * scale).astype(qt_ref.dtype)

    for base, ref in ((ATT_WIDTH, ksl_ref), (ATT_WIDTH + KV_WIDTH, kw_ref)):
        parts = []
        for g in range(KV_GROUPS):
            parts.extend(rope_t(base + g * HEAD_DIM))
        k_rows = jnp.concatenate(parts, axis=0).T
        for g in range(KV_GROUPS):
            ref[0, g] = k_rows[:, g * HEAD_DIM:(g + 1) * HEAD_DIM].astype(ref.dtype)

    v0 = ATT_WIDTH + 2 * KV_WIDTH
    vt_ref[0] = pt[v0:v0 + 2 * KV_WIDTH, :].astype(vt_ref.dtype)
    g0 = v0 + 2 * KV_WIDTH
    gt_ref[0] = jax.nn.sigmoid(pt[g0:g0 + KV_GROUPS * GATE_ROWS, :])


def _in_proj(x, pos_row, inv_col, g_pre, w_row, w_t, conv_w, g_conv):
    b, s, _ = x.shape
    tm = min(TM_PROJ, s)
    n_row, n_t = w_row.shape[1], w_t.shape[0]
    out_shape = (
        jax.ShapeDtypeStruct((b, s, CONV_WIDTH), BF16),
        jax.ShapeDtypeStruct((b, s, 2 * KV_WIDTH), F32),
        jax.ShapeDtypeStruct((b, ATT_WIDTH, s), BF16),
        jax.ShapeDtypeStruct((b, KV_GROUPS, s, HEAD_DIM), BF16),
        jax.ShapeDtypeStruct((b, KV_GROUPS, s, HEAD_DIM), BF16),
        jax.ShapeDtypeStruct((b, 2 * KV_WIDTH, s), BF16),
        jax.ShapeDtypeStruct((b, KV_GROUPS * GATE_ROWS, s), F32),
    )
    return pl.pallas_call(
        _in_proj_kernel,
        grid=(b, s // tm),
        in_specs=[
            pl.BlockSpec((1, tm, D_MODEL), lambda i, j: (i, j, 0)),
            pl.BlockSpec((1, 1, tm), lambda i, j: (i, 0, j)),
            _whole((HALF, 1)),
            _whole((1, D_MODEL)),
            _whole((D_MODEL, n_row)),
            _whole((n_t, D_MODEL)),
            _whole((CONV_K, CONV_WIDTH)),
            _whole((1, CONV_WIDTH)),
        ],
        out_specs=(
            pl.BlockSpec((1, tm, CONV_WIDTH), lambda i, j: (i, j, 0)),
            pl.BlockSpec((1, tm, 2 * KV_WIDTH), lambda i, j: (i, j, 0)),
            pl.BlockSpec((1, ATT_WIDTH, tm), lambda i, j: (i, 0, j)),
            pl.BlockSpec((1, KV_GROUPS, tm, HEAD_DIM), lambda i, j: (i, 0, j, 0)),
            pl.BlockSpec((1, KV_GROUPS, tm, HEAD_DIM), lambda i, j: (i, 0, j, 0)),
            pl.BlockSpec((1, 2 * KV_WIDTH, tm), lambda i, j: (i, 0, j)),
            pl.BlockSpec((1, KV_GROUPS * GATE_ROWS, tm), lambda i, j: (i, 0, j)),
        ),
        out_shape=out_shape,
        scratch_shapes=[pltpu.VMEM((8, CONV_WIDTH), F32)],
        compiler_params=_params("arbitrary", "arbitrary"),
        name="in_proj",
    )(x, pos_row, inv_col, g_pre, w_row, w_t, conv_w, g_conv)


def _compress_kernel(rk_ref, rv_ref, pos_ref, inv_ref, pek_ref, w1k_ref, w2k_ref, pev_ref, w1v_ref, w2vt_ref,
                     kc_ref, vct_ref):
    half_in = CMP_STRIDE * HEAD_DIM

    def hidden(r_ref, pe_ref, w1_ref):
        r = r_ref[0, 0].astype(BF16)
        a = _dot(r, w1_ref[0:half_in, :])
        bshift = pltpu.roll(_dot(r, w1_ref[half_in:2 * half_in, :]), r.shape[0] - 1, axis=0)
        pe = jnp.broadcast_to(pe_ref[...], (8, 2 * half_in)).astype(BF16)
        h = a + bshift + _dot(pe, w1_ref[...])[0:1, :]
        return (h * jax.nn.sigmoid(h)).astype(BF16)

    kc = _dot(hidden(rk_ref, pek_ref, w1k_ref), w2k_ref[...])
    ang = pos_ref[0].astype(F32) * inv_ref[...]
    cos, sin = jnp.cos(ang), jnp.sin(ang)
    x1, x2 = kc[:, 0:HALF], kc[:, HALF:HEAD_DIM]
    kc_ref[0, 0] = jnp.concatenate([x1 * cos - x2 * sin, x2 * cos + x1 * sin], axis=-1).astype(kc_ref.dtype)
    vct_ref[0, 0] = _dot_nt(w2vt_ref[...], hidden(rv_ref, pev_ref, w1v_ref)).astype(vct_ref.dtype)


def _compress(rk, rv, pos_c, inv_row, pek, w1k, w2k, pev, w1v, w2vt):
    b, g, n_slab, width = rk.shape
    slab = pl.BlockSpec((1, 1, n_slab, width), lambda i, j: (i, j, 0, 0))
    return pl.pallas_call(
        _compress_kernel,
        grid=(b, g),
        in_specs=[
            slab, slab,
            pl.BlockSpec((1, n_slab, 1), lambda i, j: (i, 0, 0)),
            _whole((1, HALF)),
            _whole(pek.shape), _whole(w1k.shape), _whole(w2k.shape),
            _whole(pev.shape), _whole(w1v.shape), _whole(w2vt.shape),
        ],
        out_specs=(
            pl.BlockSpec((1, 1, n_slab, HEAD_DIM), lambda i, j: (i, j, 0, 0)),
            pl.BlockSpec((1, 1, HEAD_DIM, n_slab), lambda i, j: (i, j, 0, 0)),
        ),
        out_shape=(
            jax.ShapeDtypeStruct((b, g, n_slab, HEAD_DIM), BF16),
            jax.ShapeDtypeStruct((b, g, HEAD_DIM, n_slab), BF16),
        ),
        compiler_params=_params("arbitrary", "arbitrary"),
        name="compress",
    )(rk, rv, pos_c, inv_row, pek, w1k, w2k, pev, w1v, w2vt)


def _nsa_kernel(qt_ref, kc_ref, vct_ref, ksl_ref, vslt_ref, kw_ref, vwt_ref, gt_ref, map_ref,
                out_ref, sel_ref, m_ref, l_ref, acc_ref):
    tq = qt_ref.shape[2]
    n_cmp = kc_ref.shape[2]
    n_slc = map_ref.shape[0]
    i = pl.program_id(2)
    s0 = i * tq
    t = s0 + lax.broadcasted_iota(jnp.int32, (1, tq), 1)

    def q_head(r):
        return qt_ref[0, r * HEAD_DIM:(r + 1) * HEAD_DIM, :]

    cmp_end = lax.broadcasted_iota(jnp.int32, (n_cmp, tq), 0) * CMP_STRIDE + (CMP_BLOCK - 1)
    cmask = cmp_end <= t
    kc = kc_ref[0, 0]
    vct = vct_ref[0, 0]
    psum = jnp.zeros((n_cmp, tq), F32)
    o_cmp = []
    for r in range(Q_PER_KV):
        s = jnp.where(cmask, _dot(kc, q_head(r)), NEG)
        e = jnp.where(cmask, jnp.exp(s - jnp.max(s, axis=0, keepdims=True)), 0.0)
        p = e * (1.0 / jnp.maximum(jnp.sum(e, axis=0, keepdims=True), 1e-30))
        psum = psum + p
        o_cmp.append(_dot(vct, p.astype(BF16)))

    imp = jnp.dot(map_ref[...], psum, preferred_element_type=F32, precision=lax.Precision.HIGHEST)
    blk = lax.broadcasted_iota(jnp.int32, (n_slc, tq), 0)
    cur = jnp.right_shift(t, SLC_BLOCK.bit_length() - 1)
    forced = (blk == 0) | (blk == cur) | (blk == cur - 1)
    imp = jnp.where(forced, FORCE, jnp.where(blk * SLC_BLOCK > t, -FORCE, imp))
    rank = jnp.zeros((n_slc, tq), jnp.int32)
    for j in range(n_slc):
        vj = imp[j:j + 1, :]
        rank = rank + jnp.where(blk > j, jnp.where(vj >= imp, 1, 0), jnp.where(vj > imp, 1, 0))
    sel_ref[...] = jnp.where(rank < min(N_SELECT, n_slc), 0.0, NEG)

    def sweep(slot, k_ref, vt_ref, lo, hi, bias_fn):
        for r in range(Q_PER_KV):
            m_ref[slot, r] = jnp.full((1, tq), NEG, F32)
            l_ref[slot, r] = jnp.zeros((1, tq), F32)
            acc_ref[slot, r] = jnp.zeros((HEAD_DIM, tq), F32)

        def body(kt, carry):
            k0 = pl.multiple_of(kt * TK, TK)
            k_t = k_ref[0, 0, pl.ds(k0, TK), :]
            v_t = vt_ref[0, 0, :, pl.ds(k0, TK)]
            kpos = k0 + lax.broadcasted_iota(jnp.int32, (TK, tq), 0)
            bias = bias_fn(kt, kpos)
            for r in range(Q_PER_KV):
                s = _dot(k_t, q_head(r)) + bias
                m_old = m_ref[slot, r]
                m_new = jnp.maximum(m_old, jnp.max(s, axis=0, keepdims=True))
                alpha = jnp.exp(m_old - m_new)
                p = jnp.exp(s - m_new)
                l_ref[slot, r] = alpha * l_ref[slot, r] + jnp.sum(p, axis=0, keepdims=True)
                acc_ref[slot, r] = alpha * acc_ref[slot, r] + _dot(v_t, p.astype(BF16))
                m_ref[slot, r] = m_new
            return carry

        lax.fori_loop(lo, hi, body, 0)

    def sel_bias(kt, kpos):
        b0 = kt * (TK // SLC_BLOCK)
        rows = [jnp.broadcast_to(sel_ref[pl.ds(b0 + j, 1), :], (SLC_BLOCK, tq)) for j in range(TK // SLC_BLOCK)]
        return jnp.where(kpos <= t, jnp.concatenate(rows, axis=0), NEG)

    def win_bias(kt, kpos):
        return jnp.where(kpos <= t, jnp.where(kpos > t - WINDOW, 0.0, NEG), NEG)

    sweep(0, ksl_ref, vslt_ref, 0, (s0 + tq) // TK, sel_bias)
    sweep(1, kw_ref, vwt_ref, jnp.maximum(s0 - WINDOW, 0) // TK, (s0 + tq) // TK, win_bias)

    gates = gt_ref[0]
    for r in range(Q_PER_KV):
        o_slc = acc_ref[0, r] * (1.0 / jnp.maximum(l_ref[0, r], 1e-30))
        o_win = acc_ref[1, r] * (1.0 / jnp.maximum(l_ref[1, r], 1e-30))
        out_ref[0, r * HEAD_DIM:(r + 1) * HEAD_DIM, :] = (
            gates[r:r + 1, :] * o_cmp[r]
            + gates[Q_PER_KV + r:Q_PER_KV + r + 1, :] * o_slc
            + gates[2 * Q_PER_KV + r:2 * Q_PER_KV + r + 1, :] * o_win)


def _nsa(qt, kcmp, vcmpt, ksl, vslt, kw, vwt, gt, slc_map_t):
    b, _, s = qt.shape
    tq = min(TQ, s)
    n_cmp = kcmp.shape[2]
    n_slc = s // SLC_BLOCK
    assert tq % TK == 0 and s % tq == 0 and WINDOW % TK == 0
    per_group = lambda shape: pl.BlockSpec((1, 1) + shape, lambda i, j, k: (i, j, 0, 0))
    return pl.pallas_call(
        _nsa_kernel,
        grid=(b, KV_GROUPS, s // tq),
        in_specs=[
            pl.BlockSpec((1, GROUP_WIDTH, tq), lambda i, j, k: (i, j, k)),
            per_group((n_cmp, HEAD_DIM)),
            per_group((HEAD_DIM, n_cmp)),
            per_group((s, HEAD_DIM)),
            pl.BlockSpec((1, 1, HEAD_DIM, s), lambda i, j, k: (i, 0, j, 0)),
            per_group((s, HEAD_DIM)),
            pl.BlockSpec((1, 1, HEAD_DIM, s), lambda i, j, k: (i, 1, j, 0)),
            pl.BlockSpec((1, GATE_ROWS, tq), lambda i, j, k: (i, j, k)),
            _whole((n_slc, n_cmp)),
        ],
        out_specs=pl.BlockSpec((1, GROUP_WIDTH, tq), lambda i, j, k: (i, j, k)),
        out_shape=jax.ShapeDtypeStruct((b, ATT_WIDTH, s), F32),
        scratch_shapes=[
            pltpu.VMEM((n_slc, tq), F32),
            pltpu.VMEM((2, Q_PER_KV, 1, tq), F32),
            pltpu.VMEM((2, Q_PER_KV, 1, tq), F32),
            pltpu.VMEM((2, Q_PER_KV, HEAD_DIM, tq), F32),
        ],
        compiler_params=_params("arbitrary", "arbitrary", "arbitrary"),
        name="nsa_attention",
    )(qt, kcmp, vcmpt, ksl, vslt, kw, vwt, gt, slc_map_t)


def _mem_kv_kernel(mem_ref, g_ref, wk_ref, wvt_ref, kx_ref, vxt_ref):
    mn = _rms_rows(mem_ref[0], g_ref[...]).astype(BF16)
    kx_ref[0] = _dot(mn, wk_ref[...]).astype(kx_ref.dtype)
    vxt_ref[0] = _dot_nt(wvt_ref[...], mn).astype(vxt_ref.dtype)


def _mem_kv(mem, g_mem, w_k, w_vt):
    b, m, _ = mem.shape
    return pl.pallas_call(
        _mem_kv_kernel,
        grid=(b,),
        in_specs=[pl.BlockSpec((1, m, D_MODEL), lambda i: (i, 0, 0)), _whole((1, D_MODEL)),
                  _whole((D_MODEL, D_MODEL)), _whole((D_MODEL, D_MODEL))],
        out_specs=(pl.BlockSpec((1, m, D_MODEL), lambda i: (i, 0, 0)),
                   pl.BlockSpec((1, D_MODEL, m), lambda i: (i, 0, 0))),
        out_shape=(jax.ShapeDtypeStruct((b, m, D_MODEL), BF16),
                   jax.ShapeDtypeStruct((b, D_MODEL, m), BF16)),
        compiler_params=_params("arbitrary"),
        name="mem_kv",
    )(mem, g_mem, w_k, w_vt)


def _mix_xattn_kernel(x_ref, yconv_ref, yatt_ref, gatt_ref, woa_ref, wob_ref, gpost_ref, gxpre_ref,
                      wqt_ref, kx_ref, vxt_ref, wo_ref, gxpost_ref, out_ref):
    ya = yatt_ref[0]
    yan = ya * lax.rsqrt(jnp.mean(ya * ya, axis=0, keepdims=True) + EPS) * gatt_ref[...]
    mixed = _dot(yconv_ref[0], woa_ref[...]) + _dot_tn(yan.astype(BF16), wob_ref[...])
    h1 = x_ref[0] + _rms_rows(mixed, gpost_ref[...])

    hn = _rms_rows(h1, gxpre_ref[...]).astype(BF16)
    qxt = (_dot_nt(wqt_ref[...], hn) * (X_HEAD_DIM ** -0.5)).astype(BF16)
    heads = []
    for h in range(X_HEADS):
        lo, hi = h * X_HEAD_DIM, (h + 1) * X_HEAD_DIM
        s = _dot(kx_ref[0, :, lo:hi], qxt[lo:hi, :])
        e = jnp.exp(s - jnp.max(s, axis=0, keepdims=True))
        o = _dot(vxt_ref[0, lo:hi, :], e.astype(BF16))
        heads.append(o * (1.0 / jnp.sum(e, axis=0, keepdims=True)))
    oxt = jnp.concatenate(heads, axis=0).astype(BF16)
    out_ref[0] = h1 + _rms_rows(_dot_tn(oxt, wo_ref[...]), gxpost_ref[...])


def _mix_xattn(x, yconv, yatt_t, g_att, w_out_a, w_out_b, g_post, g_xpre, w_qt, kx, vxt, w_o, g_xpost):
    b, s, _ = x.shape
    m = kx.shape[1]
    tm = min(TM_MIX, s)
    return pl.pallas_call(
        _mix_xattn_kernel,
        grid=(b, s // tm),
        in_specs=[
            pl.BlockSpec((1, tm, D_MODEL), lambda i, j: (i, j, 0)),
            pl.BlockSpec((1, tm, CONV_WIDTH), lambda i, j: (i, j, 0)),
            pl.BlockSpec((1, ATT_WIDTH, tm), lambda i, j: (i, 0, j)),
            _whole((ATT_WIDTH, 1)),
            _whole((CONV_WIDTH, D_MODEL)), _whole((ATT_WIDTH, D_MODEL)),
            _whole((1, D_MODEL)), _whole((1, D_MODEL)),
            _whole((D_MODEL, D_MODEL)),
            pl.BlockSpec((1, m, D_MODEL), lambda i, j: (i, 0, 0)),
            pl.BlockSpec((1, D_MODEL, m), lambda i, j: (i, 0, 0)),
            _whole((D_MODEL, D_MODEL)), _whole((1, D_MODEL)),
        ],
        out_specs=pl.BlockSpec((1, tm, D_MODEL), lambda i, j: (i, j, 0)),
        out_shape=jax.ShapeDtypeStruct((b, s, D_MODEL), F32),
        compiler_params=_params("arbitrary", "arbitrary"),
        name="mix_xattn",
    )(x, yconv, yatt_t, g_att, w_out_a, w_out_b, g_post, g_xpre, w_qt, kx, vxt, w_o, g_xpost)


def _ffn_kernel(h_ref, gpre_ref, wgu_ref, wdown_ref, gpost_ref, out_ref):
    h = h_ref[...]
    gu = _dot(_rms_rows(h, gpre_ref[...]).astype(BF16), wgu_ref[...])
    g, up = gu[:, 0:D_FF], gu[:, D_FF:2 * D_FF]
    act = (g * jax.nn.sigmoid(g) * up).astype(BF16)
    out_ref[...] = h + _rms_rows(_dot(act, wdown_ref[...]), gpost_ref[...])


def _ffn(h, g_pre, w_gu, w_down, g_post):
    n, _ = h.shape
    tm = min(TM_FFN, n)
    return pl.pallas_call(
        _ffn_kernel,
        grid=(n // tm,),
        in_specs=[pl.BlockSpec((tm, D_MODEL), lambda i: (i, 0)), _whole((1, D_MODEL)),
                  _whole((D_MODEL, 2 * D_FF)), _whole((D_FF, D_MODEL)), _whole((1, D_MODEL))],
        out_specs=pl.BlockSpec((tm, D_MODEL), lambda i: (i, 0)),
        out_shape=jax.ShapeDtypeStruct((n, D_MODEL), F32),
        compiler_params=_params("arbitrary"),
        name="ffn",
    )(h, g_pre, w_gu, w_down, g_post)


def _slc_map_t(s):
    n_slab = s // CMP_STRIDE
    ci = np.arange(n_slab)[None, :] * CMP_STRIDE
    sj = np.arange(s // SLC_BLOCK)[:, None] * SLC_BLOCK
    ov = np.clip(np.minimum(ci + CMP_BLOCK, sj + SLC_BLOCK) - np.maximum(ci, sj), 0, None)
    ov[:, n_slab - 1] = 0
    return jnp.asarray(ov / CMP_BLOCK, dtype=F32)


def kernel(x, mem, positions, norm_mix_pre, w_in, conv_w, pe_kc, w1_kc, w2_kc, pe_vc, w1_vc, w2_vc,
           norm_conv_out, norm_attn_out, w_out, norm_mix_post, norm_x_pre, norm_mem, w_q_x, w_kv_x,
           w_o_x, norm_x_post, norm_ffn_pre, w_gate_up, w_down, norm_ffn_post):
    b, s, _ = x.shape
    row = lambda v: v.reshape(1, -1).astype(F32)

    cuts = np.cumsum([0, CONV_WIDTH, CONV_WIDTH, CONV_WIDTH, ATT_WIDTH] + [KV_WIDTH] * 6 + [3 * ATT_HEADS])
    col = lambda k: w_in[:, cuts[k]:cuts[k + 1]]
    w_row = jnp.concatenate([col(0), col(1), col(2), col(4), col(5)], axis=1).astype(BF16)
    gate_cols = np.zeros((KV_GROUPS * GATE_ROWS,), np.int32)
    gate_live = np.zeros((KV_GROUPS * GATE_ROWS,), np.float32)
    for g in range(KV_GROUPS):
        for c in range(3):
            for r in range(Q_PER_KV):
                gate_cols[g * GATE_ROWS + c * Q_PER_KV + r] = (g * Q_PER_KV + r) * 3 + c
                gate_live[g * GATE_ROWS + c * Q_PER_KV + r] = 1.0
    w_gate = col(10)[:, gate_cols] * gate_live[None, :]
    w_t = jnp.concatenate([col(3), col(6), col(8), col(7), col(9), w_gate], axis=1).T.astype(BF16)

    inv = ROPE_THETA ** (-jnp.arange(HALF, dtype=F32) / HALF)
    yconv, kvc, qt, ksl, kw, vt, gt = _in_proj(
        x, positions.reshape(b, 1, s), inv.reshape(HALF, 1), row(norm_mix_pre), w_row, w_t,
        conv_w.astype(F32), row(norm_conv_out))

    n_slab = s // CMP_STRIDE
    slabs = kvc.reshape(b, n_slab, CMP_STRIDE, 2, KV_GROUPS, HEAD_DIM).transpose(3, 0, 4, 1, 2, 5)
    slabs = slabs.reshape(2, b, KV_GROUPS, n_slab, CMP_STRIDE * HEAD_DIM)
    pos_c = positions[:, np.minimum(np.arange(n_slab) * CMP_STRIDE + CMP_BLOCK - 1, s - 1)]
    kcmp, vcmpt = _compress(
        slabs[0], slabs[1], pos_c.reshape(b, n_slab, 1), inv.reshape(1, HALF),
        pe_kc.reshape(1, -1), w1_kc.astype(BF16), w2_kc.astype(BF16),
        pe_vc.reshape(1, -1), w1_vc.astype(BF16), w2_vc.T.astype(BF16))

    vt4 = vt.reshape(b, 2, KV_WIDTH, s)
    yatt_t = _nsa(qt, kcmp, vcmpt, ksl, vt4, kw, vt4, gt, _slc_map_t(s))

    kx, vxt = _mem_kv(mem, row(norm_mem), w_kv_x[:, :D_MODEL].astype(BF16), w_kv_x[:, D_MODEL:].T.astype(BF16))
    h2 = _mix_xattn(
        x, yconv, yatt_t, norm_attn_out.reshape(-1, 1).astype(F32),
        w_out[:CONV_WIDTH].astype(BF16), w_out[CONV_WIDTH:].astype(BF16),
        row(norm_mix_post), row(norm_x_pre), w_q_x.T.astype(BF16), kx, vxt, w_o_x.astype(BF16), row(norm_x_post))

    out = _ffn(h2.reshape(b * s, D_MODEL), row(norm_ffn_pre), w_gate_up.astype(BF16), w_down.astype(BF16),
               row(norm_ffn_post))
    return out.reshape(b, s, D_MODEL)
```

```python
import math

import numpy as np
import jax
import jax.numpy as jnp
from jax import lax
from jax.experimental import pallas as pl
from jax.experimental.pallas import tpu as pltpu

D_MODEL = 1024
CONV_WIDTH = 512
CONV_K = 3
ATT_HEADS = 8
HEAD_DIM = 64
HALF = HEAD_DIM // 2
ATT_WIDTH = ATT_HEADS * HEAD_DIM
KV_GROUPS = 2
Q_PER_KV = ATT_HEADS // KV_GROUPS
KV_WIDTH = KV_GROUPS * HEAD_DIM
GROUP_WIDTH = Q_PER_KV * HEAD_DIM
CMP_BLOCK = 32
CMP_STRIDE = 16
CMP_HIDDEN = 256
SLC_BLOCK = 64
N_SELECT = 16
WINDOW = 512
ROPE_THETA = 10000.0
X_HEADS = 4
X_HEAD_DIM = D_MODEL // X_HEADS
D_FF = 2816
EPS = 1e-6
FORCE = 1e4
NEG = -1e30
GATE_ROWS = 16
K_AUG = 2 * HEAD_DIM
V_AUG = HEAD_DIM + 16
WIN, SEL = 0, 1

TM_PROJ = 512
TM_MIX = 512
TM_FFN = 256
TQ = 256
TK = 256
VMEM_LIMIT = 56 * 1024 * 1024

F32 = jnp.float32
BF16 = jnp.bfloat16

_NT = (((1,), (1,)), ((), ()))
_TN = (((0,), (0,)), ((), ()))


def _dot(a, b):
    return jnp.dot(a, b, preferred_element_type=F32)


def _dot_nt(a, b):
    return lax.dot_general(a, b, _NT, preferred_element_type=F32)


def _dot_tn(a, b):
    return lax.dot_general(a, b, _TN, preferred_element_type=F32)


def _rms_rows(x, g):
    return x * lax.rsqrt(jnp.mean(x * x, axis=-1, keepdims=True) + EPS) * g


def _params(*sem):
    return pltpu.CompilerParams(dimension_semantics=sem, vmem_limit_bytes=VMEM_LIMIT)


def _whole(shape):
    nd = len(shape)
    return pl.BlockSpec(shape, lambda *_: (0,) * nd)


def _in_proj_kernel(x_ref, pos_ref, inv_ref, g_ref, wrow_ref, wt_ref, convw_ref, gconv_ref,
                    yconv_ref, kvc_ref, qt_ref, kaug_ref, vaug_ref, gt_ref, carry_ref):
    tm = x_ref.shape[1]
    u = _rms_rows(x_ref[0], g_ref[...]).astype(BF16)
    prow = _dot(u, wrow_ref[...])
    pt = _dot_nt(wt_ref[...], u)

    bg = prow[:, 0:CONV_WIDTH]
    z = prow[:, CONV_WIDTH:2 * CONV_WIDTH] * prow[:, 2 * CONV_WIDTH:3 * CONV_WIDTH]

    @pl.when(pl.program_id(1) == 0)
    def _():
        carry_ref[...] = jnp.zeros_like(carry_ref)

    prev = carry_ref[...]
    row = lax.broadcasted_iota(jnp.int32, (tm, CONV_WIDTH), 0)
    z1 = jnp.where(row == 0, prev[7:8, :], pltpu.roll(z, 1, axis=0))
    z2 = jnp.where(row == 0, prev[6:7, :], jnp.where(row == 1, prev[7:8, :], pltpu.roll(z, 2, axis=0)))
    carry_ref[...] = z[tm - 8:tm, :]
    cw = convw_ref[...]
    yc = bg * (cw[0:1, :] * z2 + cw[1:2, :] * z1 + cw[2:3, :] * z)
    yconv_ref[0] = _rms_rows(yc, gconv_ref[...]).astype(yconv_ref.dtype)

    kvc_ref[0] = prow[:, 3 * CONV_WIDTH:3 * CONV_WIDTH + 2 * KV_WIDTH]

    ang = pos_ref[0].astype(F32) * inv_ref[...]
    cos, sin = jnp.cos(ang), jnp.sin(ang)

    def rope_t(base):
        x1 = pt[base:base + HALF, :]
        x2 = pt[base + HALF:base + HEAD_DIM, :]
        return x1 * cos - x2 * sin, x2 * cos + x1 * sin

    scale = HEAD_DIM ** -0.5 * math.log2(math.e)
    for h in range(ATT_HEADS):
        r1, r2 = rope_t(h * HEAD_DIM)
        qt_ref[0, h * HEAD_DIM:h * HEAD_DIM + HALF, :] = (r1 * scale).astype(qt_ref.dtype)
        qt_ref[0, h * HEAD_DIM + HALF:(h + 1) * HEAD_DIM, :] = (r2 * scale).astype(qt_ref.dtype)

    tok = pl.program_id(1) * tm + lax.broadcasted_iota(jnp.int32, (tm, HEAD_DIM), 0)
    blk_col = lax.broadcasted_iota(jnp.int32, (tm, HEAD_DIM), 1)
    onehot = jnp.where(jnp.right_shift(tok, SLC_BLOCK.bit_length() - 1) == blk_col, 1.0, 0.0)
    for br in (WIN, SEL):
        base = ATT_WIDTH + br * KV_WIDTH
        parts = []
        for g in range(KV_GROUPS):
            parts.extend(rope_t(base + g * HEAD_DIM))
        k_rows = jnp.concatenate(parts, axis=0).T
        extra = onehot if br == SEL else jnp.zeros_like(onehot)
        for g in range(KV_GROUPS):
            k_g = k_rows[:, g * HEAD_DIM:(g + 1) * HEAD_DIM]
            kaug_ref[0, g, br] = jnp.concatenate([k_g, extra], axis=-1).astype(kaug_ref.dtype)

    v0 = ATT_WIDTH + 2 * KV_WIDTH
    ones_rows = jnp.where(lax.broadcasted_iota(jnp.int32, (V_AUG - HEAD_DIM, tm), 0) == 0, 1.0, 0.0)
    for br in (WIN, SEL):
        for g in range(KV_GROUPS):
            lo = v0 + br * KV_WIDTH + g * HEAD_DIM
            vaug_ref[0, g, br] = jnp.concatenate([pt[lo:lo + HEAD_DIM, :], ones_rows], axis=0).astype(vaug_ref.dtype)

    g0 = v0 + 2 * KV_WIDTH
    gt_ref[0] = jax.nn.sigmoid(pt[g0:g0 + KV_GROUPS * GATE_ROWS, :])


def _in_proj(x, pos_row, inv_col, g_pre, w_row, w_t, conv_w, g_conv):
    b, s, _ = x.shape
    tm = min(TM_PROJ, s)
    n_row, n_t = w_row.shape[1], w_t.shape[0]
    out_shape = (
        jax.ShapeDtypeStruct((b, s, CONV_WIDTH), BF16),
        jax.ShapeDtypeStruct((b, s, 2 * KV_WIDTH), F32),
        jax.ShapeDtypeStruct((b, ATT_WIDTH, s), BF16),
        jax.ShapeDtypeStruct((b, KV_GROUPS, 2, s, K_AUG), BF16),
        jax.ShapeDtypeStruct((b, KV_GROUPS, 2, V_AUG, s), BF16),
        jax.ShapeDtypeStruct((b, KV_GROUPS * GATE_ROWS, s), F32),
    )
    return pl.pallas_call(
        _in_proj_kernel,
        grid=(b, s // tm),
        in_specs=[
            pl.BlockSpec((1, tm, D_MODEL), lambda i, j: (i, j, 0)),
            pl.BlockSpec((1, 1, tm), lambda i, j: (i, 0, j)),
            _whole((HALF, 1)),
            _whole((1, D_MODEL)),
            _whole((D_MODEL, n_row)),
            _whole((n_t, D_MODEL)),
            _whole((CONV_K, CONV_WIDTH)),
            _whole((1, CONV_WIDTH)),
        ],
        out_specs=(
            pl.BlockSpec((1, tm, CONV_WIDTH), lambda i, j: (i, j, 0)),
            pl.BlockSpec((1, tm, 2 * KV_WIDTH), lambda i, j: (i, j, 0)),
            pl.BlockSpec((1, ATT_WIDTH, tm), lambda i, j: (i, 0, j)),
            pl.BlockSpec((1, KV_GROUPS, 2, tm, K_AUG), lambda i, j: (i, 0, 0, j, 0)),
            pl.BlockSpec((1, KV_GROUPS, 2, V_AUG, tm), lambda i, j: (i, 0, 0, 0, j)),
            pl.BlockSpec((1, KV_GROUPS * GATE_ROWS, tm), lambda i, j: (i, 0, j)),
        ),
        out_shape=out_shape,
        scratch_shapes=[pltpu.VMEM((8, CONV_WIDTH), F32)],
        compiler_params=_params("arbitrary", "arbitrary"),
        name="in_proj",
    )(x, pos_row, inv_col, g_pre, w_row, w_t, conv_w, g_conv)


def _compress_kernel(rk_ref, rv_ref, pos_ref, inv_ref, pek_ref, w1k_ref, w2k_ref, pev_ref, w1v_ref, w2vt_ref,
                     kc_ref, vct_ref):
    half_in = CMP_STRIDE * HEAD_DIM

    def hidden(r_ref, pe_ref, w1_ref):
        r = r_ref[0, 0].astype(BF16)
        a = _dot(r, w1_ref[0:half_in, :])
        bshift = pltpu.roll(_dot(r, w1_ref[half_in:2 * half_in, :]), r.shape[0] - 1, axis=0)
        pe = jnp.broadcast_to(pe_ref[...], (8, 2 * half_in)).astype(BF16)
        h = a + bshift + _dot(pe, w1_ref[...])[0:1, :]
        return (h * jax.nn.sigmoid(h)).astype(BF16)

    kc = _dot(hidden(rk_ref, pek_ref, w1k_ref), w2k_ref[...])
    ang = pos_ref[0].astype(F32) * inv_ref[...]
    cos, sin = jnp.cos(ang), jnp.sin(ang)
    x1, x2 = kc[:, 0:HALF], kc[:, HALF:HEAD_DIM]
    kc_ref[0, 0] = jnp.concatenate([x1 * cos - x2 * sin, x2 * cos + x1 * sin], axis=-1).astype(kc_ref.dtype)
    vct_ref[0, 0] = _dot_nt(w2vt_ref[...], hidden(rv_ref, pev_ref, w1v_ref)).astype(vct_ref.dtype)


def _compress(rk, rv, pos_c, inv_row, pek, w1k, w2k, pev, w1v, w2vt):
    b, g, n_slab, width = rk.shape
    slab = pl.BlockSpec((1, 1, n_slab, width), lambda i, j: (i, j, 0, 0))
    return pl.pallas_call(
        _compress_kernel,
        grid=(b, g),
        in_specs=[
            slab, slab,
            pl.BlockSpec((1, n_slab, 1), lambda i, j: (i, 0, 0)),
            _whole((1, HALF)),
            _whole(pek.shape), _whole(w1k.shape), _whole(w2k.shape),
            _whole(pev.shape), _whole(w1v.shape), _whole(w2vt.shape),
        ],
        out_specs=(
            pl.BlockSpec((1, 1, n_slab, HEAD_DIM), lambda i, j: (i, j, 0, 0)),
            pl.BlockSpec((1, 1, HEAD_DIM, n_slab), lambda i, j: (i, j, 0, 0)),
        ),
        out_shape=(
            jax.ShapeDtypeStruct((b, g, n_slab, HEAD_DIM), BF16),
            jax.ShapeDtypeStruct((b, g, HEAD_DIM, n_slab), BF16),
        ),
        compiler_params=_params("arbitrary", "arbitrary"),
        name="compress",
    )(rk, rv, pos_c, inv_row, pek, w1k, w2k, pev, w1v, w2vt)


def _nsa_kernel(qt_ref, kc_ref, vct_ref, kaug_ref, vaug_ref, gt_ref, map_ref, tri_ref,
                out_ref, rhs_ref, s_ref, m_ref, acc_ref):
    tq = qt_ref.shape[2]
    n_cmp = kc_ref.shape[2]
    n_slc = map_ref.shape[0]
    i = pl.program_id(2)
    s0 = i * tq
    t = s0 + lax.broadcasted_iota(jnp.int32, (1, tq), 1)
    row_of = lambda r: slice(r * HEAD_DIM, (r + 1) * HEAD_DIM)

    n_win = jnp.minimum(i, WINDOW // TK) + 1
    n_steps = n_win + i + 1

    def step_info(st):
        is_sel = st >= n_win
        tile = jnp.where(is_sel, st - n_win, i - (n_win - 1) + st)
        bias = jnp.where(tile == i, 1, jnp.where(is_sel, 0, jnp.where(tile == i - WINDOW // TK, 2, 0)))
        return jnp.where(is_sel, SEL, WIN), tile, bias

    def issue_scores(st, slot):
        br, tile, _ = step_info(st)
        k_t = kaug_ref[0, 0, br, pl.ds(pl.multiple_of(tile * TK, TK), TK), :]
        for r in range(Q_PER_KV):
            s_ref[slot, r] = _dot(k_t, rhs_ref[br, r])

    def softmax_step(st, slot):
        br, tile, bias_idx = step_info(st)
        v_t = vaug_ref[0, 0, br, :, pl.ds(pl.multiple_of(tile * TK, TK), TK)]
        bias = tri_ref[bias_idx]
        for r in range(Q_PER_KV):
            s = s_ref[slot, r] + bias
            m_old = m_ref[r]
            m_new = jnp.maximum(m_old, jnp.max(s, axis=0, keepdims=True))
            p = jnp.exp2(s - m_new).astype(BF16)
            acc_ref[r] = jnp.exp2(m_old - m_new) * acc_ref[r] + _dot(v_t, p)
            m_ref[r] = m_new

    def reset_state():
        for r in range(Q_PER_KV):
            m_ref[r] = jnp.full((1, tq), NEG, F32)
            acc_ref[r] = jnp.zeros((V_AUG, tq), F32)

    def flush_state(gate_row0):
        for r in range(Q_PER_KV):
            acc = acc_ref[r]
            o = acc[0:HEAD_DIM, :] * (1.0 / jnp.maximum(acc[HEAD_DIM:HEAD_DIM + 1, :], 1e-30))
            out_ref[0, row_of(r), :] = out_ref[0, row_of(r), :] + gt_ref[0, gate_row0 + r:gate_row0 + r + 1, :] * o

    zeros_half = jnp.zeros((K_AUG - HEAD_DIM, tq), BF16)
    for r in range(Q_PER_KV):
        rhs_ref[WIN, r, 0:HEAD_DIM, :] = qt_ref[0, row_of(r), :]
        rhs_ref[WIN, r, HEAD_DIM:K_AUG, :] = zeros_half
        rhs_ref[SEL, r, 0:HEAD_DIM, :] = qt_ref[0, row_of(r), :]
    reset_state()
    issue_scores(0, 0)

    cmp_end = lax.broadcasted_iota(jnp.int32, (n_cmp, tq), 0) * CMP_STRIDE + (CMP_BLOCK - 1)
    cmask = cmp_end <= t
    kc = kc_ref[0, 0]
    vct = vct_ref[0, 0]
    psum = jnp.zeros((n_cmp, tq), F32)
    for r in range(Q_PER_KV):
        s = jnp.where(cmask, _dot(kc, qt_ref[0, row_of(r), :]), NEG)
        e = jnp.where(cmask, jnp.exp2(s - jnp.max(s, axis=0, keepdims=True)), 0.0)
        p = e * (1.0 / jnp.maximum(jnp.sum(e, axis=0, keepdims=True), 1e-30))
        psum = psum + p
        out_ref[0, row_of(r), :] = gt_ref[0, r:r + 1, :] * _dot(vct, p.astype(BF16))

    imp = jnp.dot(map_ref[...], psum, preferred_element_type=F32, precision=lax.Precision.HIGHEST)
    blk = lax.broadcasted_iota(jnp.int32, (n_slc, tq), 0)
    cur = jnp.right_shift(t, SLC_BLOCK.bit_length() - 1)
    forced = (blk == 0) | (blk == cur) | (blk == cur - 1)
    imp = jnp.where(forced, FORCE, jnp.where(blk * SLC_BLOCK > t, -FORCE, imp))
    rank = jnp.zeros((n_slc, tq), jnp.int32)
    for j in range(n_slc):
        vj = imp[j:j + 1, :]
        rank = rank + jnp.where(blk > j, jnp.where(vj >= imp, 1, 0), jnp.where(vj > imp, 1, 0))
    sel = jnp.where(rank < min(N_SELECT, n_slc), 0.0, NEG).astype(BF16)
    if n_slc < K_AUG - HEAD_DIM:
        sel = jnp.concatenate([sel, jnp.zeros((K_AUG - HEAD_DIM - n_slc, tq), BF16)], axis=0)
    for r in range(Q_PER_KV):
        rhs_ref[SEL, r, HEAD_DIM:K_AUG, :] = sel

    def body(st, carry):
        @pl.when(st == n_win)
        def _():
            flush_state(2 * Q_PER_KV)
            reset_state()

        nxt = jnp.minimum(st + 1, n_steps - 1)
        for parity in (0, 1):
            @pl.when(lax.rem(st, 2) == parity)
            def _():
                issue_scores(nxt, 1 - parity)
                softmax_step(st, parity)
        return carry

    lax.fori_loop(0, n_steps, body, 0)
    flush_state(Q_PER_KV)


def _nsa(qt, kcmp, vcmpt, kaug, vaug, gt, slc_map_t, tri):
    b, _, s = qt.shape
    tq = min(TQ, s)
    n_cmp = kcmp.shape[2]
    n_slc = s // SLC_BLOCK
    assert tq == TK and s % tq == 0 and WINDOW % TK == 0 and n_slc <= K_AUG - HEAD_DIM
    per_group = lambda shape: pl.BlockSpec((1, 1) + shape, lambda i, j, k: (i, j) + (0,) * len(shape))
    return pl.pallas_call(
        _nsa_kernel,
        grid=(b, KV_GROUPS, s // tq),
        in_specs=[
            pl.BlockSpec((1, GROUP_WIDTH, tq), lambda i, j, k: (i, j, k)),
            per_group((n_cmp, HEAD_DIM)),
            per_group((HEAD_DIM, n_cmp)),
            per_group((2, s, K_AUG)),
            per_group((2, V_AUG, s)),
            pl.BlockSpec((1, GATE_ROWS, tq), lambda i, j, k: (i, j, k)),
            _whole((n_slc, n_cmp)),
            _whole((3, TK, tq)),
        ],
        out_specs=pl.BlockSpec((1, GROUP_WIDTH, tq), lambda i, j, k: (i, j, k)),
        out_shape=jax.ShapeDtypeStruct((b, ATT_WIDTH, s), F32),
        scratch_shapes=[
            pltpu.VMEM((2, Q_PER_KV, K_AUG, tq), BF16),
            pltpu.VMEM((2, Q_PER_KV, TK, tq), F32),
            pltpu.VMEM((Q_PER_KV, 1, tq), F32),
            pltpu.VMEM((Q_PER_KV, V_AUG, tq), F32),
        ],
        compiler_params=_params("arbitrary", "arbitrary", "arbitrary"),
        name="nsa_attention",
    )(qt, kcmp, vcmpt, kaug, vaug, gt, slc_map_t, tri)


def _mem_kv_kernel(mem_ref, g_ref, wk_ref, wvt_ref, kx_ref, vxt_ref):
    mn = _rms_rows(mem_ref[0], g_ref[...]).astype(BF16)
    kx_ref[0] = _dot(mn, wk_ref[...]).astype(kx_ref.dtype)
    vxt_ref[0] = _dot_nt(wvt_ref[...], mn).astype(vxt_ref.dtype)


def _mem_kv(mem, g_mem, w_k, w_vt):
    b, m, _ = mem.shape
    return pl.pallas_call(
        _mem_kv_kernel,
        grid=(b,),
        in_specs=[pl.BlockSpec((1, m, D_MODEL), lambda i: (i, 0, 0)), _whole((1, D_MODEL)),
                  _whole((D_MODEL, D_MODEL)), _whole((D_MODEL, D_MODEL))],
        out_specs=(pl.BlockSpec((1, m, D_MODEL), lambda i: (i, 0, 0)),
                   pl.BlockSpec((1, D_MODEL, m), lambda i: (i, 0, 0))),
        out_shape=(jax.ShapeDtypeStruct((b, m, D_MODEL), BF16),
                   jax.ShapeDtypeStruct((b, D_MODEL, m), BF16)),
        compiler_params=_params("arbitrary"),
        name="mem_kv",
    )(mem, g_mem, w_k, w_vt)


def _mix_xattn_kernel(x_ref, yconv_ref, yatt_ref, gatt_ref, woa_ref, wob_ref, gpost_ref, gxpre_ref,
                      wqt_ref, kx_ref, vxt_ref, wo_ref, gxpost_ref, out_ref):
    ya = yatt_ref[0]
    yan = ya * lax.rsqrt(jnp.mean(ya * ya, axis=0, keepdims=True) + EPS) * gatt_ref[...]
    mixed = _dot(yconv_ref[0], woa_ref[...]) + _dot_tn(yan.astype(BF16), wob_ref[...])
    h1 = x_ref[0] + _rms_rows(mixed, gpost_ref[...])

    hn = _rms_rows(h1, gxpre_ref[...]).astype(BF16)
    qxt = (_dot_nt(wqt_ref[...], hn) * (X_HEAD_DIM ** -0.5)).astype(BF16)
    heads = []
    for h in range(X_HEADS):
        lo, hi = h * X_HEAD_DIM, (h + 1) * X_HEAD_DIM
        s = _dot(kx_ref[0, :, lo:hi], qxt[lo:hi, :])
        e = jnp.exp(s - jnp.max(s, axis=0, keepdims=True))
        o = _dot(vxt_ref[0, lo:hi, :], e.astype(BF16))
        heads.append(o * (1.0 / jnp.sum(e, axis=0, keepdims=True)))
    oxt = jnp.concatenate(heads, axis=0).astype(BF16)
    out_ref[0] = h1 + _rms_rows(_dot_tn(oxt, wo_ref[...]), gxpost_ref[...])


def _mix_xattn(x, yconv, yatt_t, g_att, w_out_a, w_out_b, g_post, g_xpre, w_qt, kx, vxt, w_o, g_xpost):
    b, s, _ = x.shape
    m = kx.shape[1]
    tm = min(TM_MIX, s)
    return pl.pallas_call(
        _mix_xattn_kernel,
        grid=(b, s // tm),
        in_specs=[
            pl.BlockSpec((1, tm, D_MODEL), lambda i, j: (i, j, 0)),
            pl.BlockSpec((1, tm, CONV_WIDTH), lambda i, j: (i, j, 0)),
            pl.BlockSpec((1, ATT_WIDTH, tm), lambda i, j: (i, 0, j)),
            _whole((ATT_WIDTH, 1)),
            _whole((CONV_WIDTH, D_MODEL)), _whole((ATT_WIDTH, D_MODEL)),
            _whole((1, D_MODEL)), _whole((1, D_MODEL)),
            _whole((D_MODEL, D_MODEL)),
            pl.BlockSpec((1, m, D_MODEL), lambda i, j: (i, 0, 0)),
            pl.BlockSpec((1, D_MODEL, m), lambda i, j: (i, 0, 0)),
            _whole((D_MODEL, D_MODEL)), _whole((1, D_MODEL)),
        ],
        out_specs=pl.BlockSpec((1, tm, D_MODEL), lambda i, j: (i, j, 0)),
        out_shape=jax.ShapeDtypeStruct((b, s, D_MODEL), F32),
        compiler_params=_params("arbitrary", "arbitrary"),
        name="mix_xattn",
    )(x, yconv, yatt_t, g_att, w_out_a, w_out_b, g_post, g_xpre, w_qt, kx, vxt, w_o, g_xpost)


def _ffn_kernel(h_ref, gpre_ref, wgu_ref, wdown_ref, gpost_ref, out_ref):
    h = h_ref[...]
    gu = _dot(_rms_rows(h, gpre_ref[...]).astype(BF16), wgu_ref[...])
    g, up = gu[:, 0:D_FF], gu[:, D_FF:2 * D_FF]
    act = (g * jax.nn.sigmoid(g) * up).astype(BF16)
    out_ref[...] = h + _rms_rows(_dot(act, wdown_ref[...]), gpost_ref[...])


def _ffn(h, g_pre, w_gu, w_down, g_post):
    n, _ = h.shape
    tm = min(TM_FFN, n)
    return pl.pallas_call(
        _ffn_kernel,
        grid=(n // tm,),
        in_specs=[pl.BlockSpec((tm, D_MODEL), lambda i: (i, 0)), _whole((1, D_MODEL)),
                  _whole((D_MODEL, 2 * D_FF)), _whole((D_FF, D_MODEL)), _whole((1, D_MODEL))],
        out_specs=pl.BlockSpec((tm, D_MODEL), lambda i: (i, 0)),
        out_shape=jax.ShapeDtypeStruct((n, D_MODEL), F32),
        compiler_params=_params("arbitrary"),
        name="ffn",
    )(h, g_pre, w_gu, w_down, g_post)


def _slc_map_t(s):
    n_slab = s // CMP_STRIDE
    ci = np.arange(n_slab)[None, :] * CMP_STRIDE
    sj = np.arange(s // SLC_BLOCK)[:, None] * SLC_BLOCK
    ov = np.clip(np.minimum(ci + CMP_BLOCK, sj + SLC_BLOCK) - np.maximum(ci, sj), 0, None)
    ov[:, n_slab - 1] = 0
    return jnp.asarray(ov / CMP_BLOCK, dtype=F32)


def _tile_biases(tq):
    key = np.arange(TK)[:, None]
    qry = np.arange(tq)[None, :]
    none = np.zeros((TK, tq), np.float32)
    causal = np.where(key <= qry, 0.0, NEG).astype(np.float32)
    window_start = np.where(key > qry, 0.0, NEG).astype(np.float32)
    return jnp.asarray(np.stack([none, causal, window_start]))


def kernel(x, mem, positions, norm_mix_pre, w_in, conv_w, pe_kc, w1_kc, w2_kc, pe_vc, w1_vc, w2_vc,
           norm_conv_out, norm_attn_out, w_out, norm_mix_post, norm_x_pre, norm_mem, w_q_x, w_kv_x,
           w_o_x, norm_x_post, norm_ffn_pre, w_gate_up, w_down, norm_ffn_post):
    b, s, _ = x.shape
    row = lambda v: v.reshape(1, -1).astype(F32)

    cuts = np.cumsum([0, CONV_WIDTH, CONV_WIDTH, CONV_WIDTH, ATT_WIDTH] + [KV_WIDTH] * 6 + [3 * ATT_HEADS])
    col = lambda k: w_in[:, cuts[k]:cuts[k + 1]]
    w_row = jnp.concatenate([col(0), col(1), col(2), col(4), col(5)], axis=1).astype(BF16)
    gate_cols = np.zeros((KV_GROUPS * GATE_ROWS,), np.int32)
    gate_live = np.zeros((KV_GROUPS * GATE_ROWS,), np.float32)
    for g in range(KV_GROUPS):
        for c in range(3):
            for r in range(Q_PER_KV):
                gate_cols[g * GATE_ROWS + c * Q_PER_KV + r] = (g * Q_PER_KV + r) * 3 + c
                gate_live[g * GATE_ROWS + c * Q_PER_KV + r] = 1.0
    w_gate = col(10)[:, gate_cols] * gate_live[None, :]
    w_t = jnp.concatenate([col(3), col(8), col(6), col(9), col(7), w_gate], axis=1).T.astype(BF16)

    inv = ROPE_THETA ** (-jnp.arange(HALF, dtype=F32) / HALF)
    yconv, kvc, qt, kaug, vaug, gt = _in_proj(
        x, positions.reshape(b, 1, s), inv.reshape(HALF, 1), row(norm_mix_pre), w_row, w_t,
        conv_w.astype(F32), row(norm_conv_out))

    n_slab = s // CMP_STRIDE
    slabs = kvc.reshape(b, n_slab, CMP_STRIDE, 2, KV_GROUPS, HEAD_DIM).transpose(3, 0, 4, 1, 2, 5)
    slabs = slabs.reshape(2, b, KV_GROUPS, n_slab, CMP_STRIDE * HEAD_DIM)
    pos_c = positions[:, np.minimum(np.arange(n_slab) * CMP_STRIDE + CMP_BLOCK - 1, s - 1)]
    kcmp, vcmpt = _compress(
        slabs[0], slabs[1], pos_c.reshape(b, n_slab, 1), inv.reshape(1, HALF),
        pe_kc.reshape(1, -1), w1_kc.astype(BF16), w2_kc.astype(BF16),
        pe_vc.reshape(1, -1), w1_vc.astype(BF16), w2_vc.T.astype(BF16))

    yatt_t = _nsa(qt, kcmp, vcmpt, kaug, vaug, gt, _slc_map_t(s), _tile_biases(min(TQ, s)))

    kx, vxt = _mem_kv(mem, row(norm_mem), w_kv_x[:, :D_MODEL].astype(BF16), w_kv_x[:, D_MODEL:].T.astype(BF16))
    h2 = _mix_xattn(
        x, yconv, yatt_t, norm_attn_out.reshape(-1, 1).astype(F32),
        w_out[:CONV_WIDTH].astype(BF16), w_out[CONV_WIDTH:].astype(BF16),
        row(norm_mix_post), row(norm_x_pre), w_q_x.T.astype(BF16), kx, vxt, w_o_x.astype(BF16), row(norm_x_post))

    out = _ffn(h2.reshape(b * s, D_MODEL), row(norm_ffn_pre), w_gate_up.astype(BF16), w_down.astype(BF16),
               row(norm_ffn_post))
    return out.reshape(b, s, D_MODEL)
```

```python
import math

import numpy as np
import jax
import jax.numpy as jnp
from jax import lax
from jax.experimental import pallas as pl
from jax.experimental.pallas import tpu as pltpu

D_MODEL = 1024
CONV_WIDTH = 512
CONV_K = 3
ATT_HEADS = 8
HEAD_DIM = 64
HALF = HEAD_DIM // 2
ATT_WIDTH = ATT_HEADS * HEAD_DIM
KV_GROUPS = 2
Q_PER_KV = ATT_HEADS // KV_GROUPS
KV_WIDTH = KV_GROUPS * HEAD_DIM
GROUP_WIDTH = Q_PER_KV * HEAD_DIM
CMP_BLOCK = 32
CMP_STRIDE = 16
CMP_HIDDEN = 256
SLC_BLOCK = 64
N_SELECT = 16
WINDOW = 512
ROPE_THETA = 10000.0
X_HEADS = 4
X_HEAD_DIM = D_MODEL // X_HEADS
D_FF = 2816
EPS = 1e-6
FORCE = 1e4
NEG = -1e30
GATE_ROWS = 16
K_AUG = 2 * HEAD_DIM
V_AUG = HEAD_DIM + 16
WIN, SEL = 0, 1
_SLOT_WIN, _SLOT_CMP, _N_SLOTS = 2, 5, 6
_BIAS_NONE, _BIAS_CAUSAL, _BIAS_WINDOW_START, _BIAS_MASKED = 0, 1, 2, 3

TM_PROJ = 512
TM_MIX = 512
TM_FFN = 256
TQ = 256
TK = 256
VMEM_LIMIT = 56 * 1024 * 1024

F32 = jnp.float32
BF16 = jnp.bfloat16

_NT = (((1,), (1,)), ((), ()))
_TN = (((0,), (0,)), ((), ()))


def _dot(a, b):
    return jnp.dot(a, b, preferred_element_type=F32)


def _dot_nt(a, b):
    return lax.dot_general(a, b, _NT, preferred_element_type=F32)


def _dot_tn(a, b):
    return lax.dot_general(a, b, _TN, preferred_element_type=F32)


def _rms_rows(x, g):
    return x * lax.rsqrt(jnp.mean(x * x, axis=-1, keepdims=True) + EPS) * g


def _params(*sem):
    return pltpu.CompilerParams(dimension_semantics=sem, vmem_limit_bytes=VMEM_LIMIT)


def _whole(shape):
    nd = len(shape)
    return pl.BlockSpec(shape, lambda *_: (0,) * nd)


def _in_proj_kernel(x_ref, pos_ref, inv_ref, g_ref, wrow_ref, wt_ref, convw_ref, gconv_ref,
                    yconv_ref, kvc_ref, qt_ref, kaug_ref, vaug_ref, gt_ref, carry_ref):
    tm = x_ref.shape[1]
    u = _rms_rows(x_ref[0], g_ref[...]).astype(BF16)
    prow = _dot(u, wrow_ref[...])
    pt = _dot_nt(wt_ref[...], u)

    bg = prow[:, 0:CONV_WIDTH]
    z = prow[:, CONV_WIDTH:2 * CONV_WIDTH] * prow[:, 2 * CONV_WIDTH:3 * CONV_WIDTH]

    @pl.when(pl.program_id(1) == 0)
    def _():
        carry_ref[...] = jnp.zeros_like(carry_ref)

    prev = carry_ref[...]
    row = lax.broadcasted_iota(jnp.int32, (tm, CONV_WIDTH), 0)
    z1 = jnp.where(row == 0, prev[7:8, :], pltpu.roll(z, 1, axis=0))
    z2 = jnp.where(row == 0, prev[6:7, :], jnp.where(row == 1, prev[7:8, :], pltpu.roll(z, 2, axis=0)))
    carry_ref[...] = z[tm - 8:tm, :]
    cw = convw_ref[...]
    yc = bg * (cw[0:1, :] * z2 + cw[1:2, :] * z1 + cw[2:3, :] * z)
    yconv_ref[0] = _rms_rows(yc, gconv_ref[...]).astype(yconv_ref.dtype)

    kvc_ref[0] = prow[:, 3 * CONV_WIDTH:3 * CONV_WIDTH + 2 * KV_WIDTH]

    ang = pos_ref[0].astype(F32) * inv_ref[...]
    cos, sin = jnp.cos(ang), jnp.sin(ang)

    def rope_t(base):
        x1 = pt[base:base + HALF, :]
        x2 = pt[base + HALF:base + HEAD_DIM, :]
        return x1 * cos - x2 * sin, x2 * cos + x1 * sin

    scale = HEAD_DIM ** -0.5 * math.log2(math.e)
    for h in range(ATT_HEADS):
        r1, r2 = rope_t(h * HEAD_DIM)
        qt_ref[0, h * HEAD_DIM:h * HEAD_DIM + HALF, :] = (r1 * scale).astype(qt_ref.dtype)
        qt_ref[0, h * HEAD_DIM + HALF:(h + 1) * HEAD_DIM, :] = (r2 * scale).astype(qt_ref.dtype)

    tok = pl.program_id(1) * tm + lax.broadcasted_iota(jnp.int32, (tm, HEAD_DIM), 0)
    blk_col = lax.broadcasted_iota(jnp.int32, (tm, HEAD_DIM), 1)
    onehot = jnp.where(jnp.right_shift(tok, SLC_BLOCK.bit_length() - 1) == blk_col, 1.0, 0.0)
    for br in (WIN, SEL):
        base = ATT_WIDTH + br * KV_WIDTH
        parts = []
        for g in range(KV_GROUPS):
            parts.extend(rope_t(base + g * HEAD_DIM))
        k_rows = jnp.concatenate(parts, axis=0).T
        extra = onehot if br == SEL else jnp.zeros_like(onehot)
        for g in range(KV_GROUPS):
            k_g = k_rows[:, g * HEAD_DIM:(g + 1) * HEAD_DIM]
            kaug_ref[0, g, br] = jnp.concatenate([k_g, extra], axis=-1).astype(kaug_ref.dtype)

    v0 = ATT_WIDTH + 2 * KV_WIDTH
    ones_rows = jnp.where(lax.broadcasted_iota(jnp.int32, (V_AUG - HEAD_DIM, tm), 0) == 0, 1.0, 0.0)
    for br in (WIN, SEL):
        for g in range(KV_GROUPS):
            lo = v0 + br * KV_WIDTH + g * HEAD_DIM
            vaug_ref[0, g, br] = jnp.concatenate([pt[lo:lo + HEAD_DIM, :], ones_rows], axis=0).astype(vaug_ref.dtype)

    g0 = v0 + 2 * KV_WIDTH
    gt_ref[0] = jax.nn.sigmoid(pt[g0:g0 + KV_GROUPS * GATE_ROWS, :])


def _in_proj(x, pos_row, inv_col, g_pre, w_row, w_t, conv_w, g_conv):
    b, s, _ = x.shape
    tm = min(TM_PROJ, s)
    n_row, n_t = w_row.shape[1], w_t.shape[0]
    out_shape = (
        jax.ShapeDtypeStruct((b, s, CONV_WIDTH), BF16),
        jax.ShapeDtypeStruct((b, s, 2 * KV_WIDTH), F32),
        jax.ShapeDtypeStruct((b, ATT_WIDTH, s), BF16),
        jax.ShapeDtypeStruct((b, KV_GROUPS, 2, s, K_AUG), BF16),
        jax.ShapeDtypeStruct((b, KV_GROUPS, 2, V_AUG, s), BF16),
        jax.ShapeDtypeStruct((b, KV_GROUPS * GATE_ROWS, s), F32),
    )
    return pl.pallas_call(
        _in_proj_kernel,
        grid=(b, s // tm),
        in_specs=[
            pl.BlockSpec((1, tm, D_MODEL), lambda i, j: (i, j, 0)),
            pl.BlockSpec((1, 1, tm), lambda i, j: (i, 0, j)),
            _whole((HALF, 1)),
            _whole((1, D_MODEL)),
            _whole((D_MODEL, n_row)),
            _whole((n_t, D_MODEL)),
            _whole((CONV_K, CONV_WIDTH)),
            _whole((1, CONV_WIDTH)),
        ],
        out_specs=(
            pl.BlockSpec((1, tm, CONV_WIDTH), lambda i, j: (i, j, 0)),
            pl.BlockSpec((1, tm, 2 * KV_WIDTH), lambda i, j: (i, j, 0)),
            pl.BlockSpec((1, ATT_WIDTH, tm), lambda i, j: (i, 0, j)),
            pl.BlockSpec((1, KV_GROUPS, 2, tm, K_AUG), lambda i, j: (i, 0, 0, j, 0)),
            pl.BlockSpec((1, KV_GROUPS, 2, V_AUG, tm), lambda i, j: (i, 0, 0, 0, j)),
            pl.BlockSpec((1, KV_GROUPS * GATE_ROWS, tm), lambda i, j: (i, 0, j)),
        ),
        out_shape=out_shape,
        scratch_shapes=[pltpu.VMEM((8, CONV_WIDTH), F32)],
        compiler_params=_params("arbitrary", "arbitrary"),
        name="in_proj",
    )(x, pos_row, inv_col, g_pre, w_row, w_t, conv_w, g_conv)


def _compress_kernel(rk_ref, rv_ref, pos_ref, inv_ref, pek_ref, w1k_ref, w2k_ref, pev_ref, w1v_ref, w2vt_ref,
                     kc_ref, vct_ref):
    half_in = CMP_STRIDE * HEAD_DIM

    def hidden(r_ref, pe_ref, w1_ref):
        r = r_ref[0, 0].astype(BF16)
        a = _dot(r, w1_ref[0:half_in, :])
        bshift = pltpu.roll(_dot(r, w1_ref[half_in:2 * half_in, :]), r.shape[0] - 1, axis=0)
        pe = jnp.broadcast_to(pe_ref[...], (8, 2 * half_in)).astype(BF16)
        h = a + bshift + _dot(pe, w1_ref[...])[0:1, :]
        return (h * jax.nn.sigmoid(h)).astype(BF16)

    kc = _dot(hidden(rk_ref, pek_ref, w1k_ref), w2k_ref[...])
    ang = pos_ref[0].astype(F32) * inv_ref[...]
    cos, sin = jnp.cos(ang), jnp.sin(ang)
    x1, x2 = kc[:, 0:HALF], kc[:, HALF:HEAD_DIM]
    kc_ref[0, 0] = jnp.concatenate([x1 * cos - x2 * sin, x2 * cos + x1 * sin], axis=-1).astype(kc_ref.dtype)
    vct_ref[0, 0] = _dot_nt(w2vt_ref[...], hidden(rv_ref, pev_ref, w1v_ref)).astype(vct_ref.dtype)


def _compress(rk, rv, pos_c, inv_row, pek, w1k, w2k, pev, w1v, w2vt):
    b, g, n_slab, width = rk.shape
    slab = pl.BlockSpec((1, 1, n_slab, width), lambda i, j: (i, j, 0, 0))
    return pl.pallas_call(
        _compress_kernel,
        grid=(b, g),
        in_specs=[
            slab, slab,
            pl.BlockSpec((1, n_slab, 1), lambda i, j: (i, 0, 0)),
            _whole((1, HALF)),
            _whole(pek.shape), _whole(w1k.shape), _whole(w2k.shape),
            _whole(pev.shape), _whole(w1v.shape), _whole(w2vt.shape),
        ],
        out_specs=(
            pl.BlockSpec((1, 1, n_slab, HEAD_DIM), lambda i, j: (i, j, 0, 0)),
            pl.BlockSpec((1, 1, HEAD_DIM, n_slab), lambda i, j: (i, j, 0, 0)),
        ),
        out_shape=(
            jax.ShapeDtypeStruct((b, g, n_slab, HEAD_DIM), BF16),
            jax.ShapeDtypeStruct((b, g, HEAD_DIM, n_slab), BF16),
        ),
        compiler_params=_params("arbitrary", "arbitrary"),
        name="compress",
    )(rk, rv, pos_c, inv_row, pek, w1k, w2k, pev, w1v, w2vt)


def _nsa_kernel(qt_ref, kc_ref, vct_ref, kaug_ref, vaug_ref, gt_ref, map_ref, tri_ref,
                out_ref, rhs_ref, s_ref, m_ref, acc_ref):
    tq = qt_ref.shape[2]
    n_cmp = kc_ref.shape[2]
    n_slc = map_ref.shape[0]
    i = pl.program_id(2)
    s0 = i * tq
    t = s0 + lax.broadcasted_iota(jnp.int32, (1, tq), 1)
    row_of = lambda r: slice(r * HEAD_DIM, (r + 1) * HEAD_DIM)

    def issue_scores(slot, br, tile):
        k_t = kaug_ref[0, 0, br, pl.ds(pl.multiple_of(tile * TK, TK), TK), :]
        for r in range(Q_PER_KV):
            s_ref[slot, r] = _dot(k_t, rhs_ref[br, r])

    def softmax_step(slot, br, tile, bias_idx):
        v_t = vaug_ref[0, 0, br, :, pl.ds(pl.multiple_of(tile * TK, TK), TK)]
        bias = tri_ref[bias_idx]
        for r in range(Q_PER_KV):
            s = s_ref[slot, r] + bias
            m_old = m_ref[r]
            m_new = jnp.maximum(m_old, jnp.max(s, axis=0, keepdims=True))
            p = jnp.exp2(s - m_new).astype(BF16)
            acc_ref[r] = jnp.exp2(m_old - m_new) * acc_ref[r] + _dot(v_t, p)
            m_ref[r] = m_new

    def reset_state():
        for r in range(Q_PER_KV):
            m_ref[r] = jnp.full((1, tq), NEG, F32)
            acc_ref[r] = jnp.zeros((V_AUG, tq), F32)

    def flush_state(gate_row0):
        for r in range(Q_PER_KV):
            acc = acc_ref[r]
            o = acc[0:HEAD_DIM, :] * (1.0 / jnp.maximum(acc[HEAD_DIM:HEAD_DIM + 1, :], 1e-30))
            out_ref[0, row_of(r), :] = out_ref[0, row_of(r), :] + gt_ref[0, gate_row0 + r:gate_row0 + r + 1, :] * o

    zeros_half = jnp.zeros((K_AUG - HEAD_DIM, tq), BF16)
    for r in range(Q_PER_KV):
        rhs_ref[WIN, r, 0:HEAD_DIM, :] = qt_ref[0, row_of(r), :]
        rhs_ref[WIN, r, HEAD_DIM:K_AUG, :] = zeros_half
        rhs_ref[SEL, r, 0:HEAD_DIM, :] = qt_ref[0, row_of(r), :]
    n_win = WINDOW // TK + 1
    win_tiles = [i - (n_win - 1) + k for k in range(n_win)]
    for k in range(n_win):
        issue_scores(_SLOT_WIN + k, WIN, jnp.maximum(win_tiles[k], 0))
    kc = kc_ref[0, 0]
    for r in range(Q_PER_KV):
        s_ref[_SLOT_CMP, r, 0:n_cmp, :] = _dot(kc, qt_ref[0, row_of(r), :])

    cmp_end = lax.broadcasted_iota(jnp.int32, (n_cmp, tq), 0) * CMP_STRIDE + (CMP_BLOCK - 1)
    cbias = jnp.where(cmp_end <= t, 0.0, NEG)
    has_cmp = t >= CMP_BLOCK - 1
    vct = vct_ref[0, 0]
    psum = jnp.zeros((n_cmp, tq), F32)
    for r in range(Q_PER_KV):
        s = s_ref[_SLOT_CMP, r, 0:n_cmp, :] + cbias
        e = jnp.exp2(s - jnp.max(s, axis=0, keepdims=True))
        inv = jnp.where(has_cmp, 1.0 / jnp.maximum(jnp.sum(e, axis=0, keepdims=True), 1e-30), 0.0)
        psum = psum + e * inv
        out_ref[0, row_of(r), :] = (gt_ref[0, r:r + 1, :] * inv) * _dot(vct, e.astype(BF16))

    imp = jnp.dot(map_ref[...], psum, preferred_element_type=F32, precision=lax.Precision.HIGHEST)
    blk = lax.broadcasted_iota(jnp.int32, (n_slc, tq), 0)
    cur = jnp.right_shift(t, SLC_BLOCK.bit_length() - 1)
    forced = (blk == 0) | (blk == cur) | (blk == cur - 1)
    imp = jnp.where(forced, FORCE, jnp.where(blk * SLC_BLOCK > t, -FORCE, imp))
    n_grp = n_slc // 8
    imp_g = [imp[8 * g:8 * g + 8, :] for g in range(n_grp)]
    rank_g = [jnp.zeros((8, tq), jnp.int32) for _ in range(n_grp)]
    sub = lax.broadcasted_iota(jnp.int32, (8, tq), 0)
    for j in range(n_slc):
        vj = jnp.broadcast_to(imp[j:j + 1, :], (8, tq))
        for g in range(n_grp):
            if 8 * g > j:
                beats = jnp.where(vj >= imp_g[g], 1, 0)
            elif 8 * g + 7 < j:
                beats = jnp.where(vj > imp_g[g], 1, 0)
            else:
                beats = jnp.where(sub > j - 8 * g, jnp.where(vj >= imp_g[g], 1, 0), jnp.where(vj > imp_g[g], 1, 0))
            rank_g[g] = rank_g[g] + beats
    rank = jnp.concatenate(rank_g, axis=0)
    sel = jnp.where(rank < min(N_SELECT, n_slc), 0.0, NEG).astype(BF16)
    if n_slc < K_AUG - HEAD_DIM:
        sel = jnp.concatenate([sel, jnp.zeros((K_AUG - HEAD_DIM - n_slc, tq), BF16)], axis=0)
    for r in range(Q_PER_KV):
        rhs_ref[SEL, r, HEAD_DIM:K_AUG, :] = sel
    issue_scores(0, SEL, 0)

    reset_state()
    for k in range(n_win):
        exists = win_tiles[k] >= 0
        bias_idx = _BIAS_CAUSAL if k == n_win - 1 else jnp.where(exists, _BIAS_WINDOW_START if k == 0 else _BIAS_NONE, _BIAS_MASKED)
        softmax_step(_SLOT_WIN + k, WIN, jnp.maximum(win_tiles[k], 0), bias_idx)
    flush_state(2 * Q_PER_KV)

    reset_state()

    def sel_bias(step):
        return jnp.where(step == i, _BIAS_CAUSAL, jnp.where(step > i, _BIAS_MASKED, _BIAS_NONE))

    def body(pair, carry):
        a = 2 * pair
        issue_scores(1, SEL, jnp.minimum(a + 1, i))
        softmax_step(0, SEL, a, sel_bias(a))
        issue_scores(0, SEL, jnp.minimum(a + 2, i))
        softmax_step(1, SEL, jnp.minimum(a + 1, i), sel_bias(a + 1))
        return carry

    lax.fori_loop(0, (i + 2) // 2, body, 0)
    flush_state(Q_PER_KV)


def _nsa(qt, kcmp, vcmpt, kaug, vaug, gt, slc_map_t, tri):
    b, _, s = qt.shape
    tq = min(TQ, s)
    n_cmp = kcmp.shape[2]
    n_slc = s // SLC_BLOCK
    assert tq == TK and s % tq == 0 and WINDOW % TK == 0 and n_slc <= K_AUG - HEAD_DIM and n_cmp <= TK
    assert n_slc % 8 == 0 and _SLOT_WIN + WINDOW // TK + 1 <= _SLOT_CMP < _N_SLOTS
    per_group = lambda shape: pl.BlockSpec((1, 1) + shape, lambda i, j, k: (i, j) + (0,) * len(shape))
    return pl.pallas_call(
        _nsa_kernel,
        grid=(b, KV_GROUPS, s // tq),
        in_specs=[
            pl.BlockSpec((1, GROUP_WIDTH, tq), lambda i, j, k: (i, j, k)),
            per_group((n_cmp, HEAD_DIM)),
            per_group((HEAD_DIM, n_cmp)),
            per_group((2, s, K_AUG)),
            per_group((2, V_AUG, s)),
            pl.BlockSpec((1, GATE_ROWS, tq), lambda i, j, k: (i, j, k)),
            _whole((n_slc, n_cmp)),
            _whole((4, TK, tq)),
        ],
        out_specs=pl.BlockSpec((1, GROUP_WIDTH, tq), lambda i, j, k: (i, j, k)),
        out_shape=jax.ShapeDtypeStruct((b, ATT_WIDTH, s), F32),
        scratch_shapes=[
            pltpu.VMEM((2, Q_PER_KV, K_AUG, tq), BF16),
            pltpu.VMEM((_N_SLOTS, Q_PER_KV, TK, tq), F32),
            pltpu.VMEM((Q_PER_KV, 1, tq), F32),
            pltpu.VMEM((Q_PER_KV, V_AUG, tq), F32),
        ],
        compiler_params=_params("arbitrary", "arbitrary", "arbitrary"),
        name="nsa_attention",
    )(qt, kcmp, vcmpt, kaug, vaug, gt, slc_map_t, tri)


def _mem_kv_kernel(mem_ref, g_ref, wk_ref, wvt_ref, kx_ref, vxt_ref):
    mn = _rms_rows(mem_ref[0], g_ref[...]).astype(BF16)
    kx_ref[0] = _dot(mn, wk_ref[...]).astype(kx_ref.dtype)
    vxt_ref[0] = _dot_nt(wvt_ref[...], mn).astype(vxt_ref.dtype)


def _mem_kv(mem, g_mem, w_k, w_vt):
    b, m, _ = mem.shape
    return pl.pallas_call(
        _mem_kv_kernel,
        grid=(b,),
        in_specs=[pl.BlockSpec((1, m, D_MODEL), lambda i: (i, 0, 0)), _whole((1, D_MODEL)),
                  _whole((D_MODEL, D_MODEL)), _whole((D_MODEL, D_MODEL))],
        out_specs=(pl.BlockSpec((1, m, D_MODEL), lambda i: (i, 0, 0)),
                   pl.BlockSpec((1, D_MODEL, m), lambda i: (i, 0, 0))),
        out_shape=(jax.ShapeDtypeStruct((b, m, D_MODEL), BF16),
                   jax.ShapeDtypeStruct((b, D_MODEL, m), BF16)),
        compiler_params=_params("arbitrary"),
        name="mem_kv",
    )(mem, g_mem, w_k, w_vt)


def _mix_xattn_kernel(x_ref, yconv_ref, yatt_ref, gatt_ref, woa_ref, wob_ref, gpost_ref, gxpre_ref,
                      wqt_ref, kx_ref, vxt_ref, wo_ref, gxpost_ref, out_ref):
    ya = yatt_ref[0]
    yan = ya * lax.rsqrt(jnp.mean(ya * ya, axis=0, keepdims=True) + EPS) * gatt_ref[...]
    mixed = _dot(yconv_ref[0], woa_ref[...]) + _dot_tn(yan.astype(BF16), wob_ref[...])
    h1 = x_ref[0] + _rms_rows(mixed, gpost_ref[...])

    hn = _rms_rows(h1, gxpre_ref[...]).astype(BF16)
    qxt = (_dot_nt(wqt_ref[...], hn) * (X_HEAD_DIM ** -0.5)).astype(BF16)
    heads = []
    for h in range(X_HEADS):
        lo, hi = h * X_HEAD_DIM, (h + 1) * X_HEAD_DIM
        s = _dot(kx_ref[0, :, lo:hi], qxt[lo:hi, :])
        e = jnp.exp(s - jnp.max(s, axis=0, keepdims=True))
        o = _dot(vxt_ref[0, lo:hi, :], e.astype(BF16))
        heads.append(o * (1.0 / jnp.sum(e, axis=0, keepdims=True)))
    oxt = jnp.concatenate(heads, axis=0).astype(BF16)
    out_ref[0] = h1 + _rms_rows(_dot_tn(oxt, wo_ref[...]), gxpost_ref[...])


def _mix_xattn(x, yconv, yatt_t, g_att, w_out_a, w_out_b, g_post, g_xpre, w_qt, kx, vxt, w_o, g_xpost):
    b, s, _ = x.shape
    m = kx.shape[1]
    tm = min(TM_MIX, s)
    return pl.pallas_call(
        _mix_xattn_kernel,
        grid=(b, s // tm),
        in_specs=[
            pl.BlockSpec((1, tm, D_MODEL), lambda i, j: (i, j, 0)),
            pl.BlockSpec((1, tm, CONV_WIDTH), lambda i, j: (i, j, 0)),
            pl.BlockSpec((1, ATT_WIDTH, tm), lambda i, j: (i, 0, j)),
            _whole((ATT_WIDTH, 1)),
            _whole((CONV_WIDTH, D_MODEL)), _whole((ATT_WIDTH, D_MODEL)),
            _whole((1, D_MODEL)), _whole((1, D_MODEL)),
            _whole((D_MODEL, D_MODEL)),
            pl.BlockSpec((1, m, D_MODEL), lambda i, j: (i, 0, 0)),
            pl.BlockSpec((1, D_MODEL, m), lambda i, j: (i, 0, 0)),
            _whole((D_MODEL, D_MODEL)), _whole((1, D_MODEL)),
        ],
        out_specs=pl.BlockSpec((1, tm, D_MODEL), lambda i, j: (i, j, 0)),
        out_shape=jax.ShapeDtypeStruct((b, s, D_MODEL), F32),
        compiler_params=_params("arbitrary", "arbitrary"),
        name="mix_xattn",
    )(x, yconv, yatt_t, g_att, w_out_a, w_out_b, g_post, g_xpre, w_qt, kx, vxt, w_o, g_xpost)


def _ffn_kernel(h_ref, gpre_ref, wgu_ref, wdown_ref, gpost_ref, out_ref):
    h = h_ref[...]
    gu = _dot(_rms_rows(h, gpre_ref[...]).astype(BF16), wgu_ref[...])
    g, up = gu[:, 0:D_FF], gu[:, D_FF:2 * D_FF]
    act = (g * jax.nn.sigmoid(g) * up).astype(BF16)
    out_ref[...] = h + _rms_rows(_dot(act, wdown_ref[...]), gpost_ref[...])


def _ffn(h, g_pre, w_gu, w_down, g_post):
    n, _ = h.shape
    tm = min(TM_FFN, n)
    return pl.pallas_call(
        _ffn_kernel,
        grid=(n // tm,),
        in_specs=[pl.BlockSpec((tm, D_MODEL), lambda i: (i, 0)), _whole((1, D_MODEL)),
                  _whole((D_MODEL, 2 * D_FF)), _whole((D_FF, D_MODEL)), _whole((1, D_MODEL))],
        out_specs=pl.BlockSpec((tm, D_MODEL), lambda i: (i, 0)),
        out_shape=jax.ShapeDtypeStruct((n, D_MODEL), F32),
        compiler_params=_params("arbitrary"),
        name="ffn",
    )(h, g_pre, w_gu, w_down, g_post)


def _slc_map_t(s):
    n_slab = s // CMP_STRIDE
    ci = np.arange(n_slab)[None, :] * CMP_STRIDE
    sj = np.arange(s // SLC_BLOCK)[:, None] * SLC_BLOCK
    ov = np.clip(np.minimum(ci + CMP_BLOCK, sj + SLC_BLOCK) - np.maximum(ci, sj), 0, None)
    ov[:, n_slab - 1] = 0
    return jnp.asarray(ov / CMP_BLOCK, dtype=F32)


def _tile_biases(tq):
    key = np.arange(TK)[:, None]
    qry = np.arange(tq)[None, :]
    none = np.zeros((TK, tq), np.float32)
    causal = np.where(key <= qry, 0.0, NEG).astype(np.float32)
    window_start = np.where(key > qry, 0.0, NEG).astype(np.float32)
    masked = np.full((TK, tq), NEG, np.float32)
    return jnp.asarray(np.stack([none, causal, window_start, masked]))


def kernel(x, mem, positions, norm_mix_pre, w_in, conv_w, pe_kc, w1_kc, w2_kc, pe_vc, w1_vc, w2_vc,
           norm_conv_out, norm_attn_out, w_out, norm_mix_post, norm_x_pre, norm_mem, w_q_x, w_kv_x,
           w_o_x, norm_x_post, norm_ffn_pre, w_gate_up, w_down, norm_ffn_post):
    b, s, _ = x.shape
    row = lambda v: v.reshape(1, -1).astype(F32)

    cuts = np.cumsum([0, CONV_WIDTH, CONV_WIDTH, CONV_WIDTH, ATT_WIDTH] + [KV_WIDTH] * 6 + [3 * ATT_HEADS])
    col = lambda k: w_in[:, cuts[k]:cuts[k + 1]]
    w_row = jnp.concatenate([col(0), col(1), col(2), col(4), col(5)], axis=1).astype(BF16)
    gate_cols = np.zeros((KV_GROUPS * GATE_ROWS,), np.int32)
    gate_live = np.zeros((KV_GROUPS * GATE_ROWS,), np.float32)
    for g in range(KV_GROUPS):
        for c in range(3):
            for r in range(Q_PER_KV):
                gate_cols[g * GATE_ROWS + c * Q_PER_KV + r] = (g * Q_PER_KV + r) * 3 + c
                gate_live[g * GATE_ROWS + c * Q_PER_KV + r] = 1.0
    w_gate = col(10)[:, gate_cols] * gate_live[None, :]
    w_t = jnp.concatenate([col(3), col(8), col(6), col(9), col(7), w_gate], axis=1).T.astype(BF16)

    inv = ROPE_THETA ** (-jnp.arange(HALF, dtype=F32) / HALF)
    yconv, kvc, qt, kaug, vaug, gt = _in_proj(
        x, positions.reshape(b, 1, s), inv.reshape(HALF, 1), row(norm_mix_pre), w_row, w_t,
        conv_w.astype(F32), row(norm_conv_out))

    n_slab = s // CMP_STRIDE
    slabs = kvc.reshape(b, n_slab, CMP_STRIDE, 2, KV_GROUPS, HEAD_DIM).transpose(3, 0, 4, 1, 2, 5)
    slabs = slabs.reshape(2, b, KV_GROUPS, n_slab, CMP_STRIDE * HEAD_DIM)
    pos_c = positions[:, np.minimum(np.arange(n_slab) * CMP_STRIDE + CMP_BLOCK - 1, s - 1)]
    kcmp, vcmpt = _compress(
        slabs[0], slabs[1], pos_c.reshape(b, n_slab, 1), inv.reshape(1, HALF),
        pe_kc.reshape(1, -1), w1_kc.astype(BF16), w2_kc.astype(BF16),
        pe_vc.reshape(1, -1), w1_vc.astype(BF16), w2_vc.T.astype(BF16))

    yatt_t = _nsa(qt, kcmp, vcmpt, kaug, vaug, gt, _slc_map_t(s), _tile_biases(min(TQ, s)))

    kx, vxt = _mem_kv(mem, row(norm_mem), w_kv_x[:, :D_MODEL].astype(BF16), w_kv_x[:, D_MODEL:].T.astype(BF16))
    h2 = _mix_xattn(
        x, yconv, yatt_t, norm_attn_out.reshape(-1, 1).astype(F32),
        w_out[:CONV_WIDTH].astype(BF16), w_out[CONV_WIDTH:].astype(BF16),
        row(norm_mix_post), row(norm_x_pre), w_q_x.T.astype(BF16), kx, vxt, w_o_x.astype(BF16), row(norm_x_post))

    out = _ffn(h2.reshape(b * s, D_MODEL), row(norm_ffn_pre), w_gate_up.astype(BF16), w_down.astype(BF16),
               row(norm_ffn_post))
    return out.reshape(b, s, D_MODEL)
```

```python
import math

import numpy as np
import jax
import jax.numpy as jnp
from jax import lax
from jax.experimental import pallas as pl
from jax.experimental.pallas import tpu as pltpu

D_MODEL = 1024
CONV_WIDTH = 512
CONV_K = 3
ATT_HEADS = 8
HEAD_DIM = 64
HALF = HEAD_DIM // 2
ATT_WIDTH = ATT_HEADS * HEAD_DIM
KV_GROUPS = 2
Q_PER_KV = ATT_HEADS // KV_GROUPS
KV_WIDTH = KV_GROUPS * HEAD_DIM
GROUP_WIDTH = Q_PER_KV * HEAD_DIM
CMP_BLOCK = 32
CMP_STRIDE = 16
CMP_HIDDEN = 256
SLC_BLOCK = 64
N_SELECT = 16
WINDOW = 512
ROPE_THETA = 10000.0
X_HEADS = 4
X_HEAD_DIM = D_MODEL // X_HEADS
D_FF = 2816
EPS = 1e-6
FORCE = 1e4
NEG = -1e30
GATE_ROWS = 16
K_AUG = 2 * HEAD_DIM
V_AUG = HEAD_DIM + 16
WIN, SEL = 0, 1
_SLOT_WIN, _N_SLOTS = 2, 5
_BIAS_NONE, _BIAS_CAUSAL, _BIAS_WINDOW_START, _BIAS_MASKED = 0, 1, 2, 3

TM_PROJ = 512
TM_MIX = 512
TM_FFN = 256
TQ = 256
TK = 256
VMEM_LIMIT = 56 * 1024 * 1024

F32 = jnp.float32
BF16 = jnp.bfloat16

_NT = (((1,), (1,)), ((), ()))
_TN = (((0,), (0,)), ((), ()))


def _dot(a, b):
    return jnp.dot(a, b, preferred_element_type=F32)


def _dot_nt(a, b):
    return lax.dot_general(a, b, _NT, preferred_element_type=F32)


def _dot_tn(a, b):
    return lax.dot_general(a, b, _TN, preferred_element_type=F32)


def _rms_rows(x, g):
    return x * lax.rsqrt(jnp.mean(x * x, axis=-1, keepdims=True) + EPS) * g


def _params(*sem):
    return pltpu.CompilerParams(dimension_semantics=sem, vmem_limit_bytes=VMEM_LIMIT)


def _whole(shape):
    nd = len(shape)
    return pl.BlockSpec(shape, lambda *_: (0,) * nd)


def _in_proj_kernel(x_ref, pos_ref, inv_ref, g_ref, wrow_ref, wt_ref, convw_ref, gconv_ref,
                    yconv_ref, kvc_ref, qt_ref, kaug_ref, vaug_ref, gt_ref, carry_ref):
    tm = x_ref.shape[1]
    u = _rms_rows(x_ref[0], g_ref[...]).astype(BF16)
    prow = _dot(u, wrow_ref[...])
    pt = _dot_nt(wt_ref[...], u)

    bg = prow[:, 0:CONV_WIDTH]
    z = prow[:, CONV_WIDTH:2 * CONV_WIDTH] * prow[:, 2 * CONV_WIDTH:3 * CONV_WIDTH]

    @pl.when(pl.program_id(1) == 0)
    def _():
        carry_ref[...] = jnp.zeros_like(carry_ref)

    prev = carry_ref[...]
    row = lax.broadcasted_iota(jnp.int32, (tm, CONV_WIDTH), 0)
    z1 = jnp.where(row == 0, prev[7:8, :], pltpu.roll(z, 1, axis=0))
    z2 = jnp.where(row == 0, prev[6:7, :], jnp.where(row == 1, prev[7:8, :], pltpu.roll(z, 2, axis=0)))
    carry_ref[...] = z[tm - 8:tm, :]
    cw = convw_ref[...]
    yc = bg * (cw[0:1, :] * z2 + cw[1:2, :] * z1 + cw[2:3, :] * z)
    yconv_ref[0] = _rms_rows(yc, gconv_ref[...]).astype(yconv_ref.dtype)

    for c in range(2 * KV_GROUPS):
        lo = 3 * CONV_WIDTH + c * HEAD_DIM
        kvc_ref[0, c] = prow[:, lo:lo + HEAD_DIM]

    ang = pos_ref[0].astype(F32) * inv_ref[...]
    cos, sin = jnp.cos(ang), jnp.sin(ang)

    def rope_t(base):
        x1 = pt[base:base + HALF, :]
        x2 = pt[base + HALF:base + HEAD_DIM, :]
        return x1 * cos - x2 * sin, x2 * cos + x1 * sin

    scale = HEAD_DIM ** -0.5 * math.log2(math.e)
    for h in range(ATT_HEADS):
        r1, r2 = rope_t(h * HEAD_DIM)
        qt_ref[0, h * HEAD_DIM:h * HEAD_DIM + HALF, :] = (r1 * scale).astype(qt_ref.dtype)
        qt_ref[0, h * HEAD_DIM + HALF:(h + 1) * HEAD_DIM, :] = (r2 * scale).astype(qt_ref.dtype)

    tok = pl.program_id(1) * tm + lax.broadcasted_iota(jnp.int32, (tm, HEAD_DIM), 0)
    blk_col = lax.broadcasted_iota(jnp.int32, (tm, HEAD_DIM), 1)
    onehot = jnp.where(jnp.right_shift(tok, SLC_BLOCK.bit_length() - 1) == blk_col, 1.0, 0.0)
    for br in (WIN, SEL):
        base = ATT_WIDTH + br * KV_WIDTH
        parts = []
        for g in range(KV_GROUPS):
            parts.extend(rope_t(base + g * HEAD_DIM))
        k_rows = jnp.concatenate(parts, axis=0).T
        extra = onehot if br == SEL else jnp.zeros_like(onehot)
        for g in range(KV_GROUPS):
            k_g = k_rows[:, g * HEAD_DIM:(g + 1) * HEAD_DIM]
            kaug_ref[0, g, br] = jnp.concatenate([k_g, extra], axis=-1).astype(kaug_ref.dtype)

    v0 = ATT_WIDTH + 2 * KV_WIDTH
    ones_rows = jnp.where(lax.broadcasted_iota(jnp.int32, (V_AUG - HEAD_DIM, tm), 0) == 0, 1.0, 0.0)
    for br in (WIN, SEL):
        for g in range(KV_GROUPS):
            lo = v0 + br * KV_WIDTH + g * HEAD_DIM
            vaug_ref[0, g, br] = jnp.concatenate([pt[lo:lo + HEAD_DIM, :], ones_rows], axis=0).astype(vaug_ref.dtype)

    g0 = v0 + 2 * KV_WIDTH
    gt_ref[0] = jax.nn.sigmoid(pt[g0:g0 + KV_GROUPS * GATE_ROWS, :])


def _in_proj(x, pos_row, inv_col, g_pre, w_row, w_t, conv_w, g_conv):
    b, s, _ = x.shape
    tm = min(TM_PROJ, s)
    n_row, n_t = w_row.shape[1], w_t.shape[0]
    out_shape = (
        jax.ShapeDtypeStruct((b, s, CONV_WIDTH), BF16),
        jax.ShapeDtypeStruct((b, 2 * KV_GROUPS, s, HEAD_DIM), F32),
        jax.ShapeDtypeStruct((b, ATT_WIDTH, s), BF16),
        jax.ShapeDtypeStruct((b, KV_GROUPS, 2, s, K_AUG), BF16),
        jax.ShapeDtypeStruct((b, KV_GROUPS, 2, V_AUG, s), BF16),
        jax.ShapeDtypeStruct((b, KV_GROUPS * GATE_ROWS, s), F32),
    )
    return pl.pallas_call(
        _in_proj_kernel,
        grid=(b, s // tm),
        in_specs=[
            pl.BlockSpec((1, tm, D_MODEL), lambda i, j: (i, j, 0)),
            pl.BlockSpec((1, 1, tm), lambda i, j: (i, 0, j)),
            _whole((HALF, 1)),
            _whole((1, D_MODEL)),
            _whole((D_MODEL, n_row)),
            _whole((n_t, D_MODEL)),
            _whole((CONV_K, CONV_WIDTH)),
            _whole((1, CONV_WIDTH)),
        ],
        out_specs=(
            pl.BlockSpec((1, tm, CONV_WIDTH), lambda i, j: (i, j, 0)),
            pl.BlockSpec((1, 2 * KV_GROUPS, tm, HEAD_DIM), lambda i, j: (i, 0, j, 0)),
            pl.BlockSpec((1, ATT_WIDTH, tm), lambda i, j: (i, 0, j)),
            pl.BlockSpec((1, KV_GROUPS, 2, tm, K_AUG), lambda i, j: (i, 0, 0, j, 0)),
            pl.BlockSpec((1, KV_GROUPS, 2, V_AUG, tm), lambda i, j: (i, 0, 0, 0, j)),
            pl.BlockSpec((1, KV_GROUPS * GATE_ROWS, tm), lambda i, j: (i, 0, j)),
        ),
        out_shape=out_shape,
        scratch_shapes=[pltpu.VMEM((8, CONV_WIDTH), F32)],
        compiler_params=_params("arbitrary", "arbitrary"),
        name="in_proj",
    )(x, pos_row, inv_col, g_pre, w_row, w_t, conv_w, g_conv)


def _compress_kernel(rk_ref, rv_ref, pos_ref, inv_ref, pek_ref, w1k_ref, w2k_ref, pev_ref, w1v_ref, w2vt_ref,
                     kc_ref, vct_ref):
    half_in = CMP_STRIDE * HEAD_DIM
    n_slab = kc_ref.shape[2]

    def hidden(x_ref, pe_ref, w1_ref):
        a = jnp.zeros((n_slab, CMP_HIDDEN), F32)
        b = jnp.zeros((n_slab, CMP_HIDDEN), F32)
        for j in range(CMP_STRIDE):
            xj = x_ref[0, 0, pl.ds(j, n_slab, stride=CMP_STRIDE), :].astype(BF16)
            a = a + _dot(xj, w1_ref[j * HEAD_DIM:(j + 1) * HEAD_DIM, :])
            b = b + _dot(xj, w1_ref[half_in + j * HEAD_DIM:half_in + (j + 1) * HEAD_DIM, :])
        pe = jnp.broadcast_to(pe_ref[...], (8, 2 * half_in)).astype(BF16)
        h = a + pltpu.roll(b, n_slab - 1, axis=0) + _dot(pe, w1_ref[...])[0:1, :]
        return (h * jax.nn.sigmoid(h)).astype(BF16)

    kc = _dot(hidden(rk_ref, pek_ref, w1k_ref), w2k_ref[...])
    ang = pos_ref[0].astype(F32) * inv_ref[...]
    cos, sin = jnp.cos(ang), jnp.sin(ang)
    x1, x2 = kc[:, 0:HALF], kc[:, HALF:HEAD_DIM]
    kc_ref[0, 0] = jnp.concatenate([x1 * cos - x2 * sin, x2 * cos + x1 * sin], axis=-1).astype(kc_ref.dtype)
    vct_ref[0, 0] = _dot_nt(w2vt_ref[...], hidden(rv_ref, pev_ref, w1v_ref)).astype(vct_ref.dtype)


def _compress(kvc, pos_c, inv_row, pek, w1k, w2k, pev, w1v, w2vt):
    b, _, s, _ = kvc.shape
    g, n_slab = KV_GROUPS, s // CMP_STRIDE
    return pl.pallas_call(
        _compress_kernel,
        grid=(b, g),
        in_specs=[
            pl.BlockSpec((1, 1, s, HEAD_DIM), lambda i, j: (i, j, 0, 0)),
            pl.BlockSpec((1, 1, s, HEAD_DIM), lambda i, j: (i, KV_GROUPS + j, 0, 0)),
            pl.BlockSpec((1, n_slab, 1), lambda i, j: (i, 0, 0)),
            _whole((1, HALF)),
            _whole(pek.shape), _whole(w1k.shape), _whole(w2k.shape),
            _whole(pev.shape), _whole(w1v.shape), _whole(w2vt.shape),
        ],
        out_specs=(
            pl.BlockSpec((1, 1, n_slab, HEAD_DIM), lambda i, j: (i, j, 0, 0)),
            pl.BlockSpec((1, 1, HEAD_DIM, n_slab), lambda i, j: (i, j, 0, 0)),
        ),
        out_shape=(
            jax.ShapeDtypeStruct((b, g, n_slab, HEAD_DIM), BF16),
            jax.ShapeDtypeStruct((b, g, HEAD_DIM, n_slab), BF16),
        ),
        compiler_params=_params("arbitrary", "arbitrary"),
        name="compress",
    )(kvc, kvc, pos_c, inv_row, pek, w1k, w2k, pev, w1v, w2vt)


def _nsa_kernel(qt_ref, kc_ref, vct_ref, kaug_ref, vaug_ref, gt_ref, map_ref, tri_ref,
                out_ref, rhs_ref, s_ref, cs_ref, m_ref, acc_ref):
    tq = qt_ref.shape[2]
    n_cmp = kc_ref.shape[2]
    n_slc = map_ref.shape[0]
    i = pl.program_id(2)
    s0 = i * tq
    t = s0 + lax.broadcasted_iota(jnp.int32, (1, tq), 1)
    row_of = lambda r: slice(r * HEAD_DIM, (r + 1) * HEAD_DIM)

    def issue_scores(slot, br, tile):
        k_t = kaug_ref[0, 0, br, pl.ds(pl.multiple_of(tile * TK, TK), TK), :]
        for r in range(Q_PER_KV):
            s_ref[slot, r] = _dot(k_t, rhs_ref[br, r]).astype(s_ref.dtype)

    def softmax_step(slot, br, tile, bias_idx=None):
        v_t = vaug_ref[0, 0, br, :, pl.ds(pl.multiple_of(tile * TK, TK), TK)]
        for r in range(Q_PER_KV):
            s = s_ref[slot, r]
            if bias_idx is not None:
                s = s + tri_ref[bias_idx]
            m_old = m_ref[r]
            m_new = jnp.maximum(m_old, jnp.max(s, axis=0, keepdims=True).astype(F32))
            p = jnp.exp2(s - m_new.astype(s.dtype))
            acc_ref[r] = jnp.exp2(m_old - m_new) * acc_ref[r] + _dot(v_t, p)
            m_ref[r] = m_new

    def reset_state():
        for r in range(Q_PER_KV):
            m_ref[r] = jnp.full((1, tq), NEG, F32)
            acc_ref[r] = jnp.zeros((V_AUG, tq), F32)

    def flush_state(gate_row0):
        for r in range(Q_PER_KV):
            acc = acc_ref[r]
            o = acc[0:HEAD_DIM, :] * (1.0 / jnp.maximum(acc[HEAD_DIM:HEAD_DIM + 1, :], 1e-30))
            out_ref[0, row_of(r), :] = out_ref[0, row_of(r), :] + gt_ref[0, gate_row0 + r:gate_row0 + r + 1, :] * o

    zeros_half = jnp.zeros((K_AUG - HEAD_DIM, tq), BF16)
    for r in range(Q_PER_KV):
        rhs_ref[WIN, r, 0:HEAD_DIM, :] = qt_ref[0, row_of(r), :]
        rhs_ref[WIN, r, HEAD_DIM:K_AUG, :] = zeros_half
        rhs_ref[SEL, r, 0:HEAD_DIM, :] = qt_ref[0, row_of(r), :]
    n_win = WINDOW // TK + 1
    win_tiles = [i - (n_win - 1) + k for k in range(n_win)]
    for k in range(n_win):
        issue_scores(_SLOT_WIN + k, WIN, jnp.maximum(win_tiles[k], 0))
    kc = kc_ref[0, 0]
    for r in range(Q_PER_KV):
        cs_ref[r] = _dot(kc, qt_ref[0, row_of(r), :])

    cmp_end = lax.broadcasted_iota(jnp.int32, (n_cmp, tq), 0) * CMP_STRIDE + (CMP_BLOCK - 1)
    cbias = jnp.where(cmp_end <= t, 0.0, NEG)
    has_cmp = t >= CMP_BLOCK - 1
    vct = vct_ref[0, 0]
    psum = jnp.zeros((n_cmp, tq), F32)
    for r in range(Q_PER_KV):
        s = cs_ref[r] + cbias
        e = jnp.exp2(s - jnp.max(s, axis=0, keepdims=True))
        inv = jnp.where(has_cmp, 1.0 / jnp.maximum(jnp.sum(e, axis=0, keepdims=True), 1e-30), 0.0)
        psum = psum + e * inv
        out_ref[0, row_of(r), :] = (gt_ref[0, r:r + 1, :] * inv) * _dot(vct, e.astype(BF16))

    imp = jnp.dot(map_ref[...], psum, preferred_element_type=F32, precision=lax.Precision.HIGHEST)
    blk = lax.broadcasted_iota(jnp.int32, (n_slc, tq), 0)
    cur = jnp.right_shift(t, SLC_BLOCK.bit_length() - 1)
    forced = (blk == 0) | (blk == cur) | (blk == cur - 1)
    imp = jnp.where(forced, FORCE, jnp.where(blk * SLC_BLOCK > t, -FORCE, imp))
    n_grp = n_slc // 8
    imp_g = [imp[8 * g:8 * g + 8, :] for g in range(n_grp)]
    rank_g = [jnp.zeros((8, tq), jnp.int32) for _ in range(n_grp)]
    sub = lax.broadcasted_iota(jnp.int32, (8, tq), 0)
    for j in range(n_slc):
        vj = jnp.broadcast_to(imp[j:j + 1, :], (8, tq))
        for g in range(n_grp):
            if 8 * g > j:
                beats = jnp.where(vj >= imp_g[g], 1, 0)
            elif 8 * g + 7 < j:
                beats = jnp.where(vj > imp_g[g], 1, 0)
            else:
                beats = jnp.where(sub > j - 8 * g, jnp.where(vj >= imp_g[g], 1, 0), jnp.where(vj > imp_g[g], 1, 0))
            rank_g[g] = rank_g[g] + beats
    rank = jnp.concatenate(rank_g, axis=0)
    sel = jnp.where(rank < min(N_SELECT, n_slc), 0.0, NEG).astype(BF16)
    if n_slc < K_AUG - HEAD_DIM:
        sel = jnp.concatenate([sel, jnp.zeros((K_AUG - HEAD_DIM - n_slc, tq), BF16)], axis=0)
    for r in range(Q_PER_KV):
        rhs_ref[SEL, r, HEAD_DIM:K_AUG, :] = sel
    issue_scores(0, SEL, 0)

    reset_state()
    for k in range(n_win):
        exists = win_tiles[k] >= 0
        bias_idx = _BIAS_CAUSAL if k == n_win - 1 else jnp.where(exists, _BIAS_WINDOW_START if k == 0 else _BIAS_NONE, _BIAS_MASKED)
        softmax_step(_SLOT_WIN + k, WIN, jnp.maximum(win_tiles[k], 0), bias_idx)
    flush_state(2 * Q_PER_KV)

    reset_state()

    def body(pair, carry):
        a = 2 * pair
        issue_scores(1, SEL, a + 1)
        softmax_step(0, SEL, a)
        issue_scores(0, SEL, a + 2)
        softmax_step(1, SEL, a + 1)
        return carry

    n_plain = jnp.right_shift(i, 1)
    lax.fori_loop(0, n_plain, body, 0)
    a = 2 * n_plain
    odd = i - a
    issue_scores(1, SEL, i)
    softmax_step(0, SEL, a, jnp.where(odd == 1, _BIAS_NONE, _BIAS_CAUSAL))
    softmax_step(1, SEL, i, jnp.where(odd == 1, _BIAS_CAUSAL, _BIAS_MASKED))
    flush_state(Q_PER_KV)


def _nsa(qt, kcmp, vcmpt, kaug, vaug, gt, slc_map_t, tri):
    b, _, s = qt.shape
    tq = min(TQ, s)
    n_cmp = kcmp.shape[2]
    n_slc = s // SLC_BLOCK
    assert tq == TK and s % tq == 0 and WINDOW % TK == 0 and n_slc <= K_AUG - HEAD_DIM and n_cmp <= TK
    assert n_slc % 8 == 0 and _SLOT_WIN + WINDOW // TK + 1 <= _N_SLOTS
    per_group = lambda shape: pl.BlockSpec((1, 1) + shape, lambda i, j, k: (i, j) + (0,) * len(shape))
    return pl.pallas_call(
        _nsa_kernel,
        grid=(b, KV_GROUPS, s // tq),
        in_specs=[
            pl.BlockSpec((1, GROUP_WIDTH, tq), lambda i, j, k: (i, j, k)),
            per_group((n_cmp, HEAD_DIM)),
            per_group((HEAD_DIM, n_cmp)),
            per_group((2, s, K_AUG)),
            per_group((2, V_AUG, s)),
            pl.BlockSpec((1, GATE_ROWS, tq), lambda i, j, k: (i, j, k)),
            _whole((n_slc, n_cmp)),
            _whole((4, TK, tq)),
        ],
        out_specs=pl.BlockSpec((1, GROUP_WIDTH, tq), lambda i, j, k: (i, j, k)),
        out_shape=jax.ShapeDtypeStruct((b, ATT_WIDTH, s), F32),
        scratch_shapes=[
            pltpu.VMEM((2, Q_PER_KV, K_AUG, tq), BF16),
            pltpu.VMEM((_N_SLOTS, Q_PER_KV, TK, tq), BF16),
            pltpu.VMEM((Q_PER_KV, n_cmp, tq), F32),
            pltpu.VMEM((Q_PER_KV, 1, tq), F32),
            pltpu.VMEM((Q_PER_KV, V_AUG, tq), F32),
        ],
        compiler_params=_params("arbitrary", "arbitrary", "arbitrary"),
        name="nsa_attention",
    )(qt, kcmp, vcmpt, kaug, vaug, gt, slc_map_t, tri)


def _mem_kv_kernel(mem_ref, g_ref, wk_ref, wvt_ref, kx_ref, vxt_ref):
    mn = _rms_rows(mem_ref[0], g_ref[...]).astype(BF16)
    kx_ref[0] = _dot(mn, wk_ref[...]).astype(kx_ref.dtype)
    vxt_ref[0] = _dot_nt(wvt_ref[...], mn).astype(vxt_ref.dtype)


def _mem_kv(mem, g_mem, w_k, w_vt):
    b, m, _ = mem.shape
    return pl.pallas_call(
        _mem_kv_kernel,
        grid=(b,),
        in_specs=[pl.BlockSpec((1, m, D_MODEL), lambda i: (i, 0, 0)), _whole((1, D_MODEL)),
                  _whole((D_MODEL, D_MODEL)), _whole((D_MODEL, D_MODEL))],
        out_specs=(pl.BlockSpec((1, m, D_MODEL), lambda i: (i, 0, 0)),
                   pl.BlockSpec((1, D_MODEL, m), lambda i: (i, 0, 0))),
        out_shape=(jax.ShapeDtypeStruct((b, m, D_MODEL), BF16),
                   jax.ShapeDtypeStruct((b, D_MODEL, m), BF16)),
        compiler_params=_params("arbitrary"),
        name="mem_kv",
    )(mem, g_mem, w_k, w_vt)


def _mix_xattn_kernel(x_ref, yconv_ref, yatt_ref, gatt_ref, woa_ref, wob_ref, gpost_ref, gxpre_ref,
                      wqt_ref, kx_ref, vxt_ref, wo_ref, gxpost_ref, out_ref):
    ya = yatt_ref[0]
    yan = ya * lax.rsqrt(jnp.mean(ya * ya, axis=0, keepdims=True) + EPS) * gatt_ref[...]
    mixed = _dot(yconv_ref[0], woa_ref[...]) + _dot_tn(yan.astype(BF16), wob_ref[...])
    h1 = x_ref[0] + _rms_rows(mixed, gpost_ref[...])

    hn = _rms_rows(h1, gxpre_ref[...]).astype(BF16)
    qxt = (_dot_nt(wqt_ref[...], hn) * (X_HEAD_DIM ** -0.5)).astype(BF16)
    heads = []
    for h in range(X_HEADS):
        lo, hi = h * X_HEAD_DIM, (h + 1) * X_HEAD_DIM
        s = _dot(kx_ref[0, :, lo:hi], qxt[lo:hi, :])
        e = jnp.exp(s - jnp.max(s, axis=0, keepdims=True))
        o = _dot(vxt_ref[0, lo:hi, :], e.astype(BF16))
        heads.append(o * (1.0 / jnp.sum(e, axis=0, keepdims=True)))
    oxt = jnp.concatenate(heads, axis=0).astype(BF16)
    out_ref[0] = h1 + _rms_rows(_dot_tn(oxt, wo_ref[...]), gxpost_ref[...])


def _mix_xattn(x, yconv, yatt_t, g_att, w_out_a, w_out_b, g_post, g_xpre, w_qt, kx, vxt, w_o, g_xpost):
    b, s, _ = x.shape
    m = kx.shape[1]
    tm = min(TM_MIX, s)
    return pl.pallas_call(
        _mix_xattn_kernel,
        grid=(b, s // tm),
        in_specs=[
            pl.BlockSpec((1, tm, D_MODEL), lambda i, j: (i, j, 0)),
            pl.BlockSpec((1, tm, CONV_WIDTH), lambda i, j: (i, j, 0)),
            pl.BlockSpec((1, ATT_WIDTH, tm), lambda i, j: (i, 0, j)),
            _whole((ATT_WIDTH, 1)),
            _whole((CONV_WIDTH, D_MODEL)), _whole((ATT_WIDTH, D_MODEL)),
            _whole((1, D_MODEL)), _whole((1, D_MODEL)),
            _whole((D_MODEL, D_MODEL)),
            pl.BlockSpec((1, m, D_MODEL), lambda i, j: (i, 0, 0)),
            pl.BlockSpec((1, D_MODEL, m), lambda i, j: (i, 0, 0)),
            _whole((D_MODEL, D_MODEL)), _whole((1, D_MODEL)),
        ],
        out_specs=pl.BlockSpec((1, tm, D_MODEL), lambda i, j: (i, j, 0)),
        out_shape=jax.ShapeDtypeStruct((b, s, D_MODEL), F32),
        compiler_params=_params("arbitrary", "arbitrary"),
        name="mix_xattn",
    )(x, yconv, yatt_t, g_att, w_out_a, w_out_b, g_post, g_xpre, w_qt, kx, vxt, w_o, g_xpost)


def _ffn_kernel(h_ref, gpre_ref, wgu_ref, wdown_ref, gpost_ref, out_ref):
    h = h_ref[...]
    gu = _dot(_rms_rows(h, gpre_ref[...]).astype(BF16), wgu_ref[...])
    g, up = gu[:, 0:D_FF], gu[:, D_FF:2 * D_FF]
    act = (g * jax.nn.sigmoid(g) * up).astype(BF16)
    out_ref[...] = h + _rms_rows(_dot(act, wdown_ref[...]), gpost_ref[...])


def _ffn(h, g_pre, w_gu, w_down, g_post):
    n, _ = h.shape
    tm = min(TM_FFN, n)
    return pl.pallas_call(
        _ffn_kernel,
        grid=(n // tm,),
        in_specs=[pl.BlockSpec((tm, D_MODEL), lambda i: (i, 0)), _whole((1, D_MODEL)),
                  _whole((D_MODEL, 2 * D_FF)), _whole((D_FF, D_MODEL)), _whole((1, D_MODEL))],
        out_specs=pl.BlockSpec((tm, D_MODEL), lambda i: (i, 0)),
        out_shape=jax.ShapeDtypeStruct((n, D_MODEL), F32),
        compiler_params=_params("arbitrary"),
        name="ffn",
    )(h, g_pre, w_gu, w_down, g_post)


def _slc_map_t(s):
    n_slab = s // CMP_STRIDE
    ci = np.arange(n_slab)[None, :] * CMP_STRIDE
    sj = np.arange(s // SLC_BLOCK)[:, None] * SLC_BLOCK
    ov = np.clip(np.minimum(ci + CMP_BLOCK, sj + SLC_BLOCK) - np.maximum(ci, sj), 0, None)
    ov[:, n_slab - 1] = 0
    return jnp.asarray(ov / CMP_BLOCK, dtype=F32)


def _tile_biases(tq):
    key = np.arange(TK)[:, None]
    qry = np.arange(tq)[None, :]
    none = np.zeros((TK, tq), np.float32)
    causal = np.where(key <= qry, 0.0, NEG).astype(np.float32)
    window_start = np.where(key > qry, 0.0, NEG).astype(np.float32)
    masked = np.full((TK, tq), NEG, np.float32)
    return jnp.asarray(np.stack([none, causal, window_start, masked]), dtype=BF16)


def kernel(x, mem, positions, norm_mix_pre, w_in, conv_w, pe_kc, w1_kc, w2_kc, pe_vc, w1_vc, w2_vc,
           norm_conv_out, norm_attn_out, w_out, norm_mix_post, norm_x_pre, norm_mem, w_q_x, w_kv_x,
           w_o_x, norm_x_post, norm_ffn_pre, w_gate_up, w_down, norm_ffn_post):
    b, s, _ = x.shape
    row = lambda v: v.reshape(1, -1).astype(F32)

    cuts = np.cumsum([0, CONV_WIDTH, CONV_WIDTH, CONV_WIDTH, ATT_WIDTH] + [KV_WIDTH] * 6 + [3 * ATT_HEADS])
    col = lambda k: w_in[:, cuts[k]:cuts[k + 1]]
    w_row = jnp.concatenate([col(0), col(1), col(2), col(4), col(5)], axis=1).astype(BF16)
    gate_cols = np.zeros((KV_GROUPS * GATE_ROWS,), np.int32)
    gate_live = np.zeros((KV_GROUPS * GATE_ROWS,), np.float32)
    for g in range(KV_GROUPS):
        for c in range(3):
            for r in range(Q_PER_KV):
                gate_cols[g * GATE_ROWS + c * Q_PER_KV + r] = (g * Q_PER_KV + r) * 3 + c
                gate_live[g * GATE_ROWS + c * Q_PER_KV + r] = 1.0
    w_gate = col(10)[:, gate_cols] * gate_live[None, :]
    w_t = jnp.concatenate([col(3), col(8), col(6), col(9), col(7), w_gate], axis=1).T.astype(BF16)

    inv = ROPE_THETA ** (-jnp.arange(HALF, dtype=F32) / HALF)
    yconv, kvc, qt, kaug, vaug, gt = _in_proj(
        x, positions.reshape(b, 1, s), inv.reshape(HALF, 1), row(norm_mix_pre), w_row, w_t,
        conv_w.astype(F32), row(norm_conv_out))

    n_slab = s // CMP_STRIDE
    pos_c = positions[:, np.minimum(np.arange(n_slab) * CMP_STRIDE + CMP_BLOCK - 1, s - 1)]
    kcmp, vcmpt = _compress(
        kvc, pos_c.reshape(b, n_slab, 1), inv.reshape(1, HALF),
        pe_kc.reshape(1, -1), w1_kc.astype(BF16), w2_kc.astype(BF16),
        pe_vc.reshape(1, -1), w1_vc.astype(BF16), w2_vc.T.astype(BF16))

    yatt_t = _nsa(qt, kcmp, vcmpt, kaug, vaug, gt, _slc_map_t(s), _tile_biases(min(TQ, s)))

    kx, vxt = _mem_kv(mem, row(norm_mem), w_kv_x[:, :D_MODEL].astype(BF16), w_kv_x[:, D_MODEL:].T.astype(BF16))
    h2 = _mix_xattn(
        x, yconv, yatt_t, norm_attn_out.reshape(-1, 1).astype(F32),
        w_out[:CONV_WIDTH].astype(BF16), w_out[CONV_WIDTH:].astype(BF16),
        row(norm_mix_post), row(norm_x_pre), w_q_x.T.astype(BF16), kx, vxt, w_o_x.astype(BF16), row(norm_x_post))

    out = _ffn(h2.reshape(b * s, D_MODEL), row(norm_ffn_pre), w_gate_up.astype(BF16), w_down.astype(BF16),
               row(norm_ffn_post))
    return out.reshape(b, s, D_MODEL)
```

```python
import math

import numpy as np
import jax
import jax.numpy as jnp
from jax import lax
from jax.experimental import pallas as pl
from jax.experimental.pallas import tpu as pltpu

D_MODEL = 1024
CONV_WIDTH = 512
CONV_K = 3
ATT_HEADS = 8
HEAD_DIM = 64
HALF = HEAD_DIM // 2
ATT_WIDTH = ATT_HEADS * HEAD_DIM
KV_GROUPS = 2
Q_PER_KV = ATT_HEADS // KV_GROUPS
KV_WIDTH = KV_GROUPS * HEAD_DIM
GROUP_WIDTH = Q_PER_KV * HEAD_DIM
CMP_BLOCK = 32
CMP_STRIDE = 16
CMP_HIDDEN = 256
SLC_BLOCK = 64
N_SELECT = 16
WINDOW = 512
ROPE_THETA = 10000.0
X_HEADS = 4
X_HEAD_DIM = D_MODEL // X_HEADS
D_FF = 2816
EPS = 1e-6
FORCE = 1e4
NEG = -1e30
GATE_ROWS = 16
K_AUG = 2 * HEAD_DIM
V_AUG = HEAD_DIM + 16
WIN, SEL = 0, 1
_SLOT_WIN, _N_SLOTS = 2, 5
_BIAS_NONE, _BIAS_CAUSAL, _BIAS_WINDOW_START, _BIAS_MASKED = 0, 1, 2, 3

TM_PROJ = 512
TM_MIX = 512
TM_FFN = 256
TQ = 256
TK = 256
VMEM_LIMIT = 56 * 1024 * 1024

F32 = jnp.float32
BF16 = jnp.bfloat16

_NT = (((1,), (1,)), ((), ()))
_TN = (((0,), (0,)), ((), ()))


def _dot(a, b):
    return jnp.dot(a, b, preferred_element_type=F32)


def _dot_nt(a, b):
    return lax.dot_general(a, b, _NT, preferred_element_type=F32)


def _dot_tn(a, b):
    return lax.dot_general(a, b, _TN, preferred_element_type=F32)


def _rms_rows(x, g):
    return x * lax.rsqrt(jnp.mean(x * x, axis=-1, keepdims=True) + EPS) * g


def _params(*sem):
    return pltpu.CompilerParams(dimension_semantics=sem, vmem_limit_bytes=VMEM_LIMIT)


def _whole(shape):
    nd = len(shape)
    return pl.BlockSpec(shape, lambda *_: (0,) * nd)


def _in_proj_kernel(x_ref, pos_ref, inv_ref, g_ref, wrow_ref, wt_ref, convw_ref, gconv_ref,
                    yconv_ref, kvc_ref, qt_ref, kaug_ref, vaug_ref, gt_ref, carry_ref):
    tm = x_ref.shape[1]
    u = _rms_rows(x_ref[0], g_ref[...]).astype(BF16)
    prow = _dot(u, wrow_ref[...])
    pt = _dot_nt(wt_ref[...], u)

    bg = prow[:, 0:CONV_WIDTH]
    z = prow[:, CONV_WIDTH:2 * CONV_WIDTH] * prow[:, 2 * CONV_WIDTH:3 * CONV_WIDTH]

    @pl.when(pl.program_id(1) == 0)
    def _():
        carry_ref[...] = jnp.zeros_like(carry_ref)

    prev = carry_ref[...]
    row = lax.broadcasted_iota(jnp.int32, (tm, CONV_WIDTH), 0)
    z1 = jnp.where(row == 0, prev[7:8, :], pltpu.roll(z, 1, axis=0))
    z2 = jnp.where(row == 0, prev[6:7, :], jnp.where(row == 1, prev[7:8, :], pltpu.roll(z, 2, axis=0)))
    carry_ref[...] = z[tm - 8:tm, :]
    cw = convw_ref[...]
    yc = bg * (cw[0:1, :] * z2 + cw[1:2, :] * z1 + cw[2:3, :] * z)
    yconv_ref[0] = _rms_rows(yc, gconv_ref[...]).astype(yconv_ref.dtype)

    for c in range(2 * KV_GROUPS):
        lo = 3 * CONV_WIDTH + c * HEAD_DIM
        kvc_ref[0, c] = prow[:, lo:lo + HEAD_DIM]

    ang = pos_ref[0].astype(F32) * inv_ref[...]
    cos, sin = jnp.cos(ang), jnp.sin(ang)

    def rope_t(base):
        x1 = pt[base:base + HALF, :]
        x2 = pt[base + HALF:base + HEAD_DIM, :]
        return x1 * cos - x2 * sin, x2 * cos + x1 * sin

    scale = HEAD_DIM ** -0.5 * math.log2(math.e)
    for h in range(ATT_HEADS):
        r1, r2 = rope_t(h * HEAD_DIM)
        qt_ref[0, h * HEAD_DIM:h * HEAD_DIM + HALF, :] = (r1 * scale).astype(qt_ref.dtype)
        qt_ref[0, h * HEAD_DIM + HALF:(h + 1) * HEAD_DIM, :] = (r2 * scale).astype(qt_ref.dtype)

    tok = pl.program_id(1) * tm + lax.broadcasted_iota(jnp.int32, (tm, HEAD_DIM), 0)
    blk_col = lax.broadcasted_iota(jnp.int32, (tm, HEAD_DIM), 1)
    onehot = jnp.where(jnp.right_shift(tok, SLC_BLOCK.bit_length() - 1) == blk_col, 1.0, 0.0)
    for br in (WIN, SEL):
        base = ATT_WIDTH + br * KV_WIDTH
        parts = []
        for g in range(KV_GROUPS):
            parts.extend(rope_t(base + g * HEAD_DIM))
        k_rows = jnp.concatenate(parts, axis=0).T
        extra = onehot if br == SEL else jnp.zeros_like(onehot)
        for g in range(KV_GROUPS):
            k_g = k_rows[:, g * HEAD_DIM:(g + 1) * HEAD_DIM]
            kaug_ref[0, g, br] = jnp.concatenate([k_g, extra], axis=-1).astype(kaug_ref.dtype)

    v0 = ATT_WIDTH + 2 * KV_WIDTH
    ones_rows = jnp.where(lax.broadcasted_iota(jnp.int32, (V_AUG - HEAD_DIM, tm), 0) == 0, 1.0, 0.0)
    for br in (WIN, SEL):
        for g in range(KV_GROUPS):
            lo = v0 + br * KV_WIDTH + g * HEAD_DIM
            vaug_ref[0, g, br] = jnp.concatenate([pt[lo:lo + HEAD_DIM, :], ones_rows], axis=0).astype(vaug_ref.dtype)

    g0 = v0 + 2 * KV_WIDTH
    gt_ref[0] = jax.nn.sigmoid(pt[g0:g0 + KV_GROUPS * GATE_ROWS, :])


def _in_proj(x, pos_row, inv_col, g_pre, w_row, w_t, conv_w, g_conv):
    b, s, _ = x.shape
    tm = min(TM_PROJ, s)
    n_row, n_t = w_row.shape[1], w_t.shape[0]
    out_shape = (
        jax.ShapeDtypeStruct((b, s, CONV_WIDTH), BF16),
        jax.ShapeDtypeStruct((b, 2 * KV_GROUPS, s, HEAD_DIM), F32),
        jax.ShapeDtypeStruct((b, ATT_WIDTH, s), BF16),
        jax.ShapeDtypeStruct((b, KV_GROUPS, 2, s, K_AUG), BF16),
        jax.ShapeDtypeStruct((b, KV_GROUPS, 2, V_AUG, s), BF16),
        jax.ShapeDtypeStruct((b, KV_GROUPS * GATE_ROWS, s), F32),
    )
    return pl.pallas_call(
        _in_proj_kernel,
        grid=(b, s // tm),
        in_specs=[
            pl.BlockSpec((1, tm, D_MODEL), lambda i, j: (i, j, 0)),
            pl.BlockSpec((1, 1, tm), lambda i, j: (i, 0, j)),
            _whole((HALF, 1)),
            _whole((1, D_MODEL)),
            _whole((D_MODEL, n_row)),
            _whole((n_t, D_MODEL)),
            _whole((CONV_K, CONV_WIDTH)),
            _whole((1, CONV_WIDTH)),
        ],
        out_specs=(
            pl.BlockSpec((1, tm, CONV_WIDTH), lambda i, j: (i, j, 0)),
            pl.BlockSpec((1, 2 * KV_GROUPS, tm, HEAD_DIM), lambda i, j: (i, 0, j, 0)),
            pl.BlockSpec((1, ATT_WIDTH, tm), lambda i, j: (i, 0, j)),
            pl.BlockSpec((1, KV_GROUPS, 2, tm, K_AUG), lambda i, j: (i, 0, 0, j, 0)),
            pl.BlockSpec((1, KV_GROUPS, 2, V_AUG, tm), lambda i, j: (i, 0, 0, 0, j)),
            pl.BlockSpec((1, KV_GROUPS * GATE_ROWS, tm), lambda i, j: (i, 0, j)),
        ),
        out_shape=out_shape,
        scratch_shapes=[pltpu.VMEM((8, CONV_WIDTH), F32)],
        compiler_params=_params("arbitrary", "arbitrary"),
        name="in_proj",
    )(x, pos_row, inv_col, g_pre, w_row, w_t, conv_w, g_conv)


def _compress_kernel(rk_ref, rv_ref, pos_ref, inv_ref, pek_ref, w1k_ref, w2k_ref, pev_ref, w1v_ref, w2vt_ref,
                     kc_ref, vct_ref):
    half_in = CMP_STRIDE * HEAD_DIM
    n_slab = kc_ref.shape[2]

    def hidden(x_ref, pe_ref, w1_ref):
        a = jnp.zeros((n_slab, CMP_HIDDEN), F32)
        b = jnp.zeros((n_slab, CMP_HIDDEN), F32)
        for j in range(CMP_STRIDE):
            xj = x_ref[0, 0, pl.ds(j, n_slab, stride=CMP_STRIDE), :].astype(BF16)
            a = a + _dot(xj, w1_ref[j * HEAD_DIM:(j + 1) * HEAD_DIM, :])
            b = b + _dot(xj, w1_ref[half_in + j * HEAD_DIM:half_in + (j + 1) * HEAD_DIM, :])
        pe = jnp.broadcast_to(pe_ref[...], (8, 2 * half_in)).astype(BF16)
        h = a + pltpu.roll(b, n_slab - 1, axis=0) + _dot(pe, w1_ref[...])[0:1, :]
        return (h * jax.nn.sigmoid(h)).astype(BF16)

    kc = _dot(hidden(rk_ref, pek_ref, w1k_ref), w2k_ref[...])
    ang = pos_ref[0].astype(F32) * inv_ref[...]
    cos, sin = jnp.cos(ang), jnp.sin(ang)
    x1, x2 = kc[:, 0:HALF], kc[:, HALF:HEAD_DIM]
    kc_ref[0, 0] = jnp.concatenate([x1 * cos - x2 * sin, x2 * cos + x1 * sin], axis=-1).astype(kc_ref.dtype)
    vct_ref[0, 0] = _dot_nt(w2vt_ref[...], hidden(rv_ref, pev_ref, w1v_ref)).astype(vct_ref.dtype)


def _compress(kvc, pos_c, inv_row, pek, w1k, w2k, pev, w1v, w2vt):
    b, _, s, _ = kvc.shape
    g, n_slab = KV_GROUPS, s // CMP_STRIDE
    return pl.pallas_call(
        _compress_kernel,
        grid=(b, g),
        in_specs=[
            pl.BlockSpec((1, 1, s, HEAD_DIM), lambda i, j: (i, j, 0, 0)),
            pl.BlockSpec((1, 1, s, HEAD_DIM), lambda i, j: (i, KV_GROUPS + j, 0, 0)),
            pl.BlockSpec((1, n_slab, 1), lambda i, j: (i, 0, 0)),
            _whole((1, HALF)),
            _whole(pek.shape), _whole(w1k.shape), _whole(w2k.shape),
            _whole(pev.shape), _whole(w1v.shape), _whole(w2vt.shape),
        ],
        out_specs=(
            pl.BlockSpec((1, 1, n_slab, HEAD_DIM), lambda i, j: (i, j, 0, 0)),
            pl.BlockSpec((1, 1, HEAD_DIM, n_slab), lambda i, j: (i, j, 0, 0)),
        ),
        out_shape=(
            jax.ShapeDtypeStruct((b, g, n_slab, HEAD_DIM), BF16),
            jax.ShapeDtypeStruct((b, g, HEAD_DIM, n_slab), BF16),
        ),
        compiler_params=_params("arbitrary", "arbitrary"),
        name="compress",
    )(kvc, kvc, pos_c, inv_row, pek, w1k, w2k, pev, w1v, w2vt)


def _nsa_kernel(qt_ref, kc_ref, vct_ref, kaug_ref, vaug_ref, gt_ref, map_ref, tri_ref,
                out_ref, rhs_ref, s_ref, cs_ref, m_ref, acc_ref):
    tq = qt_ref.shape[2]
    n_cmp = kc_ref.shape[2]
    n_slc = map_ref.shape[0]
    i = pl.program_id(2)
    s0 = i * tq
    t = s0 + lax.broadcasted_iota(jnp.int32, (1, tq), 1)
    row_of = lambda r: slice(r * HEAD_DIM, (r + 1) * HEAD_DIM)

    def issue_scores(slot, br, tile):
        k_t = kaug_ref[0, 0, br, pl.ds(pl.multiple_of(tile * TK, TK), TK), :]
        for r in range(Q_PER_KV):
            s_ref[slot, r] = _dot(k_t, rhs_ref[br, r]).astype(s_ref.dtype)

    def softmax_step(slot, br, tile, bias_idx=None):
        v_t = vaug_ref[0, 0, br, :, pl.ds(pl.multiple_of(tile * TK, TK), TK)]
        for r in range(Q_PER_KV):
            s = s_ref[slot, r]
            if bias_idx is not None:
                s = s + tri_ref[bias_idx]
            m_old = m_ref[r]
            m_new = jnp.maximum(m_old, jnp.max(s, axis=0, keepdims=True).astype(F32))
            p = jnp.exp2(s - m_new.astype(s.dtype))
            acc_ref[r] = jnp.exp2(m_old - m_new) * acc_ref[r] + _dot(v_t, p)
            m_ref[r] = m_new

    def reset_state():
        for r in range(Q_PER_KV):
            m_ref[r] = jnp.full((1, tq), NEG, F32)
            acc_ref[r] = jnp.zeros((V_AUG, tq), F32)

    def flush_state(gate_row0):
        for r in range(Q_PER_KV):
            acc = acc_ref[r]
            o = acc[0:HEAD_DIM, :] * (1.0 / jnp.maximum(acc[HEAD_DIM:HEAD_DIM + 1, :], 1e-30))
            out_ref[0, row_of(r), :] = out_ref[0, row_of(r), :] + gt_ref[0, gate_row0 + r:gate_row0 + r + 1, :] * o

    zeros_half = jnp.zeros((K_AUG - HEAD_DIM, tq), BF16)
    for r in range(Q_PER_KV):
        rhs_ref[WIN, r, 0:HEAD_DIM, :] = qt_ref[0, row_of(r), :]
        rhs_ref[WIN, r, HEAD_DIM:K_AUG, :] = zeros_half
        rhs_ref[SEL, r, 0:HEAD_DIM, :] = qt_ref[0, row_of(r), :]
    n_win = WINDOW // TK + 1
    win_tiles = [i - (n_win - 1) + k for k in range(n_win)]
    kc = kc_ref[0, 0]
    for r in range(Q_PER_KV):
        cs_ref[r] = _dot(kc, qt_ref[0, row_of(r), :])
    for k in range(n_win):
        issue_scores(_SLOT_WIN + k, WIN, jnp.maximum(win_tiles[k], 0))

    def window_step(k):
        exists = win_tiles[k] >= 0
        bias_idx = _BIAS_CAUSAL if k == n_win - 1 else jnp.where(exists, _BIAS_WINDOW_START if k == 0 else _BIAS_NONE, _BIAS_MASKED)
        softmax_step(_SLOT_WIN + k, WIN, jnp.maximum(win_tiles[k], 0), bias_idx)

    cmp_end = lax.broadcasted_iota(jnp.int32, (n_cmp, tq), 0) * CMP_STRIDE + (CMP_BLOCK - 1)
    cbias = jnp.where(cmp_end <= t, 0.0, NEG)
    has_cmp = t >= CMP_BLOCK - 1
    vct = vct_ref[0, 0]
    psum = jnp.zeros((n_cmp, tq), F32)
    for r in range(Q_PER_KV):
        s = cs_ref[r] + cbias
        e = jnp.exp2(s - jnp.max(s, axis=0, keepdims=True))
        inv = jnp.where(has_cmp, 1.0 / jnp.maximum(jnp.sum(e, axis=0, keepdims=True), 1e-30), 0.0)
        psum = psum + e * inv
        out_ref[0, row_of(r), :] = (gt_ref[0, r:r + 1, :] * inv) * _dot(vct, e.astype(BF16))

    imp = jnp.dot(map_ref[...], psum, preferred_element_type=F32, precision=lax.Precision.HIGHEST)
    reset_state()
    for k in range(n_win - 1):
        window_step(k)
    blk =lax.broadcasted_iota(jnp.int32, (n_slc, tq), 0)
    cur = jnp.right_shift(t, SLC_BLOCK.bit_length() - 1)
    forced = (blk == 0) | (blk == cur) | (blk == cur - 1)
    imp = jnp.where(forced, FORCE, jnp.where(blk * SLC_BLOCK > t, -FORCE, imp))
    n_grp = n_slc // 8
    imp_g = [imp[8 * g:8 * g + 8, :] for g in range(n_grp)]
    rank_g = [jnp.zeros((8, tq), jnp.int32) for _ in range(n_grp)]
    sub = lax.broadcasted_iota(jnp.int32, (8, tq), 0)
    for j in range(n_slc):
        vj = jnp.broadcast_to(imp[j:j + 1, :], (8, tq))
        for g in range(n_grp):
            if 8 * g > j:
                beats = jnp.where(vj >= imp_g[g], 1, 0)
            elif 8 * g + 7 < j:
                beats = jnp.where(vj > imp_g[g], 1, 0)
            else:
                beats = jnp.where(sub > j - 8 * g, jnp.where(vj >= imp_g[g], 1, 0), jnp.where(vj > imp_g[g], 1, 0))
            rank_g[g] = rank_g[g] + beats
    rank = jnp.concatenate(rank_g, axis=0)
    sel = jnp.where(rank < min(N_SELECT, n_slc), 0.0, NEG).astype(BF16)
    if n_slc < K_AUG - HEAD_DIM:
        sel = jnp.concatenate([sel, jnp.zeros((K_AUG - HEAD_DIM - n_slc, tq), BF16)], axis=0)
    for r in range(Q_PER_KV):
        rhs_ref[SEL, r, HEAD_DIM:K_AUG, :] = sel
    issue_scores(0, SEL, 0)

    window_step(n_win - 1)
    flush_state(2 * Q_PER_KV)

    reset_state()

    def body(pair, carry):
        a = 2 * pair
        issue_scores(1, SEL, a + 1)
        softmax_step(0, SEL, a)
        issue_scores(0, SEL, a + 2)
        softmax_step(1, SEL, a + 1)
        return carry

    n_plain = jnp.right_shift(i, 1)
    lax.fori_loop(0, n_plain, body, 0)
    @pl.when(i == 2 * n_plain)
    def _():
        softmax_step(0, SEL, i, _BIAS_CAUSAL)

    @pl.when(i != 2 * n_plain)
    def _():
        issue_scores(1, SEL, i)
        softmax_step(0, SEL, i - 1)
        softmax_step(1, SEL, i, _BIAS_CAUSAL)

    flush_state(Q_PER_KV)


def _nsa(qt, kcmp, vcmpt, kaug, vaug, gt, slc_map_t, tri):
    b, _, s = qt.shape
    tq = min(TQ, s)
    n_cmp = kcmp.shape[2]
    n_slc = s // SLC_BLOCK
    assert tq == TK and s % tq == 0 and WINDOW % TK == 0 and n_slc <= K_AUG - HEAD_DIM and n_cmp <= TK
    assert n_slc % 8 == 0 and _SLOT_WIN + WINDOW // TK + 1 <= _N_SLOTS
    per_group = lambda shape: pl.BlockSpec((1, 1) + shape, lambda i, j, k: (i, j) + (0,) * len(shape))
    return pl.pallas_call(
        _nsa_kernel,
        grid=(b, KV_GROUPS, s // tq),
        in_specs=[
            pl.BlockSpec((1, GROUP_WIDTH, tq), lambda i, j, k: (i, j, k)),
            per_group((n_cmp, HEAD_DIM)),
            per_group((HEAD_DIM, n_cmp)),
            per_group((2, s, K_AUG)),
            per_group((2, V_AUG, s)),
            pl.BlockSpec((1, GATE_ROWS, tq), lambda i, j, k: (i, j, k)),
            _whole((n_slc, n_cmp)),
            _whole((4, TK, tq)),
        ],
        out_specs=pl.BlockSpec((1, GROUP_WIDTH, tq), lambda i, j, k: (i, j, k)),
        out_shape=jax.ShapeDtypeStruct((b, ATT_WIDTH, s), F32),
        scratch_shapes=[
            pltpu.VMEM((2, Q_PER_KV, K_AUG, tq), BF16),
            pltpu.VMEM((_N_SLOTS, Q_PER_KV, TK, tq), BF16),
            pltpu.VMEM((Q_PER_KV, n_cmp, tq), F32),
            pltpu.VMEM((Q_PER_KV, 1, tq), F32),
            pltpu.VMEM((Q_PER_KV, V_AUG, tq), F32),
        ],
        compiler_params=_params("arbitrary", "arbitrary", "arbitrary"),
        name="nsa_attention",
    )(qt, kcmp, vcmpt, kaug, vaug, gt, slc_map_t, tri)


def _mem_kv_kernel(mem_ref, g_ref, wk_ref, wvt_ref, kx_ref, vxt_ref):
    mn = _rms_rows(mem_ref[0], g_ref[...]).astype(BF16)
    kx_ref[0] = _dot(mn, wk_ref[...]).astype(kx_ref.dtype)
    vxt_ref[0] = _dot_nt(wvt_ref[...], mn).astype(vxt_ref.dtype)


def _mem_kv(mem, g_mem, w_k, w_vt):
    b, m, _ = mem.shape
    return pl.pallas_call(
        _mem_kv_kernel,
        grid=(b,),
        in_specs=[pl.BlockSpec((1, m, D_MODEL), lambda i: (i, 0, 0)), _whole((1, D_MODEL)),
                  _whole((D_MODEL, D_MODEL)), _whole((D_MODEL, D_MODEL))],
        out_specs=(pl.BlockSpec((1, m, D_MODEL), lambda i: (i, 0, 0)),
                   pl.BlockSpec((1, D_MODEL, m), lambda i: (i, 0, 0))),
        out_shape=(jax.ShapeDtypeStruct((b, m, D_MODEL), BF16),
                   jax.ShapeDtypeStruct((b, D_MODEL, m), BF16)),
        compiler_params=_params("arbitrary"),
        name="mem_kv",
    )(mem, g_mem, w_k, w_vt)


def _mix_xattn_kernel(x_ref, yconv_ref, yatt_ref, gatt_ref, woa_ref, wob_ref, gpost_ref, gxpre_ref,
                      wqt_ref, kx_ref, vxt_ref, wo_ref, gxpost_ref, out_ref):
    ya = yatt_ref[0]
    yan = ya * lax.rsqrt(jnp.mean(ya * ya, axis=0, keepdims=True) + EPS) * gatt_ref[...]
    mixed = _dot(yconv_ref[0], woa_ref[...]) + _dot_tn(yan.astype(BF16), wob_ref[...])
    h1 = x_ref[0] + _rms_rows(mixed, gpost_ref[...])

    hn = _rms_rows(h1, gxpre_ref[...]).astype(BF16)
    qxt = (_dot_nt(wqt_ref[...], hn) * (X_HEAD_DIM ** -0.5)).astype(BF16)
    heads = []
    for h in range(X_HEADS):
        lo, hi = h * X_HEAD_DIM, (h + 1) * X_HEAD_DIM
        s = _dot(kx_ref[0, :, lo:hi], qxt[lo:hi, :])
        e = jnp.exp(s - jnp.max(s, axis=0, keepdims=True))
        o = _dot(vxt_ref[0, lo:hi, :], e.astype(BF16))
        heads.append(o * (1.0 / jnp.sum(e, axis=0, keepdims=True)))
    oxt = jnp.concatenate(heads, axis=0).astype(BF16)
    out_ref[0] = h1 + _rms_rows(_dot_tn(oxt, wo_ref[...]), gxpost_ref[...])


def _mix_xattn(x, yconv, yatt_t, g_att, w_out_a, w_out_b, g_post, g_xpre, w_qt, kx, vxt, w_o, g_xpost):
    b, s, _ = x.shape
    m = kx.shape[1]
    tm = min(TM_MIX, s)
    return pl.pallas_call(
        _mix_xattn_kernel,
        grid=(b, s // tm),
        in_specs=[
            pl.BlockSpec((1, tm, D_MODEL), lambda i, j: (i, j, 0)),
            pl.BlockSpec((1, tm, CONV_WIDTH), lambda i, j: (i, j, 0)),
            pl.BlockSpec((1, ATT_WIDTH, tm), lambda i, j: (i, 0, j)),
            _whole((ATT_WIDTH, 1)),
            _whole((CONV_WIDTH, D_MODEL)), _whole((ATT_WIDTH, D_MODEL)),
            _whole((1, D_MODEL)), _whole((1, D_MODEL)),
            _whole((D_MODEL, D_MODEL)),
            pl.BlockSpec((1, m, D_MODEL), lambda i, j: (i, 0, 0)),
            pl.BlockSpec((1, D_MODEL, m), lambda i, j: (i, 0, 0)),
            _whole((D_MODEL, D_MODEL)), _whole((1, D_MODEL)),
        ],
        out_specs=pl.BlockSpec((1, tm, D_MODEL), lambda i, j: (i, j, 0)),
        out_shape=jax.ShapeDtypeStruct((b, s, D_MODEL), F32),
        compiler_params=_params("arbitrary", "arbitrary"),
        name="mix_xattn",
    )(x, yconv, yatt_t, g_att, w_out_a, w_out_b, g_post, g_xpre, w_qt, kx, vxt, w_o, g_xpost)


def _ffn_kernel(h_ref, gpre_ref, wgu_ref, wdown_ref, gpost_ref, out_ref):
    h = h_ref[...]
    gu = _dot(_rms_rows(h, gpre_ref[...]).astype(BF16), wgu_ref[...])
    g, up = gu[:, 0:D_FF], gu[:, D_FF:2 * D_FF]
    act = (g * jax.nn.sigmoid(g) * up).astype(BF16)
    out_ref[...] = h + _rms_rows(_dot(act, wdown_ref[...]), gpost_ref[...])


def _ffn(h, g_pre, w_gu, w_down, g_post):
    n, _ = h.shape
    tm = min(TM_FFN, n)
    return pl.pallas_call(
        _ffn_kernel,
        grid=(n // tm,),
        in_specs=[pl.BlockSpec((tm, D_MODEL), lambda i: (i, 0)), _whole((1, D_MODEL)),
                  _whole((D_MODEL, 2 * D_FF)), _whole((D_FF, D_MODEL)), _whole((1, D_MODEL))],
        out_specs=pl.BlockSpec((tm, D_MODEL), lambda i: (i, 0)),
        out_shape=jax.ShapeDtypeStruct((n, D_MODEL), F32),
        compiler_params=_params("arbitrary"),
        name="ffn",
    )(h, g_pre, w_gu, w_down, g_post)


def _slc_map_t(s):
    n_slab = s // CMP_STRIDE
    ci = np.arange(n_slab)[None, :] * CMP_STRIDE
    sj = np.arange(s // SLC_BLOCK)[:, None] * SLC_BLOCK
    ov = np.clip(np.minimum(ci + CMP_BLOCK, sj + SLC_BLOCK) - np.maximum(ci, sj), 0, None)
    ov[:, n_slab - 1] = 0
    return jnp.asarray(ov / CMP_BLOCK, dtype=F32)


def _tile_biases(tq):
    key = np.arange(TK)[:, None]
    qry = np.arange(tq)[None, :]
    none = np.zeros((TK, tq), np.float32)
    causal = np.where(key <= qry, 0.0, NEG).astype(np.float32)
    window_start = np.where(key > qry, 0.0, NEG).astype(np.float32)
    masked = np.full((TK, tq), NEG, np.float32)
    return jnp.asarray(np.stack([none, causal, window_start, masked]), dtype=BF16)


def kernel(x, mem, positions, norm_mix_pre, w_in, conv_w, pe_kc, w1_kc, w2_kc, pe_vc, w1_vc, w2_vc,
           norm_conv_out, norm_attn_out, w_out, norm_mix_post, norm_x_pre, norm_mem, w_q_x, w_kv_x,
           w_o_x, norm_x_post, norm_ffn_pre, w_gate_up, w_down, norm_ffn_post):
    b, s, _ = x.shape
    row = lambda v: v.reshape(1, -1).astype(F32)

    cuts = np.cumsum([0, CONV_WIDTH, CONV_WIDTH, CONV_WIDTH, ATT_WIDTH] + [KV_WIDTH] * 6 + [3 * ATT_HEADS])
    col = lambda k: w_in[:, cuts[k]:cuts[k + 1]]
    w_row = jnp.concatenate([col(0), col(1), col(2), col(4), col(5)], axis=1).astype(BF16)
    gate_cols = np.zeros((KV_GROUPS * GATE_ROWS,), np.int32)
    gate_live = np.zeros((KV_GROUPS * GATE_ROWS,), np.float32)
    for g in range(KV_GROUPS):
        for c in range(3):
            for r in range(Q_PER_KV):
                gate_cols[g * GATE_ROWS + c * Q_PER_KV + r] = (g * Q_PER_KV + r) * 3 + c
                gate_live[g * GATE_ROWS + c * Q_PER_KV + r] = 1.0
    w_gate = col(10)[:, gate_cols] * gate_live[None, :]
    w_t = jnp.concatenate([col(3), col(8), col(6), col(9), col(7), w_gate], axis=1).T.astype(BF16)

    inv = ROPE_THETA ** (-jnp.arange(HALF, dtype=F32) / HALF)
    yconv, kvc, qt, kaug, vaug, gt = _in_proj(
        x, positions.reshape(b, 1, s), inv.reshape(HALF, 1), row(norm_mix_pre), w_row, w_t,
        conv_w.astype(F32), row(norm_conv_out))

    n_slab = s // CMP_STRIDE
    pos_c = positions[:, np.minimum(np.arange(n_slab) * CMP_STRIDE + CMP_BLOCK - 1, s - 1)]
    kcmp, vcmpt = _compress(
        kvc, pos_c.reshape(b, n_slab, 1), inv.reshape(1, HALF),
        pe_kc.reshape(1, -1), w1_kc.astype(BF16), w2_kc.astype(BF16),
        pe_vc.reshape(1, -1), w1_vc.astype(BF16), w2_vc.T.astype(BF16))

    yatt_t = _nsa(qt, kcmp, vcmpt, kaug, vaug, gt, _slc_map_t(s), _tile_biases(min(TQ, s)))

    kx, vxt = _mem_kv(mem, row(norm_mem), w_kv_x[:, :D_MODEL].astype(BF16), w_kv_x[:, D_MODEL:].T.astype(BF16))
    h2 = _mix_xattn(
        x, yconv, yatt_t, norm_attn_out.reshape(-1, 1).astype(F32),
        w_out[:CONV_WIDTH].astype(BF16), w_out[CONV_WIDTH:].astype(BF16),
        row(norm_mix_post), row(norm_x_pre), w_q_x.T.astype(BF16), kx, vxt, w_o_x.astype(BF16), row(norm_x_post))

    out = _ffn(h2.reshape(b * s, D_MODEL), row(norm_ffn_pre), w_gate_up.astype(BF16), w_down.astype(BF16),
               row(norm_ffn_post))
    return out.reshape(b, s, D_MODEL)
```

```python
import math

import numpy as np
import jax
import jax.numpy as jnp
from jax import lax
from jax.experimental import pallas as pl
from jax.experimental.pallas import tpu as pltpu

D_MODEL = 1024
CONV_WIDTH = 512
CONV_K = 3
ATT_HEADS = 8
HEAD_DIM = 64
HALF = HEAD_DIM // 2
ATT_WIDTH = ATT_HEADS * HEAD_DIM
KV_GROUPS = 2
Q_PER_KV = ATT_HEADS // KV_GROUPS
KV_WIDTH = KV_GROUPS * HEAD_DIM
GROUP_WIDTH = Q_PER_KV * HEAD_DIM
CMP_BLOCK = 32
CMP_STRIDE = 16
CMP_HIDDEN = 256
SLC_BLOCK = 64
N_SELECT = 16
WINDOW = 512
ROPE_THETA = 10000.0
X_HEADS = 4
X_HEAD_DIM = D_MODEL // X_HEADS
D_FF = 2816
EPS = 1e-6
FORCE = 1e4
NEG = -1e30
GATE_ROWS = 16
K_AUG = 2 * HEAD_DIM
V_AUG = HEAD_DIM + 16
WIN, SEL = 0, 1
_SLOT_WIN, _N_SLOTS = 2, 5
_BIAS_NONE, _BIAS_CAUSAL, _BIAS_WINDOW_START, _BIAS_MASKED = 0, 1, 2, 3

TM_PROJ = 1024
TM_MIX = 1024
TM_FFN = 512
TM_FFN_SUB = 256
SUB_ROWS = 512
TQ = 256
TK = 256
VMEM_LIMIT = 56 * 1024 * 1024

F32 = jnp.float32
BF16 = jnp.bfloat16

_NT = (((1,), (1,)), ((), ()))
_TN = (((0,), (0,)), ((), ()))


def _dot(a, b):
    return jnp.dot(a, b, preferred_element_type=F32)


def _dot_nt(a, b):
    return lax.dot_general(a, b, _NT, preferred_element_type=F32)


def _dot_tn(a, b):
    return lax.dot_general(a, b, _TN, preferred_element_type=F32)


def _rms_rows(x, g):
    return x * lax.rsqrt(jnp.mean(x * x, axis=-1, keepdims=True) + EPS) * g


def _params(*sem):
    return pltpu.CompilerParams(dimension_semantics=sem, vmem_limit_bytes=VMEM_LIMIT)


def _whole(shape):
    nd = len(shape)
    return pl.BlockSpec(shape, lambda *_: (0,) * nd, pipeline_mode=pl.Buffered(1))


def _in_proj_kernel(x_ref, pos_ref, inv_ref, g_ref, wrow_ref, wt_ref, convw_ref, gconv_ref,
                    yconv_ref, kvc_ref, qt_ref, kaug_ref, vaug_ref, gt_ref, carry_ref):
    @pl.when(pl.program_id(1) == 0)
    def _():
        carry_ref[...] = jnp.zeros_like(carry_ref)

    tm = x_ref.shape[1]
    subs = [slice(c0, c0 + SUB_ROWS) for c0 in range(0, tm, SUB_ROWS)]
    u = [_rms_rows(x_ref[0, rows, :], g_ref[...]).astype(BF16) for rows in subs]
    prow = [_dot(v, wrow_ref[...]) for v in u]
    pt = [_dot_nt(wt_ref[...], v) for v in u]
    for rows, pr, ptr in zip(subs, prow, pt):
        _in_proj_epilogue(rows, pl.program_id(1) * tm + rows.start, pr, ptr, pos_ref, inv_ref, convw_ref, gconv_ref,
                          yconv_ref, kvc_ref, qt_ref, kaug_ref, vaug_ref, gt_ref, carry_ref)


def _in_proj_epilogue(rows, tok0, prow, pt, pos_ref, inv_ref, convw_ref, gconv_ref,
                      yconv_ref, kvc_ref, qt_ref, kaug_ref, vaug_ref, gt_ref, carry_ref):
    n = rows.stop - rows.start

    bg = prow[:, 0:CONV_WIDTH]
    z = prow[:, CONV_WIDTH:2 * CONV_WIDTH] * prow[:, 2 * CONV_WIDTH:3 * CONV_WIDTH]
    prev = carry_ref[...]
    row = lax.broadcasted_iota(jnp.int32, (n, CONV_WIDTH), 0)
    z1 = jnp.where(row == 0, prev[7:8, :], pltpu.roll(z, 1, axis=0))
    z2 = jnp.where(row == 0, prev[6:7, :], jnp.where(row == 1, prev[7:8, :], pltpu.roll(z, 2, axis=0)))
    carry_ref[...] = z[n - 8:n, :]
    cw = convw_ref[...]
    yc = bg * (cw[0:1, :] * z2 + cw[1:2, :] * z1 + cw[2:3, :] * z)
    yconv_ref[0, rows, :] = _rms_rows(yc, gconv_ref[...]).astype(yconv_ref.dtype)

    for c in range(2 * KV_GROUPS):
        lo = 3 * CONV_WIDTH + c * HEAD_DIM
        kvc_ref[0, c, rows, :] = prow[:, lo:lo + HEAD_DIM]

    ang = pos_ref[0, :, rows].astype(F32) * inv_ref[...]
    cos, sin = jnp.cos(ang), jnp.sin(ang)

    def rope_t(base):
        x1 = pt[base:base + HALF, :]
        x2 = pt[base + HALF:base + HEAD_DIM, :]
        return x1 * cos - x2 * sin, x2 * cos + x1 * sin

    scale = HEAD_DIM ** -0.5 * math.log2(math.e)
    for h in range(ATT_HEADS):
        r1, r2 = rope_t(h * HEAD_DIM)
        qt_ref[0, h * HEAD_DIM:h * HEAD_DIM + HALF, rows] = (r1 * scale).astype(qt_ref.dtype)
        qt_ref[0, h * HEAD_DIM + HALF:(h + 1) * HEAD_DIM, rows] = (r2 * scale).astype(qt_ref.dtype)

    tok = tok0 + lax.broadcasted_iota(jnp.int32, (n, HEAD_DIM), 0)
    blk_col = lax.broadcasted_iota(jnp.int32, (n, HEAD_DIM), 1)
    onehot = jnp.where(jnp.right_shift(tok, SLC_BLOCK.bit_length() - 1) == blk_col, 1.0, 0.0)
    for br in (WIN, SEL):
        base = ATT_WIDTH + br * KV_WIDTH
        parts = []
        for g in range(KV_GROUPS):
            parts.extend(rope_t(base + g * HEAD_DIM))
        k_rows = jnp.concatenate(parts, axis=0).T
        extra = onehot if br == SEL else jnp.zeros_like(onehot)
        for g in range(KV_GROUPS):
            k_g = k_rows[:, g * HEAD_DIM:(g + 1) * HEAD_DIM]
            kaug_ref[0, g, br, rows, :] = jnp.concatenate([k_g, extra], axis=-1).astype(kaug_ref.dtype)

    v0 = ATT_WIDTH + 2 * KV_WIDTH
    ones_rows = jnp.where(lax.broadcasted_iota(jnp.int32, (V_AUG - HEAD_DIM, n), 0) == 0, 1.0, 0.0)
    for br in (WIN, SEL):
        for g in range(KV_GROUPS):
            lo = v0 + br * KV_WIDTH + g * HEAD_DIM
            vaug_ref[0, g, br, :, rows] = jnp.concatenate([pt[lo:lo + HEAD_DIM, :], ones_rows], axis=0).astype(vaug_ref.dtype)

    g0 = v0 + 2 * KV_WIDTH
    gt_ref[0, :, rows] = jax.nn.sigmoid(pt[g0:g0 + KV_GROUPS * GATE_ROWS, :])


def _in_proj(x, pos_row, inv_col, g_pre, w_row, w_t, conv_w, g_conv):
    b, s, _ = x.shape
    tm = min(TM_PROJ, s)
    n_row, n_t = w_row.shape[1], w_t.shape[0]
    out_shape = (
        jax.ShapeDtypeStruct((b, s, CONV_WIDTH), BF16),
        jax.ShapeDtypeStruct((b, 2 * KV_GROUPS, s, HEAD_DIM), F32),
        jax.ShapeDtypeStruct((b, ATT_WIDTH, s), BF16),
        jax.ShapeDtypeStruct((b, KV_GROUPS, 2, s, K_AUG), BF16),
        jax.ShapeDtypeStruct((b, KV_GROUPS, 2, V_AUG, s), BF16),
        jax.ShapeDtypeStruct((b, KV_GROUPS * GATE_ROWS, s), F32),
    )
    return pl.pallas_call(
        _in_proj_kernel,
        grid=(b, s // tm),
        in_specs=[
            pl.BlockSpec((1, tm, D_MODEL), lambda i, j: (i, j, 0)),
            pl.BlockSpec((1, 1, tm), lambda i, j: (i, 0, j)),
            _whole((HALF, 1)),
            _whole((1, D_MODEL)),
            _whole((D_MODEL, n_row)),
            _whole((n_t, D_MODEL)),
            _whole((CONV_K, CONV_WIDTH)),
            _whole((1, CONV_WIDTH)),
        ],
        out_specs=(
            pl.BlockSpec((1, tm, CONV_WIDTH), lambda i, j: (i, j, 0)),
            pl.BlockSpec((1, 2 * KV_GROUPS, tm, HEAD_DIM), lambda i, j: (i, 0, j, 0)),
            pl.BlockSpec((1, ATT_WIDTH, tm), lambda i, j: (i, 0, j)),
            pl.BlockSpec((1, KV_GROUPS, 2, tm, K_AUG), lambda i, j: (i, 0, 0, j, 0)),
            pl.BlockSpec((1, KV_GROUPS, 2, V_AUG, tm), lambda i, j: (i, 0, 0, 0, j)),
            pl.BlockSpec((1, KV_GROUPS * GATE_ROWS, tm), lambda i, j: (i, 0, j)),
        ),
        out_shape=out_shape,
        scratch_shapes=[pltpu.VMEM((8, CONV_WIDTH), F32)],
        compiler_params=_params("arbitrary", "arbitrary"),
        name="in_proj",
    )(x, pos_row, inv_col, g_pre, w_row, w_t, conv_w, g_conv)


def _compress_kernel(rk_ref, rv_ref, pos_ref, inv_ref, pek_ref, w1k_ref, w2k_ref, pev_ref, w1v_ref, w2vt_ref,
                     kc_ref, vct_ref):
    half_in = CMP_STRIDE * HEAD_DIM
    n_slab = kc_ref.shape[2]

    def hidden(x_ref, pe_ref, w1_ref):
        a = jnp.zeros((n_slab, CMP_HIDDEN), F32)
        b = jnp.zeros((n_slab, CMP_HIDDEN), F32)
        for j in range(CMP_STRIDE):
            xj = x_ref[0, 0, pl.ds(j, n_slab, stride=CMP_STRIDE), :].astype(BF16)
            a = a + _dot(xj, w1_ref[j * HEAD_DIM:(j + 1) * HEAD_DIM, :])
            b = b + _dot(xj, w1_ref[half_in + j * HEAD_DIM:half_in + (j + 1) * HEAD_DIM, :])
        pe = jnp.broadcast_to(pe_ref[...], (8, 2 * half_in)).astype(BF16)
        h = a + pltpu.roll(b, n_slab - 1, axis=0) + _dot(pe, w1_ref[...])[0:1, :]
        return (h * jax.nn.sigmoid(h)).astype(BF16)

    kc = _dot(hidden(rk_ref, pek_ref, w1k_ref), w2k_ref[...])
    ang = pos_ref[0].astype(F32) * inv_ref[...]
    cos, sin = jnp.cos(ang), jnp.sin(ang)
    x1, x2 = kc[:, 0:HALF], kc[:, HALF:HEAD_DIM]
    kc_ref[0, 0] = jnp.concatenate([x1 * cos - x2 * sin, x2 * cos + x1 * sin], axis=-1).astype(kc_ref.dtype)
    vct_ref[0, 0] = _dot_nt(w2vt_ref[...], hidden(rv_ref, pev_ref, w1v_ref)).astype(vct_ref.dtype)


def _compress(kvc, pos_c, inv_row, pek, w1k, w2k, pev, w1v, w2vt):
    b, _, s, _ = kvc.shape
    g, n_slab = KV_GROUPS, s // CMP_STRIDE
    return pl.pallas_call(
        _compress_kernel,
        grid=(b, g),
        in_specs=[
            pl.BlockSpec((1, 1, s, HEAD_DIM), lambda i, j: (i, j, 0, 0)),
            pl.BlockSpec((1, 1, s, HEAD_DIM), lambda i, j: (i, KV_GROUPS + j, 0, 0)),
            pl.BlockSpec((1, n_slab, 1), lambda i, j: (i, 0, 0)),
            _whole((1, HALF)),
            _whole(pek.shape), _whole(w1k.shape), _whole(w2k.shape),
            _whole(pev.shape), _whole(w1v.shape), _whole(w2vt.shape),
        ],
        out_specs=(
            pl.BlockSpec((1, 1, n_slab, HEAD_DIM), lambda i, j: (i, j, 0, 0)),
            pl.BlockSpec((1, 1, HEAD_DIM, n_slab), lambda i, j: (i, j, 0, 0)),
        ),
        out_shape=(
            jax.ShapeDtypeStruct((b, g, n_slab, HEAD_DIM), BF16),
            jax.ShapeDtypeStruct((b, g, HEAD_DIM, n_slab), BF16),
        ),
        compiler_params=_params("arbitrary", "arbitrary"),
        name="compress",
    )(kvc, kvc, pos_c, inv_row, pek, w1k, w2k, pev, w1v, w2vt)


def _nsa_kernel(qt_ref, kc_ref, vct_ref, kaug_ref, vaug_ref, gt_ref, map_ref, tri_ref,
                out_ref, rhs_ref, s_ref, cs_ref, m_ref, acc_ref):
    tq = qt_ref.shape[2]
    n_cmp = kc_ref.shape[2]
    n_slc = map_ref.shape[0]
    i = pl.program_id(2)
    s0 = i * tq
    t = s0 + lax.broadcasted_iota(jnp.int32, (1, tq), 1)
    row_of = lambda r: slice(r * HEAD_DIM, (r + 1) * HEAD_DIM)

    def issue_scores(slot, br, tile):
        k_t = kaug_ref[0, 0, br, pl.ds(pl.multiple_of(tile * TK, TK), TK), :]
        for r in range(Q_PER_KV):
            s_ref[slot, r] = _dot(k_t, rhs_ref[br, r]).astype(s_ref.dtype)

    def softmax_step(slot, br, tile, bias_idx=None):
        v_t = vaug_ref[0, 0, br, :, pl.ds(pl.multiple_of(tile * TK, TK), TK)]
        for r in range(Q_PER_KV):
            s = s_ref[slot, r]
            if bias_idx is not None:
                s = s + tri_ref[bias_idx]
            m_old = m_ref[r]
            m_new = jnp.maximum(m_old, jnp.max(s, axis=0, keepdims=True).astype(F32))
            p = jnp.exp2(s - m_new.astype(s.dtype))
            acc_ref[r] = jnp.exp2(m_old - m_new) * acc_ref[r] + _dot(v_t, p)
            m_ref[r] = m_new

    def reset_state():
        for r in range(Q_PER_KV):
            m_ref[r] = jnp.full((1, tq), NEG, F32)
            acc_ref[r] = jnp.zeros((V_AUG, tq), F32)

    def flush_state(gate_row0):
        for r in range(Q_PER_KV):
            acc = acc_ref[r]
            o = acc[0:HEAD_DIM, :] * (1.0 / jnp.maximum(acc[HEAD_DIM:HEAD_DIM + 1, :], 1e-30))
            out_ref[0, row_of(r), :] = out_ref[0, row_of(r), :] + gt_ref[0, gate_row0 + r:gate_row0 + r + 1, :] * o

    zeros_half = jnp.zeros((K_AUG - HEAD_DIM, tq), BF16)
    for r in range(Q_PER_KV):
        rhs_ref[WIN, r, 0:HEAD_DIM, :] = qt_ref[0, row_of(r), :]
        rhs_ref[WIN, r, HEAD_DIM:K_AUG, :] = zeros_half
        rhs_ref[SEL, r, 0:HEAD_DIM, :] = qt_ref[0, row_of(r), :]
    n_win = WINDOW // TK + 1
    win_tiles = [i - (n_win - 1) + k for k in range(n_win)]
    kc = kc_ref[0, 0]
    for r in range(Q_PER_KV):
        cs_ref[r] = _dot(kc, qt_ref[0, row_of(r), :])
    for k in range(n_win):
        issue_scores(_SLOT_WIN + k, WIN, jnp.maximum(win_tiles[k], 0))

    def window_step(k):
        exists = win_tiles[k] >= 0
        bias_idx = _BIAS_CAUSAL if k == n_win - 1 else jnp.where(exists, _BIAS_WINDOW_START if k == 0 else _BIAS_NONE, _BIAS_MASKED)
        softmax_step(_SLOT_WIN + k, WIN, jnp.maximum(win_tiles[k], 0), bias_idx)

    cmp_end = lax.broadcasted_iota(jnp.int32, (n_cmp, tq), 0) * CMP_STRIDE + (CMP_BLOCK - 1)
    cbias = jnp.where(cmp_end <= t, 0.0, NEG)
    has_cmp = t >= CMP_BLOCK - 1
    vct = vct_ref[0, 0]
    psum = jnp.zeros((n_cmp, tq), F32)
    for r in range(Q_PER_KV):
        s = cs_ref[r] + cbias
        e = jnp.exp2(s - jnp.max(s, axis=0, keepdims=True))
        inv = jnp.where(has_cmp, 1.0 / jnp.maximum(jnp.sum(e, axis=0, keepdims=True), 1e-30), 0.0)
        psum = psum + e * inv
        out_ref[0, row_of(r), :] = (gt_ref[0, r:r + 1, :] * inv) * _dot(vct, e.astype(BF16))

    imp = jnp.dot(map_ref[...], psum, preferred_element_type=F32, precision=lax.Precision.HIGHEST)
    reset_state()
    for k in range(n_win - 1):
        window_step(k)
    blk =lax.broadcasted_iota(jnp.int32, (n_slc, tq), 0)
    cur = jnp.right_shift(t, SLC_BLOCK.bit_length() - 1)
    forced = (blk == 0) | (blk == cur) | (blk == cur - 1)
    imp = jnp.where(forced, FORCE, jnp.where(blk * SLC_BLOCK > t, -FORCE, imp))
    n_grp = n_slc // 8
    imp_g = [imp[8 * g:8 * g + 8, :] for g in range(n_grp)]
    rank_g = [jnp.zeros((8, tq), jnp.int32) for _ in range(n_grp)]
    sub = lax.broadcasted_iota(jnp.int32, (8, tq), 0)
    for j in range(n_slc):
        vj = jnp.broadcast_to(imp[j:j + 1, :], (8, tq))
        for g in range(n_grp):
            if 8 * g > j:
                beats = jnp.where(vj >= imp_g[g], 1, 0)
            elif 8 * g + 7 < j:
                beats = jnp.where(vj > imp_g[g], 1, 0)
            else:
                beats = jnp.where(sub > j - 8 * g, jnp.where(vj >= imp_g[g], 1, 0), jnp.where(vj > imp_g[g], 1, 0))
            rank_g[g] = rank_g[g] + beats
    rank = jnp.concatenate(rank_g, axis=0)
    sel = jnp.where(rank < min(N_SELECT, n_slc), 0.0, NEG).astype(BF16)
    if n_slc < K_AUG - HEAD_DIM:
        sel = jnp.concatenate([sel, jnp.zeros((K_AUG - HEAD_DIM - n_slc, tq), BF16)], axis=0)
    for r in range(Q_PER_KV):
        rhs_ref[SEL, r, HEAD_DIM:K_AUG, :] = sel
    issue_scores(0, SEL, 0)

    window_step(n_win - 1)
    flush_state(2 * Q_PER_KV)

    reset_state()

    def body(pair, carry):
        a = 2 * pair
        issue_scores(1, SEL, a + 1)
        softmax_step(0, SEL, a)
        issue_scores(0, SEL, a + 2)
        softmax_step(1, SEL, a + 1)
        return carry

    n_plain = jnp.right_shift(i, 1)
    lax.fori_loop(0, n_plain, body, 0)
    @pl.when(i == 2 * n_plain)
    def _():
        softmax_step(0, SEL, i, _BIAS_CAUSAL)

    @pl.when(i != 2 * n_plain)
    def _():
        issue_scores(1, SEL, i)
        softmax_step(0, SEL, i - 1)
        softmax_step(1, SEL, i, _BIAS_CAUSAL)

    flush_state(Q_PER_KV)


def _nsa(qt, kcmp, vcmpt, kaug, vaug, gt, slc_map_t, tri):
    b, _, s = qt.shape
    tq = min(TQ, s)
    n_cmp = kcmp.shape[2]
    n_slc = s // SLC_BLOCK
    assert tq == TK and s % tq == 0 and WINDOW % TK == 0 and n_slc <= K_AUG - HEAD_DIM and n_cmp <= TK
    assert n_slc % 8 == 0 and _SLOT_WIN + WINDOW // TK + 1 <= _N_SLOTS
    per_group = lambda shape: pl.BlockSpec((1, 1) + shape, lambda i, j, k: (i, j) + (0,) * len(shape))
    return pl.pallas_call(
        _nsa_kernel,
        grid=(b, KV_GROUPS, s // tq),
        in_specs=[
            pl.BlockSpec((1, GROUP_WIDTH, tq), lambda i, j, k: (i, j, k)),
            per_group((n_cmp, HEAD_DIM)),
            per_group((HEAD_DIM, n_cmp)),
            per_group((2, s, K_AUG)),
            per_group((2, V_AUG, s)),
            pl.BlockSpec((1, GATE_ROWS, tq), lambda i, j, k: (i, j, k)),
            _whole((n_slc, n_cmp)),
            _whole((4, TK, tq)),
        ],
        out_specs=pl.BlockSpec((1, GROUP_WIDTH, tq), lambda i, j, k: (i, j, k)),
        out_shape=jax.ShapeDtypeStruct((b, ATT_WIDTH, s), F32),
        scratch_shapes=[
            pltpu.VMEM((2, Q_PER_KV, K_AUG, tq), BF16),
            pltpu.VMEM((_N_SLOTS, Q_PER_KV, TK, tq), BF16),
            pltpu.VMEM((Q_PER_KV, n_cmp, tq), F32),
            pltpu.VMEM((Q_PER_KV, 1, tq), F32),
            pltpu.VMEM((Q_PER_KV, V_AUG, tq), F32),
        ],
        compiler_params=_params("arbitrary", "arbitrary", "arbitrary"),
        name="nsa_attention",
    )(qt, kcmp, vcmpt, kaug, vaug, gt, slc_map_t, tri)


def _mem_kv_kernel(mem_ref, g_ref, wk_ref, wvt_ref, kx_ref, vxt_ref):
    mn = _rms_rows(mem_ref[0], g_ref[...]).astype(BF16)
    kx_ref[0] = _dot(mn, wk_ref[...]).astype(kx_ref.dtype)
    vxt_ref[0] = _dot_nt(wvt_ref[...], mn).astype(vxt_ref.dtype)


def _mem_kv(mem, g_mem, w_k, w_vt):
    b, m, _ = mem.shape
    return pl.pallas_call(
        _mem_kv_kernel,
        grid=(b,),
        in_specs=[pl.BlockSpec((1, m, D_MODEL), lambda i: (i, 0, 0)), _whole((1, D_MODEL)),
                  _whole((D_MODEL, D_MODEL)), _whole((D_MODEL, D_MODEL))],
        out_specs=(pl.BlockSpec((1, m, D_MODEL), lambda i: (i, 0, 0)),
                   pl.BlockSpec((1, D_MODEL, m), lambda i: (i, 0, 0))),
        out_shape=(jax.ShapeDtypeStruct((b, m, D_MODEL), BF16),
                   jax.ShapeDtypeStruct((b, D_MODEL, m), BF16)),
        compiler_params=_params("arbitrary"),
        name="mem_kv",
    )(mem, g_mem, w_k, w_vt)


def _mix_xattn_kernel(x_ref, yconv_ref, yatt_ref, gatt_ref, woa_ref, wob_ref, gpost_ref, gxpre_ref,
                      wqt_ref, kx_ref, vxt_ref, wo_ref, gxpost_ref, out_ref):
    subs = [slice(c0, c0 + SUB_ROWS) for c0 in range(0, x_ref.shape[1], SUB_ROWS)]
    head = lambda h: slice(h * X_HEAD_DIM, (h + 1) * X_HEAD_DIM)

    def yatt_normed(rows):
        ya = yatt_ref[0, :, rows]
        return (ya * lax.rsqrt(jnp.mean(ya * ya, axis=0, keepdims=True) + EPS) * gatt_ref[...]).astype(BF16)

    yan = [yatt_normed(rows) for rows in subs]
    mixed = [_dot(yconv_ref[0, rows, :], woa_ref[...]) + _dot_tn(y, wob_ref[...]) for rows, y in zip(subs, yan)]
    h1 = [x_ref[0, rows, :] + _rms_rows(m, gpost_ref[...]) for rows, m in zip(subs, mixed)]
    hn = [_rms_rows(h, gxpre_ref[...]).astype(BF16) for h in h1]
    qxt = [(_dot_nt(wqt_ref[...], h) * (X_HEAD_DIM ** -0.5 * math.log2(math.e))).astype(BF16) for h in hn]
    scores = [[_dot(kx_ref[0, :, head(h)], q[head(h), :]) for h in range(X_HEADS)] for q in qxt]

    def attend(sc):
        heads = []
        for h in range(X_HEADS):
            e = jnp.exp2(sc[h] - jnp.max(sc[h], axis=0, keepdims=True))
            o = _dot(vxt_ref[0, head(h), :], e.astype(BF16))
            heads.append(o * (1.0 / jnp.sum(e, axis=0, keepdims=True)))
        return jnp.concatenate(heads, axis=0).astype(BF16)

    oxt = [attend(sc) for sc in scores]
    proj = [_dot_tn(o, wo_ref[...]) for o in oxt]
    for rows, h, p in zip(subs, h1, proj):
        out_ref[0, rows, :] = h + _rms_rows(p, gxpost_ref[...])


def _mix_xattn(x, yconv, yatt_t, g_att, w_out_a, w_out_b, g_post, g_xpre, w_qt, kx, vxt, w_o, g_xpost):
    b, s, _ = x.shape
    m = kx.shape[1]
    tm = min(TM_MIX, s)
    return pl.pallas_call(
        _mix_xattn_kernel,
        grid=(b, s // tm),
        in_specs=[
            pl.BlockSpec((1, tm, D_MODEL), lambda i, j: (i, j, 0)),
            pl.BlockSpec((1, tm, CONV_WIDTH), lambda i, j: (i, j, 0)),
            pl.BlockSpec((1, ATT_WIDTH, tm), lambda i, j: (i, 0, j)),
            _whole((ATT_WIDTH, 1)),
            _whole((CONV_WIDTH, D_MODEL)), _whole((ATT_WIDTH, D_MODEL)),
            _whole((1, D_MODEL)), _whole((1, D_MODEL)),
            _whole((D_MODEL, D_MODEL)),
            pl.BlockSpec((1, m, D_MODEL), lambda i, j: (i, 0, 0)),
            pl.BlockSpec((1, D_MODEL, m), lambda i, j: (i, 0, 0)),
            _whole((D_MODEL, D_MODEL)), _whole((1, D_MODEL)),
        ],
        out_specs=pl.BlockSpec((1, tm, D_MODEL), lambda i, j: (i, j, 0)),
        out_shape=jax.ShapeDtypeStruct((b, s, D_MODEL), F32),
        compiler_params=_params("arbitrary", "arbitrary"),
        name="mix_xattn",
    )(x, yconv, yatt_t, g_att, w_out_a, w_out_b, g_post, g_xpre, w_qt, kx, vxt, w_o, g_xpost)


def _ffn_kernel(h_ref, gpre_ref, wgu_ref, wdown_ref, gpost_ref, out_ref):
    subs = [slice(c0, c0 + TM_FFN_SUB) for c0 in range(0, h_ref.shape[0], TM_FFN_SUB)]
    hn = [_rms_rows(h_ref[rows, :], gpre_ref[...]).astype(BF16) for rows in subs]
    gu = [_dot(v, wgu_ref[...]) for v in hn]
    act = [(v[:, 0:D_FF] * jax.nn.sigmoid(v[:, 0:D_FF]) * v[:, D_FF:2 * D_FF]).astype(BF16) for v in gu]
    down = [_dot(a, wdown_ref[...]) for a in act]
    for rows, d in zip(subs, down):
        out_ref[rows, :] = h_ref[rows, :] + _rms_rows(d, gpost_ref[...])


def _ffn(h, g_pre, w_gu, w_down, g_post):
    n, _ = h.shape
    tm = min(TM_FFN, n)
    return pl.pallas_call(
        _ffn_kernel,
        grid=(n // tm,),
        in_specs=[pl.BlockSpec((tm, D_MODEL), lambda i: (i, 0)), _whole((1, D_MODEL)),
                  _whole((D_MODEL, 2 * D_FF)), _whole((D_FF, D_MODEL)), _whole((1, D_MODEL))],
        out_specs=pl.BlockSpec((tm, D_MODEL), lambda i: (i, 0)),
        out_shape=jax.ShapeDtypeStruct((n, D_MODEL), F32),
        compiler_params=_params("arbitrary"),
        name="ffn",
    )(h, g_pre, w_gu, w_down, g_post)


def _slc_map_t(s):
    n_slab = s // CMP_STRIDE
    ci = np.arange(n_slab)[None, :] * CMP_STRIDE
    sj = np.arange(s // SLC_BLOCK)[:, None] * SLC_BLOCK
    ov = np.clip(np.minimum(ci + CMP_BLOCK, sj + SLC_BLOCK) - np.maximum(ci, sj), 0, None)
    ov[:, n_slab - 1] = 0
    return jnp.asarray(ov / CMP_BLOCK, dtype=F32)


def _tile_biases(tq):
    key = np.arange(TK)[:, None]
    qry = np.arange(tq)[None, :]
    none = np.zeros((TK, tq), np.float32)
    causal = np.where(key <= qry, 0.0, NEG).astype(np.float32)
    window_start = np.where(key > qry, 0.0, NEG).astype(np.float32)
    masked = np.full((TK, tq), NEG, np.float32)
    return jnp.asarray(np.stack([none, causal, window_start, masked]), dtype=BF16)


def kernel(x, mem, positions, norm_mix_pre, w_in, conv_w, pe_kc, w1_kc, w2_kc, pe_vc, w1_vc, w2_vc,
           norm_conv_out, norm_attn_out, w_out, norm_mix_post, norm_x_pre, norm_mem, w_q_x, w_kv_x,
           w_o_x, norm_x_post, norm_ffn_pre, w_gate_up, w_down, norm_ffn_post):
    b, s, _ = x.shape
    row = lambda v: v.reshape(1, -1).astype(F32)

    cuts = np.cumsum([0, CONV_WIDTH, CONV_WIDTH, CONV_WIDTH, ATT_WIDTH] + [KV_WIDTH] * 6 + [3 * ATT_HEADS])
    col = lambda k: w_in[:, cuts[k]:cuts[k + 1]]
    w_row = jnp.concatenate([col(0), col(1), col(2), col(4), col(5)], axis=1).astype(BF16)
    gate_cols = np.zeros((KV_GROUPS * GATE_ROWS,), np.int32)
    gate_live = np.zeros((KV_GROUPS * GATE_ROWS,), np.float32)
    for g in range(KV_GROUPS):
        for c in range(3):
            for r in range(Q_PER_KV):
                gate_cols[g * GATE_ROWS + c * Q_PER_KV + r] = (g * Q_PER_KV + r) * 3 + c
                gate_live[g * GATE_ROWS + c * Q_PER_KV + r] = 1.0
    w_gate = col(10)[:, gate_cols] * gate_live[None, :]
    w_t = jnp.concatenate([col(3), col(8), col(6), col(9), col(7), w_gate], axis=1).T.astype(BF16)

    inv = ROPE_THETA ** (-jnp.arange(HALF, dtype=F32) / HALF)
    yconv, kvc, qt, kaug, vaug, gt = _in_proj(
        x, positions.reshape(b, 1, s), inv.reshape(HALF, 1), row(norm_mix_pre), w_row, w_t,
        conv_w.astype(F32), row(norm_conv_out))

    n_slab = s // CMP_STRIDE
    pos_c = positions[:, np.minimum(np.arange(n_slab) * CMP_STRIDE + CMP_BLOCK - 1, s - 1)]
    kcmp, vcmpt = _compress(
        kvc, pos_c.reshape(b, n_slab, 1), inv.reshape(1, HALF),
        pe_kc.reshape(1, -1), w1_kc.astype(BF16), w2_kc.astype(BF16),
        pe_vc.reshape(1, -1), w1_vc.astype(BF16), w2_vc.T.astype(BF16))

    yatt_t = _nsa(qt, kcmp, vcmpt, kaug, vaug, gt, _slc_map_t(s), _tile_biases(min(TQ, s)))

    kx, vxt = _mem_kv(mem, row(norm_mem), w_kv_x[:, :D_MODEL].astype(BF16), w_kv_x[:, D_MODEL:].T.astype(BF16))
    h2 = _mix_xattn(
        x, yconv, yatt_t, norm_attn_out.reshape(-1, 1).astype(F32),
        w_out[:CONV_WIDTH].astype(BF16), w_out[CONV_WIDTH:].astype(BF16),
        row(norm_mix_post), row(norm_x_pre), w_q_x.T.astype(BF16), kx, vxt, w_o_x.astype(BF16), row(norm_x_post))

    out = _ffn(h2.reshape(b * s, D_MODEL), row(norm_ffn_pre), w_gate_up.astype(BF16), w_down.astype(BF16),
               row(norm_ffn_post))
    return out.reshape(b, s, D_MODEL)
```

```python
import math

import numpy as np
import jax
import jax.numpy as jnp
from jax import lax
from jax.experimental import pallas as pl
from jax.experimental.pallas import tpu as pltpu

D_MODEL = 1024
CONV_WIDTH = 512
CONV_K = 3
ATT_HEADS = 8
HEAD_DIM = 64
HALF = HEAD_DIM // 2
ATT_WIDTH = ATT_HEADS * HEAD_DIM
KV_GROUPS = 2
Q_PER_KV = ATT_HEADS // KV_GROUPS
KV_WIDTH = KV_GROUPS * HEAD_DIM
GROUP_WIDTH = Q_PER_KV * HEAD_DIM
CMP_BLOCK = 32
CMP_STRIDE = 16
CMP_HIDDEN = 256
SLC_BLOCK = 64
N_SELECT = 16
WINDOW = 512
ROPE_THETA = 10000.0
X_HEADS = 4
X_HEAD_DIM = D_MODEL // X_HEADS
D_FF = 2816
EPS = 1e-6
FORCE = 1e4
NEG = -1e30
GATE_ROWS = 16
K_AUG = 2 * HEAD_DIM
V_AUG = HEAD_DIM + 16
WIN, SEL = 0, 1
_RANK_STEP = 16
_SLOT_WIN, _N_SLOTS = 2, 5
_BIAS_NONE, _BIAS_CAUSAL, _BIAS_WINDOW_START, _BIAS_MASKED = 0, 1, 2, 3

TM_PROJ = 1024
TM_MIX = 1024
TM_FFN = 512
TM_FFN_SUB = 256
SUB_ROWS = 512
TQ = 256
TK = 256
VMEM_LIMIT = 56 * 1024 * 1024

F32 = jnp.float32
BF16 = jnp.bfloat16

_NT = (((1,), (1,)), ((), ()))
_TN = (((0,), (0,)), ((), ()))


def _dot(a, b):
    return jnp.dot(a, b, preferred_element_type=F32)


def _dot_nt(a, b):
    return lax.dot_general(a, b, _NT, preferred_element_type=F32)


def _dot_tn(a, b):
    return lax.dot_general(a, b, _TN, preferred_element_type=F32)


def _rms_rows(x, g):
    return x * lax.rsqrt(jnp.mean(x * x, axis=-1, keepdims=True) + EPS) * g


def _params(*sem):
    return pltpu.CompilerParams(dimension_semantics=sem, vmem_limit_bytes=VMEM_LIMIT)


def _whole(shape):
    nd = len(shape)
    return pl.BlockSpec(shape, lambda *_: (0,) * nd, pipeline_mode=pl.Buffered(1))


def _in_proj_kernel(x_ref, pos_ref, inv_ref, g_ref, wrow_ref, wt_ref, convw_ref, gconv_ref,
                    yconv_ref, kvc_ref, qt_ref, kaug_ref, vaug_ref, gt_ref, carry_ref):
    @pl.when(pl.program_id(1) == 0)
    def _():
        carry_ref[...] = jnp.zeros_like(carry_ref)

    tm = x_ref.shape[1]
    subs = [slice(c0, c0 + SUB_ROWS) for c0 in range(0, tm, SUB_ROWS)]
    u = [_rms_rows(x_ref[0, rows, :], g_ref[...]).astype(BF16) for rows in subs]
    prow = [_dot(v, wrow_ref[...]) for v in u]
    pt = [_dot_nt(wt_ref[...], v) for v in u]
    for rows, pr, ptr in zip(subs, prow, pt):
        _in_proj_epilogue(rows, pl.program_id(1) * tm + rows.start, pr, ptr, pos_ref, inv_ref, convw_ref, gconv_ref,
                          yconv_ref, kvc_ref, qt_ref, kaug_ref, vaug_ref, gt_ref, carry_ref)


def _in_proj_epilogue(rows, tok0, prow, pt, pos_ref, inv_ref, convw_ref, gconv_ref,
                      yconv_ref, kvc_ref, qt_ref, kaug_ref, vaug_ref, gt_ref, carry_ref):
    n = rows.stop - rows.start

    bg = prow[:, 0:CONV_WIDTH]
    z = prow[:, CONV_WIDTH:2 * CONV_WIDTH] * prow[:, 2 * CONV_WIDTH:3 * CONV_WIDTH]
    prev = carry_ref[...]
    row = lax.broadcasted_iota(jnp.int32, (n, CONV_WIDTH), 0)
    z1 = jnp.where(row == 0, prev[7:8, :], pltpu.roll(z, 1, axis=0))
    z2 = jnp.where(row == 0, prev[6:7, :], jnp.where(row == 1, prev[7:8, :], pltpu.roll(z, 2, axis=0)))
    carry_ref[...] = z[n - 8:n, :]
    cw = convw_ref[...]
    yc = bg * (cw[0:1, :] * z2 + cw[1:2, :] * z1 + cw[2:3, :] * z)
    yconv_ref[0, rows, :] = _rms_rows(yc, gconv_ref[...]).astype(yconv_ref.dtype)

    for c in range(2 * KV_GROUPS):
        lo = 3 * CONV_WIDTH + c * HEAD_DIM
        kvc_ref[0, c, rows, :] = prow[:, lo:lo + HEAD_DIM]

    ang = pos_ref[0, :, rows].astype(F32) * inv_ref[...]
    cos, sin = jnp.cos(ang), jnp.sin(ang)

    def rope_t(base):
        x1 = pt[base:base + HALF, :]
        x2 = pt[base + HALF:base + HEAD_DIM, :]
        return x1 * cos - x2 * sin, x2 * cos + x1 * sin

    scale = HEAD_DIM ** -0.5 * math.log2(math.e)
    for h in range(ATT_HEADS):
        r1, r2 = rope_t(h * HEAD_DIM)
        qt_ref[0, h * HEAD_DIM:h * HEAD_DIM + HALF, rows] = (r1 * scale).astype(qt_ref.dtype)
        qt_ref[0, h * HEAD_DIM + HALF:(h + 1) * HEAD_DIM, rows] = (r2 * scale).astype(qt_ref.dtype)

    tok = tok0 + lax.broadcasted_iota(jnp.int32, (n, HEAD_DIM), 0)
    blk_col = lax.broadcasted_iota(jnp.int32, (n, HEAD_DIM), 1)
    onehot = jnp.where(jnp.right_shift(tok, SLC_BLOCK.bit_length() - 1) == blk_col, 1.0, 0.0)
    for br in (WIN, SEL):
        base = ATT_WIDTH + br * KV_WIDTH
        parts = []
        for g in range(KV_GROUPS):
            parts.extend(rope_t(base + g * HEAD_DIM))
        k_rows = jnp.concatenate(parts, axis=0).T
        extra = onehot if br == SEL else jnp.zeros_like(onehot)
        for g in range(KV_GROUPS):
            k_g = k_rows[:, g * HEAD_DIM:(g + 1) * HEAD_DIM]
            kaug_ref[0, g, br, rows, :] = jnp.concatenate([k_g, extra], axis=-1).astype(kaug_ref.dtype)

    v0 = ATT_WIDTH + 2 * KV_WIDTH
    ones_rows = jnp.where(lax.broadcasted_iota(jnp.int32, (V_AUG - HEAD_DIM, n), 0) == 0, 1.0, 0.0)
    for br in (WIN, SEL):
        for g in range(KV_GROUPS):
            lo = v0 + br * KV_WIDTH + g * HEAD_DIM
            vaug_ref[0, g, br, :, rows] = jnp.concatenate([pt[lo:lo + HEAD_DIM, :], ones_rows], axis=0).astype(vaug_ref.dtype)

    g0 = v0 + 2 * KV_WIDTH
    gt_ref[0, :, rows] = jax.nn.sigmoid(pt[g0:g0 + KV_GROUPS * GATE_ROWS, :])


def _in_proj(x, pos_row, inv_col, g_pre, w_row, w_t, conv_w, g_conv):
    b, s, _ = x.shape
    tm = min(TM_PROJ, s)
    n_row, n_t = w_row.shape[1], w_t.shape[0]
    out_shape = (
        jax.ShapeDtypeStruct((b, s, CONV_WIDTH), BF16),
        jax.ShapeDtypeStruct((b, 2 * KV_GROUPS, s, HEAD_DIM), F32),
        jax.ShapeDtypeStruct((b, ATT_WIDTH, s), BF16),
        jax.ShapeDtypeStruct((b, KV_GROUPS, 2, s, K_AUG), BF16),
        jax.ShapeDtypeStruct((b, KV_GROUPS, 2, V_AUG, s), BF16),
        jax.ShapeDtypeStruct((b, KV_GROUPS * GATE_ROWS, s), F32),
    )
    return pl.pallas_call(
        _in_proj_kernel,
        grid=(b, s // tm),
        in_specs=[
            pl.BlockSpec((1, tm, D_MODEL), lambda i, j: (i, j, 0)),
            pl.BlockSpec((1, 1, tm), lambda i, j: (i, 0, j)),
            _whole((HALF, 1)),
            _whole((1, D_MODEL)),
            _whole((D_MODEL, n_row)),
            _whole((n_t, D_MODEL)),
            _whole((CONV_K, CONV_WIDTH)),
            _whole((1, CONV_WIDTH)),
        ],
        out_specs=(
            pl.BlockSpec((1, tm, CONV_WIDTH), lambda i, j: (i, j, 0)),
            pl.BlockSpec((1, 2 * KV_GROUPS, tm, HEAD_DIM), lambda i, j: (i, 0, j, 0)),
            pl.BlockSpec((1, ATT_WIDTH, tm), lambda i, j: (i, 0, j)),
            pl.BlockSpec((1, KV_GROUPS, 2, tm, K_AUG), lambda i, j: (i, 0, 0, j, 0)),
            pl.BlockSpec((1, KV_GROUPS, 2, V_AUG, tm), lambda i, j: (i, 0, 0, 0, j)),
            pl.BlockSpec((1, KV_GROUPS * GATE_ROWS, tm), lambda i, j: (i, 0, j)),
        ),
        out_shape=out_shape,
        scratch_shapes=[pltpu.VMEM((8, CONV_WIDTH), F32)],
        compiler_params=_params("arbitrary", "arbitrary"),
        name="in_proj",
    )(x, pos_row, inv_col, g_pre, w_row, w_t, conv_w, g_conv)


def _compress_kernel(rk_ref, rv_ref, pos_ref, inv_ref, pek_ref, w1k_ref, w2k_ref, pev_ref, w1v_ref, w2vt_ref,
                     kc_ref, vct_ref):
    half_in = CMP_STRIDE * HEAD_DIM
    n_slab = kc_ref.shape[2]

    def hidden(x_ref, pe_ref, w1_ref):
        a = jnp.zeros((n_slab, CMP_HIDDEN), F32)
        b = jnp.zeros((n_slab, CMP_HIDDEN), F32)
        per_dot = 256 // HEAD_DIM
        for j0 in range(0, CMP_STRIDE, per_dot):
            xcat = jnp.concatenate([x_ref[0, 0, pl.ds(j, n_slab, stride=CMP_STRIDE), :]
                                    for j in range(j0, j0 + per_dot)], axis=-1).astype(BF16)
            lo = j0 * HEAD_DIM
            a = a + _dot(xcat, w1_ref[lo:lo + per_dot * HEAD_DIM, :])
            b = b + _dot(xcat, w1_ref[half_in + lo:half_in + lo + per_dot * HEAD_DIM, :])
        pe = jnp.broadcast_to(pe_ref[...], (8, 2 * half_in)).astype(BF16)
        h = a + pltpu.roll(b, n_slab - 1, axis=0) + _dot(pe, w1_ref[...])[0:1, :]
        return (h * jax.nn.sigmoid(h)).astype(BF16)

    kc = _dot(hidden(rk_ref, pek_ref, w1k_ref), w2k_ref[...])
    ang = pos_ref[0].astype(F32) * inv_ref[...]
    cos, sin = jnp.cos(ang), jnp.sin(ang)
    x1, x2 = kc[:, 0:HALF], kc[:, HALF:HEAD_DIM]
    kc_ref[0, 0] = jnp.concatenate([x1 * cos - x2 * sin, x2 * cos + x1 * sin], axis=-1).astype(kc_ref.dtype)
    vct_ref[0, 0] = _dot_nt(w2vt_ref[...], hidden(rv_ref, pev_ref, w1v_ref)).astype(vct_ref.dtype)


def _compress(kvc, pos_c, inv_row, pek, w1k, w2k, pev, w1v, w2vt):
    b, _, s, _ = kvc.shape
    g, n_slab = KV_GROUPS, s // CMP_STRIDE
    return pl.pallas_call(
        _compress_kernel,
        grid=(b, g),
        in_specs=[
            pl.BlockSpec((1, 1, s, HEAD_DIM), lambda i, j: (i, j, 0, 0)),
            pl.BlockSpec((1, 1, s, HEAD_DIM), lambda i, j: (i, KV_GROUPS + j, 0, 0)),
            pl.BlockSpec((1, n_slab, 1), lambda i, j: (i, 0, 0)),
            _whole((1, HALF)),
            _whole(pek.shape), _whole(w1k.shape), _whole(w2k.shape),
            _whole(pev.shape), _whole(w1v.shape), _whole(w2vt.shape),
        ],
        out_specs=(
            pl.BlockSpec((1, 1, n_slab, HEAD_DIM), lambda i, j: (i, j, 0, 0)),
            pl.BlockSpec((1, 1, HEAD_DIM, n_slab), lambda i, j: (i, j, 0, 0)),
        ),
        out_shape=(
            jax.ShapeDtypeStruct((b, g, n_slab, HEAD_DIM), BF16),
            jax.ShapeDtypeStruct((b, g, HEAD_DIM, n_slab), BF16),
        ),
        compiler_params=_params("arbitrary", "arbitrary"),
        name="compress",
    )(kvc, kvc, pos_c, inv_row, pek, w1k, w2k, pev, w1v, w2vt)


def _nsa_kernel(qt_ref, kc_ref, vct_ref, kaug_ref, vaug_ref, gt_ref, map_ref, tri_ref,
                out_ref, rhs_ref, s_ref, cs_ref, imp_ref, m_ref, acc_ref):
    tq = qt_ref.shape[2]
    n_cmp = kc_ref.shape[2]
    n_slc = map_ref.shape[0]
    i = pl.program_id(2)
    s0 = i * tq
    t = s0 + lax.broadcasted_iota(jnp.int32, (1, tq), 1)
    row_of = lambda r: slice(r * HEAD_DIM, (r + 1) * HEAD_DIM)

    def issue_scores(slot, br, tile):
        k_t = kaug_ref[0, 0, br, pl.ds(pl.multiple_of(tile * TK, TK), TK), :]
        for r in range(Q_PER_KV):
            s_ref[slot, r] = _dot(k_t, rhs_ref[br, r]).astype(s_ref.dtype)

    def softmax_step(slot, br, tile, bias_idx=None):
        v_t = vaug_ref[0, 0, br, :, pl.ds(pl.multiple_of(tile * TK, TK), TK)]
        for r in range(Q_PER_KV):
            s = s_ref[slot, r]
            if bias_idx is not None:
                s = s + tri_ref[bias_idx]
            m_old = m_ref[r]
            m_new = jnp.maximum(m_old, jnp.max(s, axis=0, keepdims=True).astype(F32))
            p = jnp.exp2(s - m_new.astype(s.dtype))
            acc_ref[r] = jnp.exp2(m_old - m_new) * acc_ref[r] + _dot(v_t, p)
            m_ref[r] = m_new

    def reset_state():
        for r in range(Q_PER_KV):
            m_ref[r] = jnp.full((1, tq), NEG, F32)
            acc_ref[r] = jnp.zeros((V_AUG, tq), F32)

    def flush_state(gate_row0):
        for r in range(Q_PER_KV):
            acc = acc_ref[r]
            o = acc[0:HEAD_DIM, :] * (1.0 / jnp.maximum(acc[HEAD_DIM:HEAD_DIM + 1, :], 1e-30))
            out_ref[0, row_of(r), :] = out_ref[0, row_of(r), :] + gt_ref[0, gate_row0 + r:gate_row0 + r + 1, :] * o

    zeros_half = jnp.zeros((K_AUG - HEAD_DIM, tq), BF16)
    for r in range(Q_PER_KV):
        rhs_ref[WIN, r, 0:HEAD_DIM, :] = qt_ref[0, row_of(r), :]
        rhs_ref[WIN, r, HEAD_DIM:K_AUG, :] = zeros_half
        rhs_ref[SEL, r, 0:HEAD_DIM, :] = qt_ref[0, row_of(r), :]
    n_win = WINDOW // TK + 1
    win_tiles = [i - (n_win - 1) + k for k in range(n_win)]
    kc = kc_ref[0, 0]
    for r in range(Q_PER_KV):
        cs_ref[r] = _dot(kc, qt_ref[0, row_of(r), :])
    for k in range(n_win):
        issue_scores(_SLOT_WIN + k, WIN, jnp.maximum(win_tiles[k], 0))

    def window_step(k):
        exists = win_tiles[k] >= 0
        bias_idx = _BIAS_CAUSAL if k == n_win - 1 else jnp.where(exists, _BIAS_WINDOW_START if k == 0 else _BIAS_NONE, _BIAS_MASKED)
        softmax_step(_SLOT_WIN + k, WIN, jnp.maximum(win_tiles[k], 0), bias_idx)

    cmp_end = lax.broadcasted_iota(jnp.int32, (n_cmp, tq), 0) * CMP_STRIDE + (CMP_BLOCK - 1)
    cbias = jnp.where(cmp_end <= t, 0.0, NEG)
    has_cmp = t >= CMP_BLOCK - 1
    vct = vct_ref[0, 0]
    psum = jnp.zeros((n_cmp, tq), F32)
    for r in range(Q_PER_KV):
        s = cs_ref[r] + cbias
        e = jnp.exp2(s - jnp.max(s, axis=0, keepdims=True))
        inv = jnp.where(has_cmp, 1.0 / jnp.maximum(jnp.sum(e, axis=0, keepdims=True), 1e-30), 0.0)
        psum = psum + e * inv
        out_ref[0, row_of(r), :] = (gt_ref[0, r:r + 1, :] * inv) * _dot(vct, e.astype(BF16))

    imp = jnp.dot(map_ref[...], psum, preferred_element_type=F32, precision=lax.Precision.HIGHEST)
    reset_state()
    for k in range(n_win - 1):
        window_step(k)
    blk = lax.broadcasted_iota(jnp.int32, (n_slc, tq), 0)
    cur = jnp.right_shift(t, SLC_BLOCK.bit_length() - 1)
    forced = (blk == 0) | (blk == cur) | (blk == cur - 1)
    imp_ref[...] = jnp.where(forced, FORCE, jnp.where(blk * SLC_BLOCK > t, -FORCE, imp))

    def select_blocks(n_live):
        n_grp = n_live // 8
        imp_g = [imp_ref[8 * g:8 * g + 8, :] for g in range(n_grp)]
        rank_g = [jnp.zeros((8, tq), jnp.int32) for _ in range(n_grp)]
        sub = lax.broadcasted_iota(jnp.int32, (8, tq), 0)
        for j in range(n_live):
            vj = jnp.broadcast_to(imp_ref[j:j + 1, :], (8, tq))
            for g in range(n_grp):
                if 8 * g > j:
                    beats = jnp.where(vj >= imp_g[g], 1, 0)
                elif 8 * g + 7 < j:
                    beats = jnp.where(vj > imp_g[g], 1, 0)
                else:
                    beats = jnp.where(sub > j - 8 * g, jnp.where(vj >= imp_g[g], 1, 0), jnp.where(vj > imp_g[g], 1, 0))
                rank_g[g] = rank_g[g] + beats
        sel = jnp.where(jnp.concatenate(rank_g, axis=0) < min(N_SELECT, n_slc), 0.0, NEG).astype(BF16)
        dead = K_AUG - HEAD_DIM - n_live
        if dead:
            sel = jnp.concatenate([sel, jnp.full((dead, tq), NEG, BF16)], axis=0)
        for r in range(Q_PER_KV):
            rhs_ref[SEL, r, HEAD_DIM:K_AUG, :] = sel

    blocks_per_tile = tq // SLC_BLOCK
    live_sizes = list(range(_RANK_STEP, n_slc, _RANK_STEP)) + [n_slc]
    for v, n_live in enumerate(live_sizes):
        lo = v * _RANK_STEP // blocks_per_tile
        hi = n_live // blocks_per_tile if v + 1 < len(live_sizes) else n_slc // blocks_per_tile
        pl.when((i >= lo) & (i < hi))(lambda n_live=n_live: select_blocks(n_live))
    issue_scores(0, SEL, 0)

    window_step(n_win - 1)
    flush_state(2 * Q_PER_KV)

    reset_state()

    def pair_steps(a):
        issue_scores(1, SEL, a + 1)
        softmax_step(0, SEL, a)
        issue_scores(0, SEL, a + 2)
        softmax_step(1, SEL, a + 1)

    def body(quad, carry):
        pair_steps(4 * quad)
        pair_steps(4 * quad + 2)
        return carry

    n_plain = jnp.right_shift(i, 1)
    n_quad = jnp.right_shift(n_plain, 1)
    lax.fori_loop(0, n_quad, body, 0)

    @pl.when(n_plain != 2 * n_quad)
    def _():
        pair_steps(4 * n_quad)
    @pl.when(i == 2 * n_plain)
    def _():
        softmax_step(0, SEL, i, _BIAS_CAUSAL)

    @pl.when(i != 2 * n_plain)
    def _():
        issue_scores(1, SEL, i)
        softmax_step(0, SEL, i - 1)
        softmax_step(1, SEL, i, _BIAS_CAUSAL)

    flush_state(Q_PER_KV)


def _nsa(qt, kcmp, vcmpt, kaug, vaug, gt, slc_map_t, tri):
    b, _, s = qt.shape
    tq = min(TQ, s)
    n_cmp = kcmp.shape[2]
    n_slc = s // SLC_BLOCK
    assert tq == TK and s % tq == 0 and WINDOW % TK == 0 and n_slc <= K_AUG - HEAD_DIM and n_cmp <= TK
    assert n_slc % 8 == 0 and _SLOT_WIN + WINDOW // TK + 1 <= _N_SLOTS
    assert _RANK_STEP % 8 == 0 and _RANK_STEP % (tq // SLC_BLOCK) == 0 and n_slc % (tq // SLC_BLOCK) == 0
    per_group = lambda shape: pl.BlockSpec((1, 1) + shape, lambda i, j, k: (i, j) + (0,) * len(shape))
    return pl.pallas_call(
        _nsa_kernel,
        grid=(b, KV_GROUPS, s // tq),
        in_specs=[
            pl.BlockSpec((1, GROUP_WIDTH, tq), lambda i, j, k: (i, j, k)),
            per_group((n_cmp, HEAD_DIM)),
            per_group((HEAD_DIM, n_cmp)),
            per_group((2, s, K_AUG)),
            per_group((2, V_AUG, s)),
            pl.BlockSpec((1, GATE_ROWS, tq), lambda i, j, k: (i, j, k)),
            _whole((n_slc, n_cmp)),
            _whole((4, TK, tq)),
        ],
        out_specs=pl.BlockSpec((1, GROUP_WIDTH, tq), lambda i, j, k: (i, j, k)),
        out_shape=jax.ShapeDtypeStruct((b, ATT_WIDTH, s), F32),
        scratch_shapes=[
            pltpu.VMEM((2, Q_PER_KV, K_AUG, tq), BF16),
            pltpu.VMEM((_N_SLOTS, Q_PER_KV, TK, tq), BF16),
            pltpu.VMEM((Q_PER_KV, n_cmp, tq), F32),
            pltpu.VMEM((n_slc, tq), F32),
            pltpu.VMEM((Q_PER_KV, 1, tq), F32),
            pltpu.VMEM((Q_PER_KV, V_AUG, tq), F32),
        ],
        compiler_params=_params("arbitrary", "arbitrary", "arbitrary"),
        name="nsa_attention",
    )(qt, kcmp, vcmpt, kaug, vaug, gt, slc_map_t, tri)


def _mem_kv_kernel(mem_ref, g_ref, wk_ref, wvt_ref, kx_ref, vxt_ref):
    mn = _rms_rows(mem_ref[0], g_ref[...]).astype(BF16)
    kx_ref[0] = _dot(mn, wk_ref[...]).astype(kx_ref.dtype)
    vxt_ref[0] = _dot_nt(wvt_ref[...], mn).astype(vxt_ref.dtype)


def _mem_kv(mem, g_mem, w_k, w_vt):
    b, m, _ = mem.shape
    return pl.pallas_call(
        _mem_kv_kernel,
        grid=(b,),
        in_specs=[pl.BlockSpec((1, m, D_MODEL), lambda i: (i, 0, 0)), _whole((1, D_MODEL)),
                  _whole((D_MODEL, D_MODEL)), _whole((D_MODEL, D_MODEL))],
        out_specs=(pl.BlockSpec((1, m, D_MODEL), lambda i: (i, 0, 0)),
                   pl.BlockSpec((1, D_MODEL, m), lambda i: (i, 0, 0))),
        out_shape=(jax.ShapeDtypeStruct((b, m, D_MODEL), BF16),
                   jax.ShapeDtypeStruct((b, D_MODEL, m), BF16)),
        compiler_params=_params("arbitrary"),
        name="mem_kv",
    )(mem, g_mem, w_k, w_vt)


def _mix_xattn_kernel(x_ref, yconv_ref, yatt_ref, gatt_ref, woa_ref, wob_ref, gpost_ref, gxpre_ref,
                      wqt_ref, kx_ref, vxt_ref, wo_ref, gxpost_ref, out_ref):
    subs = [slice(c0, c0 + SUB_ROWS) for c0 in range(0, x_ref.shape[1], SUB_ROWS)]
    head = lambda h: slice(h * X_HEAD_DIM, (h + 1) * X_HEAD_DIM)

    def yatt_normed(rows):
        ya = yatt_ref[0, :, rows]
        return (ya * lax.rsqrt(jnp.mean(ya * ya, axis=0, keepdims=True) + EPS) * gatt_ref[...]).astype(BF16)

    yan = [yatt_normed(rows) for rows in subs]
    mixed = [_dot(yconv_ref[0, rows, :], woa_ref[...]) + _dot_tn(y, wob_ref[...]) for rows, y in zip(subs, yan)]
    h1 = [x_ref[0, rows, :] + _rms_rows(m, gpost_ref[...]) for rows, m in zip(subs, mixed)]
    hn = [_rms_rows(h, gxpre_ref[...]).astype(BF16) for h in h1]
    qxt = [(_dot_nt(wqt_ref[...], h) * (X_HEAD_DIM ** -0.5 * math.log2(math.e))).astype(BF16) for h in hn]
    scores = [[_dot(kx_ref[0, :, head(h)], q[head(h), :]) for h in range(X_HEADS)] for q in qxt]

    def attend(sc):
        heads = []
        for h in range(X_HEADS):
            e = jnp.exp2(sc[h] - jnp.max(sc[h], axis=0, keepdims=True))
            o = _dot(vxt_ref[0, head(h), :], e.astype(BF16))
            heads.append(o * (1.0 / jnp.sum(e, axis=0, keepdims=True)))
        return jnp.concatenate(heads, axis=0).astype(BF16)

    oxt = [attend(sc) for sc in scores]
    proj = [_dot_tn(o, wo_ref[...]) for o in oxt]
    for rows, h, p in zip(subs, h1, proj):
        out_ref[0, rows, :] = h + _rms_rows(p, gxpost_ref[...])


def _mix_xattn(x, yconv, yatt_t, g_att, w_out_a, w_out_b, g_post, g_xpre, w_qt, kx, vxt, w_o, g_xpost):
    b, s, _ = x.shape
    m = kx.shape[1]
    tm = min(TM_MIX, s)
    return pl.pallas_call(
        _mix_xattn_kernel,
        grid=(b, s // tm),
        in_specs=[
            pl.BlockSpec((1, tm, D_MODEL), lambda i, j: (i, j, 0)),
            pl.BlockSpec((1, tm, CONV_WIDTH), lambda i, j: (i, j, 0)),
            pl.BlockSpec((1, ATT_WIDTH, tm), lambda i, j: (i, 0, j)),
            _whole((ATT_WIDTH, 1)),
            _whole((CONV_WIDTH, D_MODEL)), _whole((ATT_WIDTH, D_MODEL)),
            _whole((1, D_MODEL)), _whole((1, D_MODEL)),
            _whole((D_MODEL, D_MODEL)),
            pl.BlockSpec((1, m, D_MODEL), lambda i, j: (i, 0, 0)),
            pl.BlockSpec((1, D_MODEL, m), lambda i, j: (i, 0, 0)),
            _whole((D_MODEL, D_MODEL)), _whole((1, D_MODEL)),
        ],
        out_specs=pl.BlockSpec((1, tm, D_MODEL), lambda i, j: (i, j, 0)),
        out_shape=jax.ShapeDtypeStruct((b, s, D_MODEL), F32),
        compiler_params=_params("arbitrary", "arbitrary"),
        name="mix_xattn",
    )(x, yconv, yatt_t, g_att, w_out_a, w_out_b, g_post, g_xpre, w_qt, kx, vxt, w_o, g_xpost)


def _ffn_kernel(h_ref, gpre_ref, wgu_ref, wdown_ref, gpost_ref, out_ref):
    subs = [slice(c0, c0 + TM_FFN_SUB) for c0 in range(0, h_ref.shape[0], TM_FFN_SUB)]
    hn = [_rms_rows(h_ref[rows, :], gpre_ref[...]).astype(BF16) for rows in subs]
    gu = [_dot(v, wgu_ref[...]) for v in hn]
    act = [(v[:, 0:D_FF] * jax.nn.sigmoid(v[:, 0:D_FF]) * v[:, D_FF:2 * D_FF]).astype(BF16) for v in gu]
    down = [_dot(a, wdown_ref[...]) for a in act]
    for rows, d in zip(subs, down):
        out_ref[rows, :] = h_ref[rows, :] + _rms_rows(d, gpost_ref[...])


def _ffn(h, g_pre, w_gu, w_down, g_post):
    n, _ = h.shape
    tm = min(TM_FFN, n)
    return pl.pallas_call(
        _ffn_kernel,
        grid=(n // tm,),
        in_specs=[pl.BlockSpec((tm, D_MODEL), lambda i: (i, 0)), _whole((1, D_MODEL)),
                  _whole((D_MODEL, 2 * D_FF)), _whole((D_FF, D_MODEL)), _whole((1, D_MODEL))],
        out_specs=pl.BlockSpec((tm, D_MODEL), lambda i: (i, 0)),
        out_shape=jax.ShapeDtypeStruct((n, D_MODEL), F32),
        compiler_params=_params("arbitrary"),
        name="ffn",
    )(h, g_pre, w_gu, w_down, g_post)


def _slc_map_t(s):
    n_slab = s // CMP_STRIDE
    ci = np.arange(n_slab)[None, :] * CMP_STRIDE
    sj = np.arange(s // SLC_BLOCK)[:, None] * SLC_BLOCK
    ov = np.clip(np.minimum(ci + CMP_BLOCK, sj + SLC_BLOCK) - np.maximum(ci, sj), 0, None)
    ov[:, n_slab - 1] = 0
    return jnp.asarray(ov / CMP_BLOCK, dtype=F32)


def _tile_biases(tq):
    key = np.arange(TK)[:, None]
    qry = np.arange(tq)[None, :]
    none = np.zeros((TK, tq), np.float32)
    causal = np.where(key <= qry, 0.0, NEG).astype(np.float32)
    window_start = np.where(key > qry, 0.0, NEG).astype(np.float32)
    masked = np.full((TK, tq), NEG, np.float32)
    return jnp.asarray(np.stack([none, causal, window_start, masked]), dtype=BF16)


def kernel(x, mem, positions, norm_mix_pre, w_in, conv_w, pe_kc, w1_kc, w2_kc, pe_vc, w1_vc, w2_vc,
           norm_conv_out, norm_attn_out, w_out, norm_mix_post, norm_x_pre, norm_mem, w_q_x, w_kv_x,
           w_o_x, norm_x_post, norm_ffn_pre, w_gate_up, w_down, norm_ffn_post):
    b, s, _ = x.shape
    row = lambda v: v.reshape(1, -1).astype(F32)

    cuts = np.cumsum([0, CONV_WIDTH, CONV_WIDTH, CONV_WIDTH, ATT_WIDTH] + [KV_WIDTH] * 6 + [3 * ATT_HEADS])
    col = lambda k: w_in[:, cuts[k]:cuts[k + 1]]
    w_row = jnp.concatenate([col(0), col(1), col(2), col(4), col(5)], axis=1).astype(BF16)
    gate_cols = np.zeros((KV_GROUPS * GATE_ROWS,), np.int32)
    gate_live = np.zeros((KV_GROUPS * GATE_ROWS,), np.float32)
    for g in range(KV_GROUPS):
        for c in range(3):
            for r in range(Q_PER_KV):
                gate_cols[g * GATE_ROWS + c * Q_PER_KV + r] = (g * Q_PER_KV + r) * 3 + c
                gate_live[g * GATE_ROWS + c * Q_PER_KV + r] = 1.0
    w_gate = col(10)[:, gate_cols] * gate_live[None, :]
    w_t = jnp.concatenate([col(3), col(8), col(6), col(9), col(7), w_gate], axis=1).T.astype(BF16)

    inv = ROPE_THETA ** (-jnp.arange(HALF, dtype=F32) / HALF)
    yconv, kvc, qt, kaug, vaug, gt = _in_proj(
        x, positions.reshape(b, 1, s), inv.reshape(HALF, 1), row(norm_mix_pre), w_row, w_t,
        conv_w.astype(F32), row(norm_conv_out))

    n_slab = s // CMP_STRIDE
    pos_c = positions[:, np.minimum(np.arange(n_slab) * CMP_STRIDE + CMP_BLOCK - 1, s - 1)]
    kcmp, vcmpt = _compress(
        kvc, pos_c.reshape(b, n_slab, 1), inv.reshape(1, HALF),
        pe_kc.reshape(1, -1), w1_kc.astype(BF16), w2_kc.astype(BF16),
        pe_vc.reshape(1, -1), w1_vc.astype(BF16), w2_vc.T.astype(BF16))

    yatt_t = _nsa(qt, kcmp, vcmpt, kaug, vaug, gt, _slc_map_t(s), _tile_biases(min(TQ, s)))

    kx, vxt = _mem_kv(mem, row(norm_mem), w_kv_x[:, :D_MODEL].astype(BF16), w_kv_x[:, D_MODEL:].T.astype(BF16))
    h2 = _mix_xattn(
        x, yconv, yatt_t, norm_attn_out.reshape(-1, 1).astype(F32),
        w_out[:CONV_WIDTH].astype(BF16), w_out[CONV_WIDTH:].astype(BF16),
        row(norm_mix_post), row(norm_x_pre), w_q_x.T.astype(BF16), kx, vxt, w_o_x.astype(BF16), row(norm_x_post))

    out = _ffn(h2.reshape(b * s, D_MODEL), row(norm_ffn_pre), w_gate_up.astype(BF16), w_down.astype(BF16),
               row(norm_ffn_post))
    return out.reshape(b, s, D_MODEL)
```

```python
import math

import numpy as np
import jax
import jax.numpy as jnp
from jax import lax
from jax.experimental import pallas as pl
from jax.experimental.pallas import tpu as pltpu

D_MODEL = 1024
CONV_WIDTH = 512
CONV_K = 3
ATT_HEADS = 8
HEAD_DIM = 64
HALF = HEAD_DIM // 2
ATT_WIDTH = ATT_HEADS * HEAD_DIM
KV_GROUPS = 2
Q_PER_KV = ATT_HEADS // KV_GROUPS
KV_WIDTH = KV_GROUPS * HEAD_DIM
CMP_BLOCK = 32
CMP_STRIDE = 16
CMP_HIDDEN = 256
SLC_BLOCK = 64
N_SELECT = 16
WINDOW = 512
ROPE_THETA = 10000.0
X_HEADS = 4
X_HEAD_DIM = D_MODEL // X_HEADS
D_FF = 2816
EPS = 1e-6
FORCE = 1e4
NEG = -1e30
GATE_ROWS = 16
K_AUG = 2 * HEAD_DIM
V_AUG = HEAD_DIM + 16
WIN, SEL = 0, 1
_RANK_STEP = 16
_SLOT_WIN, _N_SLOTS = 2, 5
_BIAS_NONE, _BIAS_CAUSAL, _BIAS_WINDOW_START, _BIAS_MASKED = 0, 1, 2, 3

TM_PROJ = 1024
TM_MIX = 1024
TM_FFN = 512
TM_FFN_SUB = 256
SUB_ROWS = 512
TQ = 256
TK = 256
VMEM_LIMIT = 56 * 1024 * 1024

F32 = jnp.float32
BF16 = jnp.bfloat16

_NT = (((1,), (1,)), ((), ()))
_TN = (((0,), (0,)), ((), ()))


def _dot(a, b):
    return jnp.dot(a, b, preferred_element_type=F32)


def _dot_nt(a, b):
    return lax.dot_general(a, b, _NT, preferred_element_type=F32)


def _dot_tn(a, b):
    return lax.dot_general(a, b, _TN, preferred_element_type=F32)


def _rms_rows(x, g):
    return x * lax.rsqrt(jnp.mean(x * x, axis=-1, keepdims=True) + EPS) * g


def _params(*sem):
    return pltpu.CompilerParams(dimension_semantics=sem, vmem_limit_bytes=VMEM_LIMIT)


def _whole(shape):
    nd = len(shape)
    return pl.BlockSpec(shape, lambda *_: (0,) * nd, pipeline_mode=pl.Buffered(1))


def _in_proj_kernel(x_ref, pos_ref, inv_ref, g_ref, wrow_ref, wt_ref, convw_ref, gconv_ref,
                    yconv_ref, kvc_ref, qt_ref, kaug_ref, vaug_ref, gt_ref, carry_ref):
    @pl.when(pl.program_id(1) == 0)
    def _():
        carry_ref[...] = jnp.zeros_like(carry_ref)

    tm = x_ref.shape[1]
    subs = [slice(c0, c0 + SUB_ROWS) for c0 in range(0, tm, SUB_ROWS)]
    u = [_rms_rows(x_ref[0, rows, :], g_ref[...]).astype(BF16) for rows in subs]
    prow = [_dot(v, wrow_ref[...]) for v in u]
    pt = [_dot_nt(wt_ref[...], v) for v in u]
    for rows, pr, ptr in zip(subs, prow, pt):
        _in_proj_epilogue(rows, pl.program_id(1) * tm + rows.start, pr, ptr, pos_ref, inv_ref, convw_ref, gconv_ref,
                          yconv_ref, kvc_ref, qt_ref, kaug_ref, vaug_ref, gt_ref, carry_ref)


def _in_proj_epilogue(rows, tok0, prow, pt, pos_ref, inv_ref, convw_ref, gconv_ref,
                      yconv_ref, kvc_ref, qt_ref, kaug_ref, vaug_ref, gt_ref, carry_ref):
    n = rows.stop - rows.start

    bg = prow[:, 0:CONV_WIDTH]
    z = prow[:, CONV_WIDTH:2 * CONV_WIDTH] * prow[:, 2 * CONV_WIDTH:3 * CONV_WIDTH]
    prev = carry_ref[...]
    row = lax.broadcasted_iota(jnp.int32, (n, CONV_WIDTH), 0)
    z1 = jnp.where(row == 0, prev[7:8, :], pltpu.roll(z, 1, axis=0))
    z2 = jnp.where(row == 0, prev[6:7, :], jnp.where(row == 1, prev[7:8, :], pltpu.roll(z, 2, axis=0)))
    carry_ref[...] = z[n - 8:n, :]
    cw = convw_ref[...]
    yc = bg * (cw[0:1, :] * z2 + cw[1:2, :] * z1 + cw[2:3, :] * z)
    yconv_ref[0, rows, :] = _rms_rows(yc, gconv_ref[...]).astype(yconv_ref.dtype)

    for c in range(2 * KV_GROUPS):
        lo = 3 * CONV_WIDTH + c * HEAD_DIM
        kvc_ref[0, c, rows, :] = prow[:, lo:lo + HEAD_DIM]

    ang = pos_ref[0, :, rows].astype(F32) * inv_ref[...]
    cos, sin = jnp.cos(ang), jnp.sin(ang)

    def rope_t(base):
        x1 = pt[base:base + HALF, :]
        x2 = pt[base + HALF:base + HEAD_DIM, :]
        return x1 * cos - x2 * sin, x2 * cos + x1 * sin

    scale = HEAD_DIM ** -0.5 * math.log2(math.e)
    for h in range(ATT_HEADS):
        r1, r2 = rope_t(h * HEAD_DIM)
        qt_ref[0, h * HEAD_DIM:h * HEAD_DIM + HALF, rows] = (r1 * scale).astype(qt_ref.dtype)
        qt_ref[0, h * HEAD_DIM + HALF:(h + 1) * HEAD_DIM, rows] = (r2 * scale).astype(qt_ref.dtype)

    tok = tok0 + lax.broadcasted_iota(jnp.int32, (n, HEAD_DIM), 0)
    blk_col = lax.broadcasted_iota(jnp.int32, (n, HEAD_DIM), 1)
    onehot = jnp.where(jnp.right_shift(tok, SLC_BLOCK.bit_length() - 1) == blk_col, 1.0, 0.0)
    for br in (WIN, SEL):
        base = ATT_WIDTH + br * KV_WIDTH
        parts = []
        for g in range(KV_GROUPS):
            parts.extend(rope_t(base + g * HEAD_DIM))
        k_rows = jnp.concatenate(parts, axis=0).T
        extra = onehot if br == SEL else jnp.zeros_like(onehot)
        for g in range(KV_GROUPS):
            k_g = k_rows[:, g * HEAD_DIM:(g + 1) * HEAD_DIM]
            kaug_ref[0, g, br, rows, :] = jnp.concatenate([k_g, extra], axis=-1).astype(kaug_ref.dtype)

    v0 = ATT_WIDTH + 2 * KV_WIDTH
    ones_rows = jnp.where(lax.broadcasted_iota(jnp.int32, (V_AUG - HEAD_DIM, n), 0) == 0, 1.0, 0.0)
    for br in (WIN, SEL):
        for g in range(KV_GROUPS):
            lo = v0 + br * KV_WIDTH + g * HEAD_DIM
            vaug_ref[0, g, br, :, rows] = jnp.concatenate([pt[lo:lo + HEAD_DIM, :], ones_rows], axis=0).astype(vaug_ref.dtype)

    g0 = v0 + 2 * KV_WIDTH
    gt_ref[0, :, rows] = jax.nn.sigmoid(pt[g0:g0 + KV_GROUPS * GATE_ROWS, :])


def _in_proj(x, pos_row, inv_col, g_pre, w_row, w_t, conv_w, g_conv):
    b, s, _ = x.shape
    tm = min(TM_PROJ, s)
    n_row, n_t = w_row.shape[1], w_t.shape[0]
    out_shape = (
        jax.ShapeDtypeStruct((b, s, CONV_WIDTH), BF16),
        jax.ShapeDtypeStruct((b, 2 * KV_GROUPS, s, HEAD_DIM), F32),
        jax.ShapeDtypeStruct((b, ATT_WIDTH, s), BF16),
        jax.ShapeDtypeStruct((b, KV_GROUPS, 2, s, K_AUG), BF16),
        jax.ShapeDtypeStruct((b, KV_GROUPS, 2, V_AUG, s), BF16),
        jax.ShapeDtypeStruct((b, KV_GROUPS * GATE_ROWS, s), F32),
    )
    return pl.pallas_call(
        _in_proj_kernel,
        grid=(b, s // tm),
        in_specs=[
            pl.BlockSpec((1, tm, D_MODEL), lambda i, j: (i, j, 0)),
            pl.BlockSpec((1, 1, tm), lambda i, j: (i, 0, j)),
            _whole((HALF, 1)),
            _whole((1, D_MODEL)),
            _whole((D_MODEL, n_row)),
            _whole((n_t, D_MODEL)),
            _whole((CONV_K, CONV_WIDTH)),
            _whole((1, CONV_WIDTH)),
        ],
        out_specs=(
            pl.BlockSpec((1, tm, CONV_WIDTH), lambda i, j: (i, j, 0)),
            pl.BlockSpec((1, 2 * KV_GROUPS, tm, HEAD_DIM), lambda i, j: (i, 0, j, 0)),
            pl.BlockSpec((1, ATT_WIDTH, tm), lambda i, j: (i, 0, j)),
            pl.BlockSpec((1, KV_GROUPS, 2, tm, K_AUG), lambda i, j: (i, 0, 0, j, 0)),
            pl.BlockSpec((1, KV_GROUPS, 2, V_AUG, tm), lambda i, j: (i, 0, 0, 0, j)),
            pl.BlockSpec((1, KV_GROUPS * GATE_ROWS, tm), lambda i, j: (i, 0, j)),
        ),
        out_shape=out_shape,
        scratch_shapes=[pltpu.VMEM((8, CONV_WIDTH), F32)],
        compiler_params=_params("arbitrary", "arbitrary"),
        name="in_proj",
    )(x, pos_row, inv_col, g_pre, w_row, w_t, conv_w, g_conv)


def _compress_kernel(rk_ref, rv_ref, pos_ref, inv_ref, pek_ref, w1k_ref, w2k_ref, pev_ref, w1v_ref, w2vt_ref,
                     kc_ref, vct_ref):
    half_in = CMP_STRIDE * HEAD_DIM
    n_slab = kc_ref.shape[2]

    def hidden(x_ref, pe_ref, w1_ref):
        a = jnp.zeros((n_slab, CMP_HIDDEN), F32)
        b = jnp.zeros((n_slab, CMP_HIDDEN), F32)
        per_dot = 256 // HEAD_DIM
        for j0 in range(0, CMP_STRIDE, per_dot):
            xcat = jnp.concatenate([x_ref[0, 0, pl.ds(j, n_slab, stride=CMP_STRIDE), :]
                                    for j in range(j0, j0 + per_dot)], axis=-1).astype(BF16)
            lo = j0 * HEAD_DIM
            a = a + _dot(xcat, w1_ref[lo:lo + per_dot * HEAD_DIM, :])
            b = b + _dot(xcat, w1_ref[half_in + lo:half_in + lo + per_dot * HEAD_DIM, :])
        pe = jnp.broadcast_to(pe_ref[...], (8, 2 * half_in)).astype(BF16)
        h = a + pltpu.roll(b, n_slab - 1, axis=0) + _dot(pe, w1_ref[...])[0:1, :]
        return (h * jax.nn.sigmoid(h)).astype(BF16)

    kc = _dot(hidden(rk_ref, pek_ref, w1k_ref), w2k_ref[...])
    ang = pos_ref[0].astype(F32) * inv_ref[...]
    cos, sin = jnp.cos(ang), jnp.sin(ang)
    x1, x2 = kc[:, 0:HALF], kc[:, HALF:HEAD_DIM]
    kc_ref[0, 0] = jnp.concatenate([x1 * cos - x2 * sin, x2 * cos + x1 * sin], axis=-1).astype(kc_ref.dtype)
    vct_ref[0, 0] = _dot_nt(w2vt_ref[...], hidden(rv_ref, pev_ref, w1v_ref)).astype(vct_ref.dtype)


def _compress(kvc, pos_c, inv_row, pek, w1k, w2k, pev, w1v, w2vt):
    b, _, s, _ = kvc.shape
    g, n_slab = KV_GROUPS, s // CMP_STRIDE
    return pl.pallas_call(
        _compress_kernel,
        grid=(b, g),
        in_specs=[
            pl.BlockSpec((1, 1, s, HEAD_DIM), lambda i, j: (i, j, 0, 0)),
            pl.BlockSpec((1, 1, s, HEAD_DIM), lambda i, j: (i, KV_GROUPS + j, 0, 0)),
            pl.BlockSpec((1, n_slab, 1), lambda i, j: (i, 0, 0)),
            _whole((1, HALF)),
            _whole(pek.shape), _whole(w1k.shape), _whole(w2k.shape),
            _whole(pev.shape), _whole(w1v.shape), _whole(w2vt.shape),
        ],
        out_specs=(
            pl.BlockSpec((1, 1, n_slab, HEAD_DIM), lambda i, j: (i, j, 0, 0)),
            pl.BlockSpec((1, 1, HEAD_DIM, n_slab), lambda i, j: (i, j, 0, 0)),
        ),
        out_shape=(
            jax.ShapeDtypeStruct((b, g, n_slab, HEAD_DIM), BF16),
            jax.ShapeDtypeStruct((b, g, HEAD_DIM, n_slab), BF16),
        ),
        compiler_params=_params("arbitrary", "arbitrary"),
        name="compress",
    )(kvc, kvc, pos_c, inv_row, pek, w1k, w2k, pev, w1v, w2vt)


def _nsa_kernel(qt_ref, kc_ref, vct_ref, kaug_ref, vaug_ref, gt_ref, map_ref, tri_ref,
                out_ref, rhs_ref, s_ref, cs_ref, imp_ref, m_ref, acc_ref):
    tq = qt_ref.shape[2]
    n_cmp = kc_ref.shape[2]
    n_slc = map_ref.shape[0]
    i = pl.program_id(1)
    s0 = i * tq
    t = s0 + lax.broadcasted_iota(jnp.int32, (1, tq), 1)
    row_of = lambda h: slice(h * HEAD_DIM, (h + 1) * HEAD_DIM)
    heads_of = lambda g: range(g * Q_PER_KV, (g + 1) * Q_PER_KV)

    def issue_scores(slot, br, tile):
        for g in range(KV_GROUPS):
            k_t = kaug_ref[0, g, br, pl.ds(pl.multiple_of(tile * TK, TK), TK), :]
            for h in heads_of(g):
                s_ref[slot, h] = _dot(k_t, rhs_ref[br, h]).astype(s_ref.dtype)

    def softmax_step(slot, br, tile, bias_idx=None, first=False):
        for g in range(KV_GROUPS):
            v_t = vaug_ref[0, g, br, :, pl.ds(pl.multiple_of(tile * TK, TK), TK)]
            for h in heads_of(g):
                s = s_ref[slot, h]
                if bias_idx is not None:
                    s = s + tri_ref[bias_idx]
                if first:
                    m_new = jnp.max(s, axis=0, keepdims=True).astype(F32)
                    acc_ref[h] = _dot(v_t, jnp.exp2(s - m_new.astype(s.dtype)))
                else:
                    m_old = m_ref[h]
                    m_new = jnp.maximum(m_old, jnp.max(s, axis=0, keepdims=True).astype(F32))
                    p = jnp.exp2(s - m_new.astype(s.dtype))
                    acc_ref[h] = jnp.exp2(m_old - m_new) * acc_ref[h] + _dot(v_t, p)
                m_ref[h] = m_new

    def gate_row(branch, h):
        row = (h // Q_PER_KV) * GATE_ROWS + branch * Q_PER_KV + h % Q_PER_KV
        return gt_ref[0, row:row + 1, :]

    def flush_state(branch):
        for h in range(ATT_HEADS):
            acc = acc_ref[h]
            o = acc[0:HEAD_DIM, :] * (1.0 / jnp.maximum(acc[HEAD_DIM:HEAD_DIM + 1, :], 1e-30))
            out_ref[0, row_of(h), :] = out_ref[0, row_of(h), :] + gate_row(branch, h) * o

    zeros_half = jnp.zeros((K_AUG - HEAD_DIM, tq), BF16)
    for h in range(ATT_HEADS):
        rhs_ref[WIN, h, 0:HEAD_DIM, :] = qt_ref[0, row_of(h), :]
        rhs_ref[WIN, h, HEAD_DIM:K_AUG, :] = zeros_half
        rhs_ref[SEL, h, 0:HEAD_DIM, :] = qt_ref[0, row_of(h), :]
    n_win = WINDOW // TK + 1
    win_tiles = [i - (n_win - 1) + k for k in range(n_win)]
    for g in range(KV_GROUPS):
        kc = kc_ref[0, g]
        for h in heads_of(g):
            cs_ref[h] = _dot(kc, qt_ref[0, row_of(h), :])
    for k in range(n_win):
        issue_scores(_SLOT_WIN + k, WIN, jnp.maximum(win_tiles[k], 0))

    def window_step(k):
        exists = win_tiles[k] >= 0
        bias_idx = _BIAS_CAUSAL if k == n_win - 1 else jnp.where(exists, _BIAS_WINDOW_START if k == 0 else _BIAS_NONE, _BIAS_MASKED)
        softmax_step(_SLOT_WIN + k, WIN, jnp.maximum(win_tiles[k], 0), bias_idx, first=(k == 0))

    cmp_end = lax.broadcasted_iota(jnp.int32, (n_cmp, tq), 0) * CMP_STRIDE + (CMP_BLOCK - 1)
    cbias = jnp.where(cmp_end <= t, 0.0, NEG)
    has_cmp = t >= CMP_BLOCK - 1
    psum = []
    for g in range(KV_GROUPS):
        vct = vct_ref[0, g]
        acc = jnp.zeros((n_cmp, tq), F32)
        for h in heads_of(g):
            s = cs_ref[h] + cbias
            e = jnp.exp2(s - jnp.max(s, axis=0, keepdims=True))
            inv = jnp.where(has_cmp, 1.0 / jnp.maximum(jnp.sum(e, axis=0, keepdims=True), 1e-30), 0.0)
            acc = acc + e * inv
            out_ref[0, row_of(h), :] = (gate_row(0, h) * inv) * _dot(vct, e.astype(BF16))
        psum.append(acc)

    imp = [jnp.dot(map_ref[...], p, preferred_element_type=F32, precision=lax.Precision.HIGHEST) for p in psum]
    for k in range(n_win - 1):
        window_step(k)
    blk = lax.broadcasted_iota(jnp.int32, (n_slc, tq), 0)
    cur = jnp.right_shift(t, SLC_BLOCK.bit_length() - 1)
    forced = (blk == 0) | (blk == cur) | (blk == cur - 1)
    for g in range(KV_GROUPS):
        imp_ref[g] = jnp.where(forced, FORCE, jnp.where(blk * SLC_BLOCK > t, -FORCE, imp[g]))

    def select_blocks(n_live, g):
        n_grp = n_live // 8
        imp_g = [imp_ref[g, 8 * a:8 * a + 8, :] for a in range(n_grp)]
        rank_g = [jnp.zeros((8, tq), jnp.int32) for _ in range(n_grp)]
        sub = lax.broadcasted_iota(jnp.int32, (8, tq), 0)
        for j in range(n_live):
            vj = jnp.broadcast_to(imp_ref[g, j:j + 1, :], (8, tq))
            for a in range(n_grp):
                if 8 * a > j:
                    beats = jnp.where(vj >= imp_g[a], 1, 0)
                elif 8 * a + 7 < j:
                    beats = jnp.where(vj > imp_g[a], 1, 0)
                else:
                    beats = jnp.where(sub > j - 8 * a, jnp.where(vj >= imp_g[a], 1, 0), jnp.where(vj > imp_g[a], 1, 0))
                rank_g[a] = rank_g[a] + beats
        sel = jnp.where(jnp.concatenate(rank_g, axis=0) < min(N_SELECT, n_slc), 0.0, NEG).astype(BF16)
        dead = K_AUG - HEAD_DIM - n_live
        if dead:
            sel = jnp.concatenate([sel, jnp.full((dead, tq), NEG, BF16)], axis=0)
        for h in heads_of(g):
            rhs_ref[SEL, h, HEAD_DIM:K_AUG, :] = sel

    def select_all(n_live):
        for g in range(KV_GROUPS):
            select_blocks(n_live, g)

    blocks_per_tile = tq // SLC_BLOCK
    live_sizes = list(range(_RANK_STEP, n_slc, _RANK_STEP)) + [n_slc]
    for v, n_live in enumerate(live_sizes):
        lo = v * _RANK_STEP // blocks_per_tile
        hi = n_live // blocks_per_tile if v + 1 < len(live_sizes) else n_slc // blocks_per_tile
        pl.when((i >= lo) & (i < hi))(lambda n_live=n_live: select_all(n_live))
    issue_scores(0, SEL, 0)

    window_step(n_win - 1)
    flush_state(2)

    for h in range(ATT_HEADS):
        m_ref[h] = jnp.full((1, tq), NEG, F32)
        acc_ref[h] = jnp.zeros((V_AUG, tq), F32)

    def pair_steps(a):
        issue_scores(1, SEL, a + 1)
        softmax_step(0, SEL, a)
        issue_scores(0, SEL, a + 2)
        softmax_step(1, SEL, a + 1)

    def body(quad, carry):
        pair_steps(4 * quad)
        pair_steps(4 * quad + 2)
        return carry

    n_plain = jnp.right_shift(i, 1)
    n_quad = jnp.right_shift(n_plain, 1)
    lax.fori_loop(0, n_quad, body, 0)

    @pl.when(n_plain != 2 * n_quad)
    def _():
        pair_steps(4 * n_quad)

    @pl.when(i == 2 * n_plain)
    def _():
        softmax_step(0, SEL, i, _BIAS_CAUSAL)

    @pl.when(i != 2 * n_plain)
    def _():
        issue_scores(1, SEL, i)
        softmax_step(0, SEL, i - 1)
        softmax_step(1, SEL, i, _BIAS_CAUSAL)

    flush_state(1)


def _nsa(qt, kcmp, vcmpt, kaug, vaug, gt, slc_map_t, tri):
    b, _, s = qt.shape
    tq = min(TQ, s)
    n_cmp = kcmp.shape[2]
    n_slc = s // SLC_BLOCK
    assert tq == TK and s % tq == 0 and WINDOW % TK == 0 and n_slc <= K_AUG - HEAD_DIM and n_cmp <= TK
    assert n_slc % 8 == 0 and _SLOT_WIN + WINDOW // TK + 1 <= _N_SLOTS
    assert _RANK_STEP % 8 == 0 and _RANK_STEP % (tq // SLC_BLOCK) == 0 and n_slc % (tq // SLC_BLOCK) == 0
    per_batch = lambda shape: pl.BlockSpec((1,) + shape, lambda i, k: (i,) + (0,) * len(shape))
    return pl.pallas_call(
        _nsa_kernel,
        grid=(b, s // tq),
        in_specs=[
            pl.BlockSpec((1, ATT_WIDTH, tq), lambda i, k: (i, 0, k)),
            per_batch((KV_GROUPS, n_cmp, HEAD_DIM)),
            per_batch((KV_GROUPS, HEAD_DIM, n_cmp)),
            per_batch((KV_GROUPS, 2, s, K_AUG)),
            per_batch((KV_GROUPS, 2, V_AUG, s)),
            pl.BlockSpec((1, KV_GROUPS * GATE_ROWS, tq), lambda i, k: (i, 0, k)),
            _whole((n_slc, n_cmp)),
            _whole((4, TK, tq)),
        ],
        out_specs=pl.BlockSpec((1, ATT_WIDTH, tq), lambda i, k: (i, 0, k)),
        out_shape=jax.ShapeDtypeStruct((b, ATT_WIDTH, s), F32),
        scratch_shapes=[
            pltpu.VMEM((2, ATT_HEADS, K_AUG, tq), BF16),
            pltpu.VMEM((_N_SLOTS, ATT_HEADS, TK, tq), BF16),
            pltpu.VMEM((ATT_HEADS, n_cmp, tq), F32),
            pltpu.VMEM((KV_GROUPS, n_slc, tq), F32),
            pltpu.VMEM((ATT_HEADS, 1, tq), F32),
            pltpu.VMEM((ATT_HEADS, V_AUG, tq), F32),
        ],
        compiler_params=_params("arbitrary", "arbitrary"),
        name="nsa_attention",
    )(qt, kcmp, vcmpt, kaug, vaug, gt, slc_map_t, tri)


def _mem_kv_kernel(mem_ref, g_ref, wk_ref, wvt_ref, kx_ref, vxt_ref):
    mn = _rms_rows(mem_ref[0], g_ref[...]).astype(BF16)
    kx_ref[0] = _dot(mn, wk_ref[...]).astype(kx_ref.dtype)
    vxt_ref[0] = _dot_nt(wvt_ref[...], mn).astype(vxt_ref.dtype)


def _mem_kv(mem, g_mem, w_k, w_vt):
    b, m, _ = mem.shape
    return pl.pallas_call(
        _mem_kv_kernel,
        grid=(b,),
        in_specs=[pl.BlockSpec((1, m, D_MODEL), lambda i: (i, 0, 0)), _whole((1, D_MODEL)),
                  _whole((D_MODEL, D_MODEL)), _whole((D_MODEL, D_MODEL))],
        out_specs=(pl.BlockSpec((1, m, D_MODEL), lambda i: (i, 0, 0)),
                   pl.BlockSpec((1, D_MODEL, m), lambda i: (i, 0, 0))),
        out_shape=(jax.ShapeDtypeStruct((b, m, D_MODEL), BF16),
                   jax.ShapeDtypeStruct((b, D_MODEL, m), BF16)),
        compiler_params=_params("arbitrary"),
        name="mem_kv",
    )(mem, g_mem, w_k, w_vt)


def _mix_xattn_kernel(x_ref, yconv_ref, yatt_ref, gatt_ref, woa_ref, wob_ref, gpost_ref, gxpre_ref,
                      wqt_ref, kx_ref, vxt_ref, wo_ref, gxpost_ref, out_ref):
    subs = [slice(c0, c0 + SUB_ROWS) for c0 in range(0, x_ref.shape[1], SUB_ROWS)]
    head = lambda h: slice(h * X_HEAD_DIM, (h + 1) * X_HEAD_DIM)

    def yatt_normed(rows):
        ya = yatt_ref[0, :, rows]
        return (ya * lax.rsqrt(jnp.mean(ya * ya, axis=0, keepdims=True) + EPS) * gatt_ref[...]).astype(BF16)

    yan = [yatt_normed(rows) for rows in subs]
    mixed = [_dot(yconv_ref[0, rows, :], woa_ref[...]) + _dot_tn(y, wob_ref[...]) for rows, y in zip(subs, yan)]
    h1 = [x_ref[0, rows, :] + _rms_rows(m, gpost_ref[...]) for rows, m in zip(subs, mixed)]
    hn = [_rms_rows(h, gxpre_ref[...]).astype(BF16) for h in h1]
    qxt = [(_dot_nt(wqt_ref[...], h) * (X_HEAD_DIM ** -0.5 * math.log2(math.e))).astype(BF16) for h in hn]
    scores = [[_dot(kx_ref[0, :, head(h)], q[head(h), :]) for h in range(X_HEADS)] for q in qxt]

    def attend(sc):
        heads = []
        for h in range(X_HEADS):
            e = jnp.exp2(sc[h] - jnp.max(sc[h], axis=0, keepdims=True))
            o = _dot(vxt_ref[0, head(h), :], e.astype(BF16))
            heads.append(o * (1.0 / jnp.sum(e, axis=0, keepdims=True)))
        return jnp.concatenate(heads, axis=0).astype(BF16)

    oxt = [attend(sc) for sc in scores]
    proj = [_dot_tn(o, wo_ref[...]) for o in oxt]
    for rows, h, p in zip(subs, h1, proj):
        out_ref[0, rows, :] = h + _rms_rows(p, gxpost_ref[...])


def _mix_xattn(x, yconv, yatt_t, g_att, w_out_a, w_out_b, g_post, g_xpre, w_qt, kx, vxt, w_o, g_xpost):
    b, s, _ = x.shape
    m = kx.shape[1]
    tm = min(TM_MIX, s)
    return pl.pallas_call(
        _mix_xattn_kernel,
        grid=(b, s // tm),
        in_specs=[
            pl.BlockSpec((1, tm, D_MODEL), lambda i, j: (i, j, 0)),
            pl.BlockSpec((1, tm, CONV_WIDTH), lambda i, j: (i, j, 0)),
            pl.BlockSpec((1, ATT_WIDTH, tm), lambda i, j: (i, 0, j)),
            _whole((ATT_WIDTH, 1)),
            _whole((CONV_WIDTH, D_MODEL)), _whole((ATT_WIDTH, D_MODEL)),
            _whole((1, D_MODEL)), _whole((1, D_MODEL)),
            _whole((D_MODEL, D_MODEL)),
            pl.BlockSpec((1, m, D_MODEL), lambda i, j: (i, 0, 0)),
            pl.BlockSpec((1, D_MODEL, m), lambda i, j: (i, 0, 0)),
            _whole((D_MODEL, D_MODEL)), _whole((1, D_MODEL)),
        ],
        out_specs=pl.BlockSpec((1, tm, D_MODEL), lambda i, j: (i, j, 0)),
        out_shape=jax.ShapeDtypeStruct((b, s, D_MODEL), F32),
        compiler_params=_params("arbitrary", "arbitrary"),
        name="mix_xattn",
    )(x, yconv, yatt_t, g_att, w_out_a, w_out_b, g_post, g_xpre, w_qt, kx, vxt, w_o, g_xpost)


def _ffn_kernel(h_ref, gpre_ref, wgu_ref, wdown_ref, gpost_ref, out_ref):
    subs = [slice(c0, c0 + TM_FFN_SUB) for c0 in range(0, h_ref.shape[0], TM_FFN_SUB)]
    hn = [_rms_rows(h_ref[rows, :], gpre_ref[...]).astype(BF16) for rows in subs]
    gu = [_dot(v, wgu_ref[...]) for v in hn]
    act = [(v[:, 0:D_FF] * jax.nn.sigmoid(v[:, 0:D_FF]) * v[:, D_FF:2 * D_FF]).astype(BF16) for v in gu]
    down = [_dot(a, wdown_ref[...]) for a in act]
    for rows, d in zip(subs, down):
        out_ref[rows, :] = h_ref[rows, :] + _rms_rows(d, gpost_ref[...])


def _ffn(h, g_pre, w_gu, w_down, g_post):
    n, _ = h.shape
    tm = min(TM_FFN, n)
    return pl.pallas_call(
        _ffn_kernel,
        grid=(n // tm,),
        in_specs=[pl.BlockSpec((tm, D_MODEL), lambda i: (i, 0)), _whole((1, D_MODEL)),
                  _whole((D_MODEL, 2 * D_FF)), _whole((D_FF, D_MODEL)), _whole((1, D_MODEL))],
        out_specs=pl.BlockSpec((tm, D_MODEL), lambda i: (i, 0)),
        out_shape=jax.ShapeDtypeStruct((n, D_MODEL), F32),
        compiler_params=_params("arbitrary"),
        name="ffn",
    )(h, g_pre, w_gu, w_down, g_post)


def _slc_map_t(s):
    n_slab = s // CMP_STRIDE
    ci = np.arange(n_slab)[None, :] * CMP_STRIDE
    sj = np.arange(s // SLC_BLOCK)[:, None] * SLC_BLOCK
    ov = np.clip(np.minimum(ci + CMP_BLOCK, sj + SLC_BLOCK) - np.maximum(ci, sj), 0, None)
    ov[:, n_slab - 1] = 0
    return jnp.asarray(ov / CMP_BLOCK, dtype=F32)


def _tile_biases(tq):
    key = np.arange(TK)[:, None]
    qry = np.arange(tq)[None, :]
    none = np.zeros((TK, tq), np.float32)
    causal = np.where(key <= qry, 0.0, NEG).astype(np.float32)
    window_start = np.where(key > qry, 0.0, NEG).astype(np.float32)
    masked = np.full((TK, tq), NEG, np.float32)
    return jnp.asarray(np.stack([none, causal, window_start, masked]), dtype=BF16)


def kernel(x, mem, positions, norm_mix_pre, w_in, conv_w, pe_kc, w1_kc, w2_kc, pe_vc, w1_vc, w2_vc,
           norm_conv_out, norm_attn_out, w_out, norm_mix_post, norm_x_pre, norm_mem, w_q_x, w_kv_x,
           w_o_x, norm_x_post, norm_ffn_pre, w_gate_up, w_down, norm_ffn_post):
    b, s, _ = x.shape
    row = lambda v: v.reshape(1, -1).astype(F32)

    cuts = np.cumsum([0, CONV_WIDTH, CONV_WIDTH, CONV_WIDTH, ATT_WIDTH] + [KV_WIDTH] * 6 + [3 * ATT_HEADS])
    col = lambda k: w_in[:, cuts[k]:cuts[k + 1]]
    w_row = jnp.concatenate([col(0), col(1), col(2), col(4), col(5)], axis=1).astype(BF16)
    gate_cols = np.zeros((KV_GROUPS * GATE_ROWS,), np.int32)
    gate_live = np.zeros((KV_GROUPS * GATE_ROWS,), np.float32)
    for g in range(KV_GROUPS):
        for c in range(3):
            for r in range(Q_PER_KV):
                gate_cols[g * GATE_ROWS + c * Q_PER_KV + r] = (g * Q_PER_KV + r) * 3 + c
                gate_live[g * GATE_ROWS + c * Q_PER_KV + r] = 1.0
    w_gate = col(10)[:, gate_cols] * gate_live[None, :]
    w_t = jnp.concatenate([col(3), col(8), col(6), col(9), col(7), w_gate], axis=1).T.astype(BF16)

    inv = ROPE_THETA ** (-jnp.arange(HALF, dtype=F32) / HALF)
    yconv, kvc, qt, kaug, vaug, gt = _in_proj(
        x, positions.reshape(b, 1, s), inv.reshape(HALF, 1), row(norm_mix_pre), w_row, w_t,
        conv_w.astype(F32), row(norm_conv_out))

    n_slab = s // CMP_STRIDE
    pos_c = positions[:, np.minimum(np.arange(n_slab) * CMP_STRIDE + CMP_BLOCK - 1, s - 1)]
    kcmp, vcmpt = _compress(
        kvc, pos_c.reshape(b, n_slab, 1), inv.reshape(1, HALF),
        pe_kc.reshape(1, -1), w1_kc.astype(BF16), w2_kc.astype(BF16),
        pe_vc.reshape(1, -1), w1_vc.astype(BF16), w2_vc.T.astype(BF16))

    yatt_t = _nsa(qt, kcmp, vcmpt, kaug, vaug, gt, _slc_map_t(s), _tile_biases(min(TQ, s)))

    kx, vxt = _mem_kv(mem, row(norm_mem), w_kv_x[:, :D_MODEL].astype(BF16), w_kv_x[:, D_MODEL:].T.astype(BF16))
    h2 = _mix_xattn(
        x, yconv, yatt_t, norm_attn_out.reshape(-1, 1).astype(F32),
        w_out[:CONV_WIDTH].astype(BF16), w_out[CONV_WIDTH:].astype(BF16),
        row(norm_mix_post), row(norm_x_pre), w_q_x.T.astype(BF16), kx, vxt, w_o_x.astype(BF16), row(norm_x_post))

    out = _ffn(h2.reshape(b * s, D_MODEL), row(norm_ffn_pre), w_gate_up.astype(BF16), w_down.astype(BF16),
               row(norm_ffn_post))
    return out.reshape(b, s, D_MODEL)
```

```python
import math

import numpy as np
import jax
import jax.numpy as jnp
from jax import lax
from jax.experimental import pallas as pl
from jax.experimental.pallas import tpu as pltpu

D_MODEL = 1024
CONV_WIDTH = 512
CONV_K = 3
ATT_HEADS = 8
HEAD_DIM = 64
HALF = HEAD_DIM // 2
ATT_WIDTH = ATT_HEADS * HEAD_DIM
KV_GROUPS = 2
Q_PER_KV = ATT_HEADS // KV_GROUPS
KV_WIDTH = KV_GROUPS * HEAD_DIM
CMP_BLOCK = 32
CMP_STRIDE = 16
CMP_HIDDEN = 256
SLC_BLOCK = 64
N_SELECT = 16
WINDOW = 512
ROPE_THETA = 10000.0
X_HEADS = 4
X_HEAD_DIM = D_MODEL // X_HEADS
D_FF = 2816
EPS = 1e-6
FORCE = 1e4
NEG = -1e30
GATE_ROWS = 16
K_AUG = 2 * HEAD_DIM
V_AUG = HEAD_DIM + 16
WIN, SEL = 0, 1
_RANK_STEP = 16
_SLOT_WIN, _N_SLOTS = 2, 5
_BIAS_NONE, _BIAS_CAUSAL, _BIAS_WINDOW_START, _BIAS_MASKED = 0, 1, 2, 3

TM_PROJ = 1024
TM_MIX = 1024
TM_FFN = 512
TM_FFN_SUB = 256
SUB_ROWS = 512
TQ = 256
TQ_SELECT = 512
TK = 256
VMEM_LIMIT = 56 * 1024 * 1024

F32 = jnp.float32
BF16 = jnp.bfloat16

_NT = (((1,), (1,)), ((), ()))
_TN = (((0,), (0,)), ((), ()))


def _dot(a, b):
    return jnp.dot(a, b, preferred_element_type=F32)


def _dot_nt(a, b):
    return lax.dot_general(a, b, _NT, preferred_element_type=F32)


def _dot_tn(a, b):
    return lax.dot_general(a, b, _TN, preferred_element_type=F32)


def _rms_rows(x, g):
    return x * lax.rsqrt(jnp.mean(x * x, axis=-1, keepdims=True) + EPS) * g


def _params(*sem):
    return pltpu.CompilerParams(dimension_semantics=sem, vmem_limit_bytes=VMEM_LIMIT)


def _whole(shape):
    nd = len(shape)
    return pl.BlockSpec(shape, lambda *_: (0,) * nd, pipeline_mode=pl.Buffered(1))


def _in_proj_kernel(x_ref, pos_ref, inv_ref, g_ref, wrow_ref, wt_ref, convw_ref, gconv_ref,
                    yconv_ref, kvc_ref, qt_ref, kaug_ref, vaug_ref, gt_ref, carry_ref):
    @pl.when(pl.program_id(1) == 0)
    def _():
        carry_ref[...] = jnp.zeros_like(carry_ref)

    tm = x_ref.shape[1]
    subs = [slice(c0, c0 + SUB_ROWS) for c0 in range(0, tm, SUB_ROWS)]
    u = [_rms_rows(x_ref[0, rows, :], g_ref[...]).astype(BF16) for rows in subs]
    prow = [_dot(v, wrow_ref[...]) for v in u]
    pt = [_dot_nt(wt_ref[...], v) for v in u]
    for rows, pr, ptr in zip(subs, prow, pt):
        _in_proj_epilogue(rows, pl.program_id(1) * tm + rows.start, pr, ptr, pos_ref, inv_ref, convw_ref, gconv_ref,
                          yconv_ref, kvc_ref, qt_ref, kaug_ref, vaug_ref, gt_ref, carry_ref)


def _in_proj_epilogue(rows, tok0, prow, pt, pos_ref, inv_ref, convw_ref, gconv_ref,
                      yconv_ref, kvc_ref, qt_ref, kaug_ref, vaug_ref, gt_ref, carry_ref):
    n = rows.stop - rows.start

    bg = prow[:, 0:CONV_WIDTH]
    z = prow[:, CONV_WIDTH:2 * CONV_WIDTH] * prow[:, 2 * CONV_WIDTH:3 * CONV_WIDTH]
    prev = carry_ref[...]
    row = lax.broadcasted_iota(jnp.int32, (n, CONV_WIDTH), 0)
    z1 = jnp.where(row == 0, prev[7:8, :], pltpu.roll(z, 1, axis=0))
    z2 = jnp.where(row == 0, prev[6:7, :], jnp.where(row == 1, prev[7:8, :], pltpu.roll(z, 2, axis=0)))
    carry_ref[...] = z[n - 8:n, :]
    cw = convw_ref[...]
    yc = bg * (cw[0:1, :] * z2 + cw[1:2, :] * z1 + cw[2:3, :] * z)
    yconv_ref[0, rows, :] = _rms_rows(yc, gconv_ref[...]).astype(yconv_ref.dtype)

    for c in range(2 * KV_GROUPS):
        lo = 3 * CONV_WIDTH + c * HEAD_DIM
        kvc_ref[0, c, rows, :] = prow[:, lo:lo + HEAD_DIM]

    ang = pos_ref[0, :, rows].astype(F32) * inv_ref[...]
    cos, sin = jnp.cos(ang), jnp.sin(ang)

    def rope_t(base):
        x1 = pt[base:base + HALF, :]
        x2 = pt[base + HALF:base + HEAD_DIM, :]
        return x1 * cos - x2 * sin, x2 * cos + x1 * sin

    scale = HEAD_DIM ** -0.5 * math.log2(math.e)
    for h in range(ATT_HEADS):
        r1, r2 = rope_t(h * HEAD_DIM)
        qt_ref[0, h * HEAD_DIM:h * HEAD_DIM + HALF, rows] = (r1 * scale).astype(qt_ref.dtype)
        qt_ref[0, h * HEAD_DIM + HALF:(h + 1) * HEAD_DIM, rows] = (r2 * scale).astype(qt_ref.dtype)

    tok = tok0 + lax.broadcasted_iota(jnp.int32, (n, HEAD_DIM), 0)
    blk_col = lax.broadcasted_iota(jnp.int32, (n, HEAD_DIM), 1)
    onehot = jnp.where(jnp.right_shift(tok, SLC_BLOCK.bit_length() - 1) == blk_col, 1.0, 0.0)
    for br in (WIN, SEL):
        base = ATT_WIDTH + br * KV_WIDTH
        parts = []
        for g in range(KV_GROUPS):
            parts.extend(rope_t(base + g * HEAD_DIM))
        k_rows = jnp.concatenate(parts, axis=0).T
        extra = onehot if br == SEL else jnp.zeros_like(onehot)
        for g in range(KV_GROUPS):
            k_g = k_rows[:, g * HEAD_DIM:(g + 1) * HEAD_DIM]
            kaug_ref[0, g, br, rows, :] = jnp.concatenate([k_g, extra], axis=-1).astype(kaug_ref.dtype)

    v0 = ATT_WIDTH + 2 * KV_WIDTH
    ones_rows = jnp.where(lax.broadcasted_iota(jnp.int32, (V_AUG - HEAD_DIM, n), 0) == 0, 1.0, 0.0)
    for br in (WIN, SEL):
        for g in range(KV_GROUPS):
            lo = v0 + br * KV_WIDTH + g * HEAD_DIM
            vaug_ref[0, g, br, :, rows] = jnp.concatenate([pt[lo:lo + HEAD_DIM, :], ones_rows], axis=0).astype(vaug_ref.dtype)

    g0 = v0 + 2 * KV_WIDTH
    gt_ref[0, :, rows] = jax.nn.sigmoid(pt[g0:g0 + KV_GROUPS * GATE_ROWS, :])


def _in_proj(x, pos_row, inv_col, g_pre, w_row, w_t, conv_w, g_conv):
    b, s, _ = x.shape
    tm = min(TM_PROJ, s)
    n_row, n_t = w_row.shape[1], w_t.shape[0]
    out_shape = (
        jax.ShapeDtypeStruct((b, s, CONV_WIDTH), BF16),
        jax.ShapeDtypeStruct((b, 2 * KV_GROUPS, s, HEAD_DIM), F32),
        jax.ShapeDtypeStruct((b, ATT_WIDTH, s), BF16),
        jax.ShapeDtypeStruct((b, KV_GROUPS, 2, s, K_AUG), BF16),
        jax.ShapeDtypeStruct((b, KV_GROUPS, 2, V_AUG, s), BF16),
        jax.ShapeDtypeStruct((b, KV_GROUPS * GATE_ROWS, s), F32),
    )
    return pl.pallas_call(
        _in_proj_kernel,
        grid=(b, s // tm),
        in_specs=[
            pl.BlockSpec((1, tm, D_MODEL), lambda i, j: (i, j, 0)),
            pl.BlockSpec((1, 1, tm), lambda i, j: (i, 0, j)),
            _whole((HALF, 1)),
            _whole((1, D_MODEL)),
            _whole((D_MODEL, n_row)),
            _whole((n_t, D_MODEL)),
            _whole((CONV_K, CONV_WIDTH)),
            _whole((1, CONV_WIDTH)),
        ],
        out_specs=(
            pl.BlockSpec((1, tm, CONV_WIDTH), lambda i, j: (i, j, 0)),
            pl.BlockSpec((1, 2 * KV_GROUPS, tm, HEAD_DIM), lambda i, j: (i, 0, j, 0)),
            pl.BlockSpec((1, ATT_WIDTH, tm), lambda i, j: (i, 0, j)),
            pl.BlockSpec((1, KV_GROUPS, 2, tm, K_AUG), lambda i, j: (i, 0, 0, j, 0)),
            pl.BlockSpec((1, KV_GROUPS, 2, V_AUG, tm), lambda i, j: (i, 0, 0, 0, j)),
            pl.BlockSpec((1, KV_GROUPS * GATE_ROWS, tm), lambda i, j: (i, 0, j)),
        ),
        out_shape=out_shape,
        scratch_shapes=[pltpu.VMEM((8, CONV_WIDTH), F32)],
        compiler_params=_params("arbitrary", "arbitrary"),
        name="in_proj",
    )(x, pos_row, inv_col, g_pre, w_row, w_t, conv_w, g_conv)


def _compress_kernel(rk_ref, rv_ref, pos_ref, inv_ref, pek_ref, w1k_ref, w2k_ref, pev_ref, w1v_ref, w2vt_ref,
                     kc_ref, vct_ref):
    half_in = CMP_STRIDE * HEAD_DIM
    n_slab = kc_ref.shape[2]

    def hidden(x_ref, pe_ref, w1_ref):
        a = jnp.zeros((n_slab, CMP_HIDDEN), F32)
        b = jnp.zeros((n_slab, CMP_HIDDEN), F32)
        per_dot = 256 // HEAD_DIM
        for j0 in range(0, CMP_STRIDE, per_dot):
            xcat = jnp.concatenate([x_ref[0, 0, pl.ds(j, n_slab, stride=CMP_STRIDE), :]
                                    for j in range(j0, j0 + per_dot)], axis=-1).astype(BF16)
            lo = j0 * HEAD_DIM
            a = a + _dot(xcat, w1_ref[lo:lo + per_dot * HEAD_DIM, :])
            b = b + _dot(xcat, w1_ref[half_in + lo:half_in + lo + per_dot * HEAD_DIM, :])
        pe = jnp.broadcast_to(pe_ref[...], (8, 2 * half_in)).astype(BF16)
        h = a + pltpu.roll(b, n_slab - 1, axis=0) + _dot(pe, w1_ref[...])[0:1, :]
        return (h * jax.nn.sigmoid(h)).astype(BF16)

    kc = _dot(hidden(rk_ref, pek_ref, w1k_ref), w2k_ref[...])
    ang = pos_ref[0].astype(F32) * inv_ref[...]
    cos, sin = jnp.cos(ang), jnp.sin(ang)
    x1, x2 = kc[:, 0:HALF], kc[:, HALF:HEAD_DIM]
    kc_ref[0, 0] = jnp.concatenate([x1 * cos - x2 * sin, x2 * cos + x1 * sin], axis=-1).astype(kc_ref.dtype)
    vct_ref[0, 0] = _dot_nt(w2vt_ref[...], hidden(rv_ref, pev_ref, w1v_ref)).astype(vct_ref.dtype)


def _compress(kvc, pos_c, inv_row, pek, w1k, w2k, pev, w1v, w2vt):
    b, _, s, _ = kvc.shape
    g, n_slab = KV_GROUPS, s // CMP_STRIDE
    return pl.pallas_call(
        _compress_kernel,
        grid=(b, g),
        in_specs=[
            pl.BlockSpec((1, 1, s, HEAD_DIM), lambda i, j: (i, j, 0, 0)),
            pl.BlockSpec((1, 1, s, HEAD_DIM), lambda i, j: (i, KV_GROUPS + j, 0, 0)),
            pl.BlockSpec((1, n_slab, 1), lambda i, j: (i, 0, 0)),
            _whole((1, HALF)),
            _whole(pek.shape), _whole(w1k.shape), _whole(w2k.shape),
            _whole(pev.shape), _whole(w1v.shape), _whole(w2vt.shape),
        ],
        out_specs=(
            pl.BlockSpec((1, 1, n_slab, HEAD_DIM), lambda i, j: (i, j, 0, 0)),
            pl.BlockSpec((1, 1, HEAD_DIM, n_slab), lambda i, j: (i, j, 0, 0)),
        ),
        out_shape=(
            jax.ShapeDtypeStruct((b, g, n_slab, HEAD_DIM), BF16),
            jax.ShapeDtypeStruct((b, g, HEAD_DIM, n_slab), BF16),
        ),
        compiler_params=_params("arbitrary", "arbitrary"),
        name="compress",
    )(kvc, kvc, pos_c, inv_row, pek, w1k, w2k, pev, w1v, w2vt)


def _select_kernel(qt_ref, kc_ref, vct_ref, gt_ref, map_ref, ocmp_ref, sel_ref, cs_ref, imp_ref):
    n_cmp = kc_ref.shape[2]
    n_slc = map_ref.shape[0]
    step = pl.program_id(1)
    n_sub = qt_ref.shape[2] // TQ
    subs = [slice(c * TQ, (c + 1) * TQ) for c in range(n_sub)]
    row_of = lambda h: slice(h * HEAD_DIM, (h + 1) * HEAD_DIM)
    heads_of = lambda g: range(g * Q_PER_KV, (g + 1) * Q_PER_KV)
    t_of = [(step * n_sub + c) * TQ + lax.broadcasted_iota(jnp.int32, (1, TQ), 1) for c in range(n_sub)]

    def gate_row(h):
        row = (h // Q_PER_KV) * GATE_ROWS + h % Q_PER_KV
        return gt_ref[0, row:row + 1, :]

    for c, lanes in enumerate(subs):
        for g in range(KV_GROUPS):
            kc = kc_ref[0, g]
            for h in heads_of(g):
                cs_ref[c, h] = _dot(kc, qt_ref[0, row_of(h), lanes])

    cmp_end = lax.broadcasted_iota(jnp.int32, (n_cmp, TQ), 0) * CMP_STRIDE + (CMP_BLOCK - 1)
    psum = []
    for c, lanes in enumerate(subs):
        cbias = jnp.where(cmp_end <= t_of[c], 0.0, NEG)
        has_cmp = t_of[c] >= CMP_BLOCK - 1
        for g in range(KV_GROUPS):
            vct = vct_ref[0, g]
            acc = jnp.zeros((n_cmp, TQ), F32)
            for h in heads_of(g):
                s = cs_ref[c, h] + cbias
                e = jnp.exp2(s - jnp.max(s, axis=0, keepdims=True))
                inv = jnp.where(has_cmp, 1.0 / jnp.maximum(jnp.sum(e, axis=0, keepdims=True), 1e-30), 0.0)
                acc = acc + e * inv
                ocmp_ref[0, row_of(h), lanes] = (gate_row(h)[:, lanes] * inv) * _dot(vct, e.astype(BF16))
            psum.append(acc)

    blk = lax.broadcasted_iota(jnp.int32, (n_slc, TQ), 0)
    for c in range(n_sub):
        cur = jnp.right_shift(t_of[c], SLC_BLOCK.bit_length() - 1)
        forced = (blk == 0) | (blk == cur) | (blk == cur - 1)
        for g in range(KV_GROUPS):
            imp = jnp.dot(map_ref[...], psum[c * KV_GROUPS + g], preferred_element_type=F32,
                          precision=lax.Precision.HIGHEST)
            imp_ref[c, g] = jnp.where(forced, FORCE, jnp.where(blk * SLC_BLOCK > t_of[c], -FORCE, imp))

    def select_blocks(n_live, c, g):
        n_grp = n_live // 8
        imp_g = [imp_ref[c, g, 8 * a:8 * a + 8, :] for a in range(n_grp)]
        rank_g = [jnp.zeros((8, TQ), jnp.int32) for _ in range(n_grp)]
        sub = lax.broadcasted_iota(jnp.int32, (8, TQ), 0)
        for j in range(n_live):
            vj = jnp.broadcast_to(imp_ref[c, g, j:j + 1, :], (8, TQ))
            for a in range(n_grp):
                if 8 * a > j:
                    beats = jnp.where(vj >= imp_g[a], 1, 0)
                elif 8 * a + 7 < j:
                    beats = jnp.where(vj > imp_g[a], 1, 0)
                else:
                    beats = jnp.where(sub > j - 8 * a, jnp.where(vj >= imp_g[a], 1, 0), jnp.where(vj > imp_g[a], 1, 0))
                rank_g[a] = rank_g[a] + beats
        sel = jnp.where(jnp.concatenate(rank_g, axis=0) < min(N_SELECT, n_slc), 0.0, NEG).astype(BF16)
        dead = K_AUG - HEAD_DIM - n_live
        if dead:
            sel = jnp.concatenate([sel, jnp.full((dead, TQ), NEG, BF16)], axis=0)
        sel_ref[0, g, :, subs[c]] = sel

    def select_all(n_live):
        for c in range(n_sub):
            for g in range(KV_GROUPS):
                select_blocks(n_live, c, g)

    blocks_per_step = qt_ref.shape[2] // SLC_BLOCK
    live_sizes = list(range(_RANK_STEP, n_slc, _RANK_STEP)) + [n_slc]
    for v, n_live in enumerate(live_sizes):
        lo = v * _RANK_STEP // blocks_per_step
        hi = n_live // blocks_per_step if v + 1 < len(live_sizes) else n_slc // blocks_per_step
        pl.when((step >= lo) & (step < hi))(lambda n_live=n_live: select_all(n_live))


def _select(qt, kcmp, vcmpt, gt, slc_map_t):
    b, _, s = qt.shape
    tqs = min(TQ_SELECT, s)
    n_cmp = kcmp.shape[2]
    n_slc = s // SLC_BLOCK
    assert tqs % TQ == 0 and s % tqs == 0 and n_slc % 8 == 0 and n_slc <= K_AUG - HEAD_DIM
    assert _RANK_STEP % 8 == 0 and _RANK_STEP % (tqs // SLC_BLOCK) == 0 and n_slc % (tqs // SLC_BLOCK) == 0
    per_batch = lambda shape: pl.BlockSpec((1,) + shape, lambda i, k: (i,) + (0,) * len(shape))
    return pl.pallas_call(
        _select_kernel,
        grid=(b, s // tqs),
        in_specs=[
            pl.BlockSpec((1, ATT_WIDTH, tqs), lambda i, k: (i, 0, k)),
            per_batch((KV_GROUPS, n_cmp, HEAD_DIM)),
            per_batch((KV_GROUPS, HEAD_DIM, n_cmp)),
            pl.BlockSpec((1, KV_GROUPS * GATE_ROWS, tqs), lambda i, k: (i, 0, k)),
            _whole((n_slc, n_cmp)),
        ],
        out_specs=(
            pl.BlockSpec((1, ATT_WIDTH, tqs), lambda i, k: (i, 0, k)),
            pl.BlockSpec((1, KV_GROUPS, K_AUG - HEAD_DIM, tqs), lambda i, k: (i, 0, 0, k)),
        ),
        out_shape=(
            jax.ShapeDtypeStruct((b, ATT_WIDTH, s), F32),
            jax.ShapeDtypeStruct((b, KV_GROUPS, K_AUG - HEAD_DIM, s), BF16),
        ),
        scratch_shapes=[
            pltpu.VMEM((tqs // TQ, ATT_HEADS, n_cmp, TQ), F32),
            pltpu.VMEM((tqs // TQ, KV_GROUPS, n_slc, TQ), F32),
        ],
        compiler_params=_params("arbitrary", "arbitrary"),
        name="nsa_select",
    )(qt, kcmp, vcmpt, gt, slc_map_t)


def _nsa_kernel(qt_ref, ocmp_ref, sel_ref, kaug_ref, vaug_ref, gt_ref, tri_ref,
                out_ref, rhs_ref, s_ref, m_ref, acc_ref):
    tq = qt_ref.shape[2]
    i = pl.program_id(1)
    row_of = lambda h: slice(h * HEAD_DIM, (h + 1) * HEAD_DIM)
    heads_of = lambda g: range(g * Q_PER_KV, (g + 1) * Q_PER_KV)

    def issue_scores(slot, br, tile):
        for g in range(KV_GROUPS):
            k_t = kaug_ref[0, g, br, pl.ds(pl.multiple_of(tile * TK, TK), TK), :]
            for h in heads_of(g):
                s_ref[slot, h] = _dot(k_t, rhs_ref[br, h]).astype(s_ref.dtype)

    def softmax_step(slot, br, tile, bias_idx=None, first=False):
        for g in range(KV_GROUPS):
            v_t = vaug_ref[0, g, br, :, pl.ds(pl.multiple_of(tile * TK, TK), TK)]
            for h in heads_of(g):
                s = s_ref[slot, h]
                if bias_idx is not None:
                    s = s + tri_ref[bias_idx]
                if first:
                    m_new = jnp.max(s, axis=0, keepdims=True).astype(F32)
                    acc_ref[h] = _dot(v_t, jnp.exp2(s - m_new.astype(s.dtype)))
                else:
                    m_old = m_ref[h]
                    m_new = jnp.maximum(m_old, jnp.max(s, axis=0, keepdims=True).astype(F32))
                    p = jnp.exp2(s - m_new.astype(s.dtype))
                    acc_ref[h] = jnp.exp2(m_old - m_new) * acc_ref[h] + _dot(v_t, p)
                m_ref[h] = m_new

    def gate_row(branch, h):
        row = (h // Q_PER_KV) * GATE_ROWS + branch * Q_PER_KV + h % Q_PER_KV
        return gt_ref[0, row:row + 1, :]

    def flush_state(branch):
        for h in range(ATT_HEADS):
            acc = acc_ref[h]
            o = acc[0:HEAD_DIM, :] * (1.0 / jnp.maximum(acc[HEAD_DIM:HEAD_DIM + 1, :], 1e-30))
            out_ref[0, row_of(h), :] = out_ref[0, row_of(h), :] + gate_row(branch, h) * o

    zeros_half = jnp.zeros((K_AUG - HEAD_DIM, tq), BF16)
    for h in range(ATT_HEADS):
        rhs_ref[WIN, h, 0:HEAD_DIM, :] = qt_ref[0, row_of(h), :]
        rhs_ref[WIN, h, HEAD_DIM:K_AUG, :] = zeros_half
        rhs_ref[SEL, h, 0:HEAD_DIM, :] = qt_ref[0, row_of(h), :]
        rhs_ref[SEL, h, HEAD_DIM:K_AUG, :] = sel_ref[0, h // Q_PER_KV]
    n_win = WINDOW // TK + 1
    win_tiles = [i - (n_win - 1) + k for k in range(n_win)]
    for k in range(n_win):
        issue_scores(_SLOT_WIN + k, WIN, jnp.maximum(win_tiles[k], 0))
    issue_scores(0, SEL, 0)
    for h in range(ATT_HEADS):
        out_ref[0, row_of(h), :] = ocmp_ref[0, row_of(h), :]

    def window_step(k):
        exists = win_tiles[k] >= 0
        bias_idx = _BIAS_CAUSAL if k == n_win - 1 else jnp.where(exists, _BIAS_WINDOW_START if k == 0 else _BIAS_NONE, _BIAS_MASKED)
        softmax_step(_SLOT_WIN + k, WIN, jnp.maximum(win_tiles[k], 0), bias_idx, first=(k == 0))

    for k in range(n_win):
        window_step(k)
    flush_state(2)

    for h in range(ATT_HEADS):
        m_ref[h] = jnp.full((1, tq), NEG, F32)
        acc_ref[h] = jnp.zeros((V_AUG, tq), F32)

    def pair_steps(a):
        issue_scores(1, SEL, a + 1)
        softmax_step(0, SEL, a)
        issue_scores(0, SEL, a + 2)
        softmax_step(1, SEL, a + 1)

    def body(quad, carry):
        pair_steps(4 * quad)
        pair_steps(4 * quad + 2)
        return carry

    n_plain = jnp.right_shift(i, 1)
    n_quad = jnp.right_shift(n_plain, 1)
    lax.fori_loop(0, n_quad, body, 0)

    @pl.when(n_plain != 2 * n_quad)
    def _():
        pair_steps(4 * n_quad)

    @pl.when(i == 2 * n_plain)
    def _():
        softmax_step(0, SEL, i, _BIAS_CAUSAL)

    @pl.when(i != 2 * n_plain)
    def _():
        issue_scores(1, SEL, i)
        softmax_step(0, SEL, i - 1)
        softmax_step(1, SEL, i, _BIAS_CAUSAL)

    flush_state(1)


def _nsa(qt, ocmp, sel, kaug, vaug, gt, tri):
    b, _, s = qt.shape
    tq = min(TQ, s)
    assert tq == TK and s % tq == 0 and WINDOW % TK == 0 and _SLOT_WIN + WINDOW // TK + 1 <= _N_SLOTS
    per_batch = lambda shape: pl.BlockSpec((1,) + shape, lambda i, k: (i,) + (0,) * len(shape))
    return pl.pallas_call(
        _nsa_kernel,
        grid=(b, s // tq),
        in_specs=[
            pl.BlockSpec((1, ATT_WIDTH, tq), lambda i, k: (i, 0, k)),
            pl.BlockSpec((1, ATT_WIDTH, tq), lambda i, k: (i, 0, k)),
            pl.BlockSpec((1, KV_GROUPS, K_AUG - HEAD_DIM, tq), lambda i, k: (i, 0, 0, k)),
            per_batch((KV_GROUPS, 2, s, K_AUG)),
            per_batch((KV_GROUPS, 2, V_AUG, s)),
            pl.BlockSpec((1, KV_GROUPS * GATE_ROWS, tq), lambda i, k: (i, 0, k)),
            _whole((4, TK, tq)),
        ],
        out_specs=pl.BlockSpec((1, ATT_WIDTH, tq), lambda i, k: (i, 0, k)),
        out_shape=jax.ShapeDtypeStruct((b, ATT_WIDTH, s), F32),
        scratch_shapes=[
            pltpu.VMEM((2, ATT_HEADS, K_AUG, tq), BF16),
            pltpu.VMEM((_N_SLOTS, ATT_HEADS, TK, tq), BF16),
            pltpu.VMEM((ATT_HEADS, 1, tq), F32),
            pltpu.VMEM((ATT_HEADS, V_AUG, tq), F32),
        ],
        compiler_params=_params("arbitrary", "arbitrary"),
        name="nsa_attention",
    )(qt, ocmp, sel, kaug, vaug, gt, tri)


def _mem_kv_kernel(mem_ref, g_ref, wk_ref, wvt_ref, kx_ref, vxt_ref):
    mn = _rms_rows(mem_ref[0], g_ref[...]).astype(BF16)
    kx_ref[0] = _dot(mn, wk_ref[...]).astype(kx_ref.dtype)
    vxt_ref[0] = _dot_nt(wvt_ref[...], mn).astype(vxt_ref.dtype)


def _mem_kv(mem, g_mem, w_k, w_vt):
    b, m, _ = mem.shape
    return pl.pallas_call(
        _mem_kv_kernel,
        grid=(b,),
        in_specs=[pl.BlockSpec((1, m, D_MODEL), lambda i: (i, 0, 0)), _whole((1, D_MODEL)),
                  _whole((D_MODEL, D_MODEL)), _whole((D_MODEL, D_MODEL))],
        out_specs=(pl.BlockSpec((1, m, D_MODEL), lambda i: (i, 0, 0)),
                   pl.BlockSpec((1, D_MODEL, m), lambda i: (i, 0, 0))),
        out_shape=(jax.ShapeDtypeStruct((b, m, D_MODEL), BF16),
                   jax.ShapeDtypeStruct((b, D_MODEL, m), BF16)),
        compiler_params=_params("arbitrary"),
        name="mem_kv",
    )(mem, g_mem, w_k, w_vt)


def _mix_xattn_kernel(x_ref, yconv_ref, yatt_ref, gatt_ref, woa_ref, wob_ref, gpost_ref, gxpre_ref,
                      wqt_ref, kx_ref, vxt_ref, wo_ref, gxpost_ref, out_ref):
    subs = [slice(c0, c0 + SUB_ROWS) for c0 in range(0, x_ref.shape[1], SUB_ROWS)]
    head = lambda h: slice(h * X_HEAD_DIM, (h + 1) * X_HEAD_DIM)

    def yatt_normed(rows):
        ya = yatt_ref[0, :, rows]
        return (ya * lax.rsqrt(jnp.mean(ya * ya, axis=0, keepdims=True) + EPS) * gatt_ref[...]).astype(BF16)

    yan = [yatt_normed(rows) for rows in subs]
    mixed = [_dot(yconv_ref[0, rows, :], woa_ref[...]) + _dot_tn(y, wob_ref[...]) for rows, y in zip(subs, yan)]
    h1 = [x_ref[0, rows, :] + _rms_rows(m, gpost_ref[...]) for rows, m in zip(subs, mixed)]
    hn = [_rms_rows(h, gxpre_ref[...]).astype(BF16) for h in h1]
    qxt = [(_dot_nt(wqt_ref[...], h) * (X_HEAD_DIM ** -0.5 * math.log2(math.e))).astype(BF16) for h in hn]
    scores = [[_dot(kx_ref[0, :, head(h)], q[head(h), :]) for h in range(X_HEADS)] for q in qxt]

    def attend(sc):
        heads = []
        for h in range(X_HEADS):
            e = jnp.exp2(sc[h] - jnp.max(sc[h], axis=0, keepdims=True))
            o = _dot(vxt_ref[0, head(h), :], e.astype(BF16))
            heads.append(o * (1.0 / jnp.sum(e, axis=0, keepdims=True)))
        return jnp.concatenate(heads, axis=0).astype(BF16)

    oxt = [attend(sc) for sc in scores]
    proj = [_dot_tn(o, wo_ref[...]) for o in oxt]
    for rows, h, p in zip(subs, h1, proj):
        out_ref[0, rows, :] = h + _rms_rows(p, gxpost_ref[...])


def _mix_xattn(x, yconv, yatt_t, g_att, w_out_a, w_out_b, g_post, g_xpre, w_qt, kx, vxt, w_o, g_xpost):
    b, s, _ = x.shape
    m = kx.shape[1]
    tm = min(TM_MIX, s)
    return pl.pallas_call(
        _mix_xattn_kernel,
        grid=(b, s // tm),
        in_specs=[
            pl.BlockSpec((1, tm, D_MODEL), lambda i, j: (i, j, 0)),
            pl.BlockSpec((1, tm, CONV_WIDTH), lambda i, j: (i, j, 0)),
            pl.BlockSpec((1, ATT_WIDTH, tm), lambda i, j: (i, 0, j)),
            _whole((ATT_WIDTH, 1)),
            _whole((CONV_WIDTH, D_MODEL)), _whole((ATT_WIDTH, D_MODEL)),
            _whole((1, D_MODEL)), _whole((1, D_MODEL)),
            _whole((D_MODEL, D_MODEL)),
            pl.BlockSpec((1, m, D_MODEL), lambda i, j: (i, 0, 0)),
            pl.BlockSpec((1, D_MODEL, m), lambda i, j: (i, 0, 0)),
            _whole((D_MODEL, D_MODEL)), _whole((1, D_MODEL)),
        ],
        out_specs=pl.BlockSpec((1, tm, D_MODEL), lambda i, j: (i, j, 0)),
        out_shape=jax.ShapeDtypeStruct((b, s, D_MODEL), F32),
        compiler_params=_params("arbitrary", "arbitrary"),
        name="mix_xattn",
    )(x, yconv, yatt_t, g_att, w_out_a, w_out_b, g_post, g_xpre, w_qt, kx, vxt, w_o, g_xpost)


def _ffn_kernel(h_ref, gpre_ref, wgu_ref, wdown_ref, gpost_ref, out_ref):
    subs = [slice(c0, c0 + TM_FFN_SUB) for c0 in range(0, h_ref.shape[0], TM_FFN_SUB)]
    hn = [_rms_rows(h_ref[rows, :], gpre_ref[...]).astype(BF16) for rows in subs]
    gu = [_dot(v, wgu_ref[...]) for v in hn]
    act = [(v[:, 0:D_FF] * jax.nn.sigmoid(v[:, 0:D_FF]) * v[:, D_FF:2 * D_FF]).astype(BF16) for v in gu]
    down = [_dot(a, wdown_ref[...]) for a in act]
    for rows, d in zip(subs, down):
        out_ref[rows, :] = h_ref[rows, :] + _rms_rows(d, gpost_ref[...])


def _ffn(h, g_pre, w_gu, w_down, g_post):
    n, _ = h.shape
    tm = min(TM_FFN, n)
    return pl.pallas_call(
        _ffn_kernel,
        grid=(n // tm,),
        in_specs=[pl.BlockSpec((tm, D_MODEL), lambda i: (i, 0)), _whole((1, D_MODEL)),
                  _whole((D_MODEL, 2 * D_FF)), _whole((D_FF, D_MODEL)), _whole((1, D_MODEL))],
        out_specs=pl.BlockSpec((tm, D_MODEL), lambda i: (i, 0)),
        out_shape=jax.ShapeDtypeStruct((n, D_MODEL), F32),
        compiler_params=_params("arbitrary"),
        name="ffn",
    )(h, g_pre, w_gu, w_down, g_post)


def _slc_map_t(s):
    n_slab = s // CMP_STRIDE
    ci = np.arange(n_slab)[None, :] * CMP_STRIDE
    sj = np.arange(s // SLC_BLOCK)[:, None] * SLC_BLOCK
    ov = np.clip(np.minimum(ci + CMP_BLOCK, sj + SLC_BLOCK) - np.maximum(ci, sj), 0, None)
    ov[:, n_slab - 1] = 0
    return jnp.asarray(ov / CMP_BLOCK, dtype=F32)


def _tile_biases(tq):
    key = np.arange(TK)[:, None]
    qry = np.arange(tq)[None, :]
    none = np.zeros((TK, tq), np.float32)
    causal = np.where(key <= qry, 0.0, NEG).astype(np.float32)
    window_start = np.where(key > qry, 0.0, NEG).astype(np.float32)
    masked = np.full((TK, tq), NEG, np.float32)
    return jnp.asarray(np.stack([none, causal, window_start, masked]), dtype=BF16)


def kernel(x, mem, positions, norm_mix_pre, w_in, conv_w, pe_kc, w1_kc, w2_kc, pe_vc, w1_vc, w2_vc,
           norm_conv_out, norm_attn_out, w_out, norm_mix_post, norm_x_pre, norm_mem, w_q_x, w_kv_x,
           w_o_x, norm_x_post, norm_ffn_pre, w_gate_up, w_down, norm_ffn_post):
    b, s, _ = x.shape
    row = lambda v: v.reshape(1, -1).astype(F32)

    cuts = np.cumsum([0, CONV_WIDTH, CONV_WIDTH, CONV_WIDTH, ATT_WIDTH] + [KV_WIDTH] * 6 + [3 * ATT_HEADS])
    col = lambda k: w_in[:, cuts[k]:cuts[k + 1]]
    w_row = jnp.concatenate([col(0), col(1), col(2), col(4), col(5)], axis=1).astype(BF16)
    gate_cols = np.zeros((KV_GROUPS * GATE_ROWS,), np.int32)
    gate_live = np.zeros((KV_GROUPS * GATE_ROWS,), np.float32)
    for g in range(KV_GROUPS):
        for c in range(3):
            for r in range(Q_PER_KV):
                gate_cols[g * GATE_ROWS + c * Q_PER_KV + r] = (g * Q_PER_KV + r) * 3 + c
                gate_live[g * GATE_ROWS + c * Q_PER_KV + r] = 1.0
    w_gate = col(10)[:, gate_cols] * gate_live[None, :]
    w_t = jnp.concatenate([col(3), col(8), col(6), col(9), col(7), w_gate], axis=1).T.astype(BF16)

    inv = ROPE_THETA ** (-jnp.arange(HALF, dtype=F32) / HALF)
    yconv, kvc, qt, kaug, vaug, gt = _in_proj(
        x, positions.reshape(b, 1, s), inv.reshape(HALF, 1), row(norm_mix_pre), w_row, w_t,
        conv_w.astype(F32), row(norm_conv_out))

    n_slab = s // CMP_STRIDE
    pos_c = positions[:, np.minimum(np.arange(n_slab) * CMP_STRIDE + CMP_BLOCK - 1, s - 1)]
    kcmp, vcmpt = _compress(
        kvc, pos_c.reshape(b, n_slab, 1), inv.reshape(1, HALF),
        pe_kc.reshape(1, -1), w1_kc.astype(BF16), w2_kc.astype(BF16),
        pe_vc.reshape(1, -1), w1_vc.astype(BF16), w2_vc.T.astype(BF16))

    ocmp, sel = _select(qt, kcmp, vcmpt, gt, _slc_map_t(s))
    yatt_t = _nsa(qt, ocmp, sel, kaug, vaug, gt, _tile_biases(min(TQ, s)))

    kx, vxt = _mem_kv(mem, row(norm_mem), w_kv_x[:, :D_MODEL].astype(BF16), w_kv_x[:, D_MODEL:].T.astype(BF16))
    h2 = _mix_xattn(
        x, yconv, yatt_t, norm_attn_out.reshape(-1, 1).astype(F32),
        w_out[:CONV_WIDTH].astype(BF16), w_out[CONV_WIDTH:].astype(BF16),
        row(norm_mix_post), row(norm_x_pre), w_q_x.T.astype(BF16), kx, vxt, w_o_x.astype(BF16), row(norm_x_post))

    out = _ffn(h2.reshape(b * s, D_MODEL), row(norm_ffn_pre), w_gate_up.astype(BF16), w_down.astype(BF16),
               row(norm_ffn_post))
    return out.reshape(b, s, D_MODEL)
```

```python
import math

import numpy as np
import jax
import jax.numpy as jnp
from jax import lax
from jax.experimental import pallas as pl
from jax.experimental.pallas import tpu as pltpu

D_MODEL = 1024
CONV_WIDTH = 512
CONV_K = 3
ATT_HEADS = 8
HEAD_DIM = 64
HALF = HEAD_DIM // 2
ATT_WIDTH = ATT_HEADS * HEAD_DIM
KV_GROUPS = 2
Q_PER_KV = ATT_HEADS // KV_GROUPS
KV_WIDTH = KV_GROUPS * HEAD_DIM
CMP_BLOCK = 32
CMP_STRIDE = 16
CMP_HIDDEN = 256
SLC_BLOCK = 64
N_SELECT = 16
WINDOW = 512
ROPE_THETA = 10000.0
X_HEADS = 4
X_HEAD_DIM = D_MODEL // X_HEADS
D_FF = 2816
EPS = 1e-6
FORCE = 1e4
NEG = -1e30
GATE_ROWS = 16
K_AUG = 2 * HEAD_DIM
V_AUG = HEAD_DIM + 16
WIN, SEL = 0, 1
_RANK_STEP = 16
_SLOT_WIN, _N_SLOTS = 2, 5
_BIAS_NONE, _BIAS_CAUSAL, _BIAS_WINDOW_START, _BIAS_MASKED = 0, 1, 2, 3

TM_PROJ = 1024
TM_MIX = 1024
TM_FFN = 512
TM_FFN_SUB = 256
SUB_ROWS = 512
TQ = 256
TQ_SELECT = 512
TK = 256
VMEM_LIMIT = 56 * 1024 * 1024

F32 = jnp.float32
BF16 = jnp.bfloat16

_NT = (((1,), (1,)), ((), ()))
_TN = (((0,), (0,)), ((), ()))


def _dot(a, b):
    return jnp.dot(a, b, preferred_element_type=F32)


def _dot_nt(a, b):
    return lax.dot_general(a, b, _NT, preferred_element_type=F32)


def _dot_tn(a, b):
    return lax.dot_general(a, b, _TN, preferred_element_type=F32)


def _rms_rows(x, g):
    return x * lax.rsqrt(jnp.mean(x * x, axis=-1, keepdims=True) + EPS) * g


def _params(*sem):
    return pltpu.CompilerParams(dimension_semantics=sem, vmem_limit_bytes=VMEM_LIMIT)


def _whole(shape):
    nd = len(shape)
    return pl.BlockSpec(shape, lambda *_: (0,) * nd, pipeline_mode=pl.Buffered(1))


def _in_proj_kernel(x_ref, pos_ref, inv_ref, g_ref, wrow_ref, wt_ref, convw_ref, gconv_ref,
                    yconv_ref, kvc_ref, qt_ref, kaug_ref, vaug_ref, gt_ref, carry_ref):
    @pl.when(pl.program_id(1) == 0)
    def _():
        carry_ref[...] = jnp.zeros_like(carry_ref)

    tm = x_ref.shape[1]
    subs = [slice(c0, c0 + SUB_ROWS) for c0 in range(0, tm, SUB_ROWS)]
    u = [_rms_rows(x_ref[0, rows, :], g_ref[...]).astype(BF16) for rows in subs]
    prow = [_dot(v, wrow_ref[...]) for v in u]
    pt = [_dot_nt(wt_ref[...], v) for v in u]
    for rows, pr, ptr in zip(subs, prow, pt):
        _in_proj_epilogue(rows, pl.program_id(1) * tm + rows.start, pr, ptr, pos_ref, inv_ref, convw_ref, gconv_ref,
                          yconv_ref, kvc_ref, qt_ref, kaug_ref, vaug_ref, gt_ref, carry_ref)


def _in_proj_epilogue(rows, tok0, prow, pt, pos_ref, inv_ref, convw_ref, gconv_ref,
                      yconv_ref, kvc_ref, qt_ref, kaug_ref, vaug_ref, gt_ref, carry_ref):
    n = rows.stop - rows.start

    bg = prow[:, 0:CONV_WIDTH]
    z = prow[:, CONV_WIDTH:2 * CONV_WIDTH] * prow[:, 2 * CONV_WIDTH:3 * CONV_WIDTH]
    prev = carry_ref[...]
    row = lax.broadcasted_iota(jnp.int32, (n, CONV_WIDTH), 0)
    z1 = jnp.where(row == 0, prev[7:8, :], pltpu.roll(z, 1, axis=0))
    z2 = jnp.where(row == 0, prev[6:7, :], jnp.where(row == 1, prev[7:8, :], pltpu.roll(z, 2, axis=0)))
    carry_ref[...] = z[n - 8:n, :]
    cw = convw_ref[...]
    yc = bg * (cw[0:1, :] * z2 + cw[1:2, :] * z1 + cw[2:3, :] * z)
    yconv_ref[0, rows, :] = _rms_rows(yc, gconv_ref[...]).astype(yconv_ref.dtype)

    for c in range(2 * KV_GROUPS):
        lo = 3 * CONV_WIDTH + c * HEAD_DIM
        kvc_ref[0, c, rows, :] = prow[:, lo:lo + HEAD_DIM]

    ang = pos_ref[0, :, rows].astype(F32) * inv_ref[...]
    cos, sin = jnp.cos(ang), jnp.sin(ang)

    def rope_t(base):
        x1 = pt[base:base + HALF, :]
        x2 = pt[base + HALF:base + HEAD_DIM, :]
        return x1 * cos - x2 * sin, x2 * cos + x1 * sin

    scale = HEAD_DIM ** -0.5 * math.log2(math.e)
    for h in range(ATT_HEADS):
        r1, r2 = rope_t(h * HEAD_DIM)
        qt_ref[0, h * HEAD_DIM:h * HEAD_DIM + HALF, rows] = (r1 * scale).astype(qt_ref.dtype)
        qt_ref[0, h * HEAD_DIM + HALF:(h + 1) * HEAD_DIM, rows] = (r2 * scale).astype(qt_ref.dtype)

    tok = tok0 + lax.broadcasted_iota(jnp.int32, (n, HEAD_DIM), 0)
    blk_col = lax.broadcasted_iota(jnp.int32, (n, HEAD_DIM), 1)
    onehot = jnp.where(jnp.right_shift(tok, SLC_BLOCK.bit_length() - 1) == blk_col, 1.0, 0.0)
    for br in (WIN, SEL):
        base = ATT_WIDTH + br * KV_WIDTH
        parts = []
        for g in range(KV_GROUPS):
            parts.extend(rope_t(base + g * HEAD_DIM))
        k_rows = jnp.concatenate(parts, axis=0).T
        extra = onehot if br == SEL else jnp.zeros_like(onehot)
        for g in range(KV_GROUPS):
            k_g = k_rows[:, g * HEAD_DIM:(g + 1) * HEAD_DIM]
            kaug_ref[0, g, br, rows, :] = jnp.concatenate([k_g, extra], axis=-1).astype(kaug_ref.dtype)

    v0 = ATT_WIDTH + 2 * KV_WIDTH
    ones_rows = jnp.where(lax.broadcasted_iota(jnp.int32, (V_AUG - HEAD_DIM, n), 0) == 0, 1.0, 0.0)
    for br in (WIN, SEL):
        for g in range(KV_GROUPS):
            lo = v0 + br * KV_WIDTH + g * HEAD_DIM
            vaug_ref[0, g, br, :, rows] = jnp.concatenate([pt[lo:lo + HEAD_DIM, :], ones_rows], axis=0).astype(vaug_ref.dtype)

    g0 = v0 + 2 * KV_WIDTH
    gt_ref[0, :, rows] = jax.nn.sigmoid(pt[g0:g0 + KV_GROUPS * GATE_ROWS, :])


def _in_proj(x, pos_row, inv_col, g_pre, w_row, w_t, conv_w, g_conv):
    b, s, _ = x.shape
    tm = min(TM_PROJ, s)
    n_row, n_t = w_row.shape[1], w_t.shape[0]
    out_shape = (
        jax.ShapeDtypeStruct((b, s, CONV_WIDTH), BF16),
        jax.ShapeDtypeStruct((b, 2 * KV_GROUPS, s, HEAD_DIM), F32),
        jax.ShapeDtypeStruct((b, ATT_WIDTH, s), BF16),
        jax.ShapeDtypeStruct((b, KV_GROUPS, 2, s, K_AUG), BF16),
        jax.ShapeDtypeStruct((b, KV_GROUPS, 2, V_AUG, s), BF16),
        jax.ShapeDtypeStruct((b, KV_GROUPS * GATE_ROWS, s), F32),
    )
    return pl.pallas_call(
        _in_proj_kernel,
        grid=(b, s // tm),
        in_specs=[
            pl.BlockSpec((1, tm, D_MODEL), lambda i, j: (i, j, 0)),
            pl.BlockSpec((1, 1, tm), lambda i, j: (i, 0, j)),
            _whole((HALF, 1)),
            _whole((1, D_MODEL)),
            _whole((D_MODEL, n_row)),
            _whole((n_t, D_MODEL)),
            _whole((CONV_K, CONV_WIDTH)),
            _whole((1, CONV_WIDTH)),
        ],
        out_specs=(
            pl.BlockSpec((1, tm, CONV_WIDTH), lambda i, j: (i, j, 0)),
            pl.BlockSpec((1, 2 * KV_GROUPS, tm, HEAD_DIM), lambda i, j: (i, 0, j, 0)),
            pl.BlockSpec((1, ATT_WIDTH, tm), lambda i, j: (i, 0, j)),
            pl.BlockSpec((1, KV_GROUPS, 2, tm, K_AUG), lambda i, j: (i, 0, 0, j, 0)),
            pl.BlockSpec((1, KV_GROUPS, 2, V_AUG, tm), lambda i, j: (i, 0, 0, 0, j)),
            pl.BlockSpec((1, KV_GROUPS * GATE_ROWS, tm), lambda i, j: (i, 0, j)),
        ),
        out_shape=out_shape,
        scratch_shapes=[pltpu.VMEM((8, CONV_WIDTH), F32)],
        compiler_params=_params("arbitrary", "arbitrary"),
        name="in_proj",
    )(x, pos_row, inv_col, g_pre, w_row, w_t, conv_w, g_conv)


def _compress_kernel(rk_ref, rv_ref, pos_ref, inv_ref, pek_ref, w1k_ref, w2k_ref, pev_ref, w1v_ref, w2vt_ref,
                     kc_ref, vct_ref):
    half_in = CMP_STRIDE * HEAD_DIM
    n_slab = kc_ref.shape[2]

    def hidden(x_ref, pe_ref, w1_ref):
        a = jnp.zeros((n_slab, CMP_HIDDEN), F32)
        b = jnp.zeros((n_slab, CMP_HIDDEN), F32)
        per_dot = 256 // HEAD_DIM
        for j0 in range(0, CMP_STRIDE, per_dot):
            xcat = jnp.concatenate([x_ref[0, 0, pl.ds(j, n_slab, stride=CMP_STRIDE), :]
                                    for j in range(j0, j0 + per_dot)], axis=-1).astype(BF16)
            lo = j0 * HEAD_DIM
            a = a + _dot(xcat, w1_ref[lo:lo + per_dot * HEAD_DIM, :])
            b = b + _dot(xcat, w1_ref[half_in + lo:half_in + lo + per_dot * HEAD_DIM, :])
        pe = jnp.broadcast_to(pe_ref[...], (8, 2 * half_in)).astype(BF16)
        h = a + pltpu.roll(b, n_slab - 1, axis=0) + _dot(pe, w1_ref[...])[0:1, :]
        return (h * jax.nn.sigmoid(h)).astype(BF16)

    kc = _dot(hidden(rk_ref, pek_ref, w1k_ref), w2k_ref[...])
    ang = pos_ref[0].astype(F32) * inv_ref[...]
    cos, sin = jnp.cos(ang), jnp.sin(ang)
    x1, x2 = kc[:, 0:HALF], kc[:, HALF:HEAD_DIM]
    kc_ref[0, 0] = jnp.concatenate([x1 * cos - x2 * sin, x2 * cos + x1 * sin], axis=-1).astype(kc_ref.dtype)
    vct_ref[0, 0] = _dot_nt(w2vt_ref[...], hidden(rv_ref, pev_ref, w1v_ref)).astype(vct_ref.dtype)


def _compress(kvc, pos_c, inv_row, pek, w1k, w2k, pev, w1v, w2vt):
    b, _, s, _ = kvc.shape
    g, n_slab = KV_GROUPS, s // CMP_STRIDE
    return pl.pallas_call(
        _compress_kernel,
        grid=(b, g),
        in_specs=[
            pl.BlockSpec((1, 1, s, HEAD_DIM), lambda i, j: (i, j, 0, 0)),
            pl.BlockSpec((1, 1, s, HEAD_DIM), lambda i, j: (i, KV_GROUPS + j, 0, 0)),
            pl.BlockSpec((1, n_slab, 1), lambda i, j: (i, 0, 0)),
            _whole((1, HALF)),
            _whole(pek.shape), _whole(w1k.shape), _whole(w2k.shape),
            _whole(pev.shape), _whole(w1v.shape), _whole(w2vt.shape),
        ],
        out_specs=(
            pl.BlockSpec((1, 1, n_slab, HEAD_DIM), lambda i, j: (i, j, 0, 0)),
            pl.BlockSpec((1, 1, HEAD_DIM, n_slab), lambda i, j: (i, j, 0, 0)),
        ),
        out_shape=(
            jax.ShapeDtypeStruct((b, g, n_slab, HEAD_DIM), BF16),
            jax.ShapeDtypeStruct((b, g, HEAD_DIM, n_slab), BF16),
        ),
        compiler_params=_params("arbitrary", "arbitrary"),
        name="compress",
    )(kvc, kvc, pos_c, inv_row, pek, w1k, w2k, pev, w1v, w2vt)


def _select_kernel(qt_ref, kc_ref, vct_ref, gt_ref, map_ref, ocmp_ref, sel_ref, cs_ref, imp_ref):
    n_cmp = kc_ref.shape[2]
    n_slc = map_ref.shape[0]
    step = pl.program_id(1)
    n_sub = qt_ref.shape[2] // TQ
    subs = [slice(c * TQ, (c + 1) * TQ) for c in range(n_sub)]
    row_of = lambda h: slice(h * HEAD_DIM, (h + 1) * HEAD_DIM)
    heads_of = lambda g: range(g * Q_PER_KV, (g + 1) * Q_PER_KV)
    t_of = [(step * n_sub + c) * TQ + lax.broadcasted_iota(jnp.int32, (1, TQ), 1) for c in range(n_sub)]

    def gate_row(h):
        row = (h // Q_PER_KV) * GATE_ROWS + h % Q_PER_KV
        return gt_ref[0, row:row + 1, :]

    def compressed_branch(n_live):
        rows = min(n_live * (SLC_BLOCK // CMP_STRIDE), n_cmp)
        for c, lanes in enumerate(subs):
            for g in range(KV_GROUPS):
                kc = kc_ref[0, g, 0:rows, :]
                for h in heads_of(g):
                    cs_ref[c, h, 0:rows, :] = _dot(kc, qt_ref[0, row_of(h), lanes])

        cmp_end = lax.broadcasted_iota(jnp.int32, (rows, TQ), 0) * CMP_STRIDE + (CMP_BLOCK - 1)
        psum = []
        for c, lanes in enumerate(subs):
            cbias = jnp.where(cmp_end <= t_of[c], 0.0, NEG)
            has_cmp = t_of[c] >= CMP_BLOCK - 1
            for g in range(KV_GROUPS):
                vct = vct_ref[0, g, :, 0:rows]
                acc = jnp.zeros((rows, TQ), F32)
                for h in heads_of(g):
                    s = cs_ref[c, h, 0:rows, :] + cbias
                    e = jnp.exp2(s - jnp.max(s, axis=0, keepdims=True))
                    inv = jnp.where(has_cmp, 1.0 / jnp.maximum(jnp.sum(e, axis=0, keepdims=True), 1e-30), 0.0)
                    acc = acc + e * inv
                    ocmp_ref[0, row_of(h), lanes] = (gate_row(h)[:, lanes] * inv) * _dot(vct, e.astype(BF16))
                psum.append(acc)

        blk = lax.broadcasted_iota(jnp.int32, (n_live, TQ), 0)
        for c in range(n_sub):
            cur = jnp.right_shift(t_of[c], SLC_BLOCK.bit_length() - 1)
            forced = (blk == 0) | (blk == cur) | (blk == cur - 1)
            for g in range(KV_GROUPS):
                imp = jnp.dot(map_ref[0:n_live, 0:rows], psum[c * KV_GROUPS + g], preferred_element_type=F32,
                              precision=lax.Precision.HIGHEST)
                imp_ref[c, g, 0:n_live, :] = jnp.where(forced, FORCE, jnp.where(blk * SLC_BLOCK > t_of[c], -FORCE, imp))

    def select_blocks(n_live, c, g):
        n_grp = n_live // 8
        imp_g = [imp_ref[c, g, 8 * a:8 * a + 8, :] for a in range(n_grp)]
        rank_g = [jnp.zeros((8, TQ), jnp.int32) for _ in range(n_grp)]
        sub = lax.broadcasted_iota(jnp.int32, (8, TQ), 0)
        for j in range(n_live):
            vj = jnp.broadcast_to(imp_ref[c, g, j:j + 1, :], (8, TQ))
            for a in range(n_grp):
                if 8 * a > j:
                    beats = jnp.where(vj >= imp_g[a], 1, 0)
                elif 8 * a + 7 < j:
                    beats = jnp.where(vj > imp_g[a], 1, 0)
                else:
                    beats = jnp.where(sub > j - 8 * a, jnp.where(vj >= imp_g[a], 1, 0), jnp.where(vj > imp_g[a], 1, 0))
                rank_g[a] = rank_g[a] + beats
        sel = jnp.where(jnp.concatenate(rank_g, axis=0) < min(N_SELECT, n_slc), 0.0, NEG).astype(BF16)
        dead = K_AUG - HEAD_DIM - n_live
        if dead:
            sel = jnp.concatenate([sel, jnp.full((dead, TQ), NEG, BF16)], axis=0)
        sel_ref[0, g, :, subs[c]] = sel

    def select_all(n_live):
        compressed_branch(n_live)
        for c in range(n_sub):
            for g in range(KV_GROUPS):
                select_blocks(n_live, c, g)

    blocks_per_step = qt_ref.shape[2] // SLC_BLOCK
    live_sizes = list(range(_RANK_STEP, n_slc, _RANK_STEP)) + [n_slc]
    for v, n_live in enumerate(live_sizes):
        lo = v * _RANK_STEP // blocks_per_step
        hi = n_live // blocks_per_step if v + 1 < len(live_sizes) else n_slc // blocks_per_step
        pl.when((step >= lo) & (step < hi))(lambda n_live=n_live: select_all(n_live))


def _select(qt, kcmp, vcmpt, gt, slc_map_t):
    b, _, s = qt.shape
    tqs = min(TQ_SELECT, s)
    n_cmp = kcmp.shape[2]
    n_slc = s // SLC_BLOCK
    assert tqs % TQ == 0 and s % tqs == 0 and n_slc % 8 == 0 and n_slc <= K_AUG - HEAD_DIM
    assert _RANK_STEP % 8 == 0 and _RANK_STEP % (tqs // SLC_BLOCK) == 0 and n_slc % (tqs // SLC_BLOCK) == 0
    per_batch = lambda shape: pl.BlockSpec((1,) + shape, lambda i, k: (i,) + (0,) * len(shape))
    return pl.pallas_call(
        _select_kernel,
        grid=(b, s // tqs),
        in_specs=[
            pl.BlockSpec((1, ATT_WIDTH, tqs), lambda i, k: (i, 0, k)),
            per_batch((KV_GROUPS, n_cmp, HEAD_DIM)),
            per_batch((KV_GROUPS, HEAD_DIM, n_cmp)),
            pl.BlockSpec((1, KV_GROUPS * GATE_ROWS, tqs), lambda i, k: (i, 0, k)),
            _whole((n_slc, n_cmp)),
        ],
        out_specs=(
            pl.BlockSpec((1, ATT_WIDTH, tqs), lambda i, k: (i, 0, k)),
            pl.BlockSpec((1, KV_GROUPS, K_AUG - HEAD_DIM, tqs), lambda i, k: (i, 0, 0, k)),
        ),
        out_shape=(
            jax.ShapeDtypeStruct((b, ATT_WIDTH, s), F32),
            jax.ShapeDtypeStruct((b, KV_GROUPS, K_AUG - HEAD_DIM, s), BF16),
        ),
        scratch_shapes=[
            pltpu.VMEM((tqs // TQ, ATT_HEADS, n_cmp, TQ), F32),
            pltpu.VMEM((tqs // TQ, KV_GROUPS, n_slc, TQ), F32),
        ],
        compiler_params=_params("arbitrary", "arbitrary"),
        name="nsa_select",
    )(qt, kcmp, vcmpt, gt, slc_map_t)


def _nsa_kernel(qt_ref, ocmp_ref, sel_ref, kaug_ref, vaug_ref, gt_ref, tri_ref,
                out_ref, rhs_ref, s_ref, m_ref, acc_ref):
    tq = qt_ref.shape[2]
    i = pl.program_id(1)
    row_of = lambda h: slice(h * HEAD_DIM, (h + 1) * HEAD_DIM)
    heads_of = lambda g: range(g * Q_PER_KV, (g + 1) * Q_PER_KV)

    def issue_scores(slot, br, tile):
        for g in range(KV_GROUPS):
            k_t = kaug_ref[0, g, br, pl.ds(pl.multiple_of(tile * TK, TK), TK), :]
            for h in heads_of(g):
                s_ref[slot, h] = _dot(k_t, rhs_ref[br, h]).astype(s_ref.dtype)

    def softmax_step(slot, br, tile, bias_idx=None, first=False):
        for g in range(KV_GROUPS):
            v_t = vaug_ref[0, g, br, :, pl.ds(pl.multiple_of(tile * TK, TK), TK)]
            for h in heads_of(g):
                s = s_ref[slot, h]
                if bias_idx is not None:
                    s = s + tri_ref[bias_idx]
                if first:
                    m_new = jnp.max(s, axis=0, keepdims=True).astype(F32)
                    acc_ref[h] = _dot(v_t, jnp.exp2(s - m_new.astype(s.dtype)))
                else:
                    m_old = m_ref[h]
                    m_new = jnp.maximum(m_old, jnp.max(s, axis=0, keepdims=True).astype(F32))
                    p = jnp.exp2(s - m_new.astype(s.dtype))
                    acc_ref[h] = jnp.exp2(m_old - m_new) * acc_ref[h] + _dot(v_t, p)
                m_ref[h] = m_new

    def gate_row(branch, h):
        row = (h // Q_PER_KV) * GATE_ROWS + branch * Q_PER_KV + h % Q_PER_KV
        return gt_ref[0, row:row + 1, :]

    def flush_state(branch):
        for h in range(ATT_HEADS):
            acc = acc_ref[h]
            o = acc[0:HEAD_DIM, :] * (1.0 / jnp.maximum(acc[HEAD_DIM:HEAD_DIM + 1, :], 1e-30))
            out_ref[0, row_of(h), :] = out_ref[0, row_of(h), :] + gate_row(branch, h) * o

    zeros_half = jnp.zeros((K_AUG - HEAD_DIM, tq), BF16)
    for h in range(ATT_HEADS):
        rhs_ref[WIN, h, 0:HEAD_DIM, :] = qt_ref[0, row_of(h), :]
        rhs_ref[WIN, h, HEAD_DIM:K_AUG, :] = zeros_half
        rhs_ref[SEL, h, 0:HEAD_DIM, :] = qt_ref[0, row_of(h), :]
        rhs_ref[SEL, h, HEAD_DIM:K_AUG, :] = sel_ref[0, h // Q_PER_KV]
    n_win = WINDOW // TK + 1
    win_tiles = [i - (n_win - 1) + k for k in range(n_win)]
    for k in range(n_win):
        issue_scores(_SLOT_WIN + k, WIN, jnp.maximum(win_tiles[k], 0))
    issue_scores(0, SEL, 0)
    for h in range(ATT_HEADS):
        out_ref[0, row_of(h), :] = ocmp_ref[0, row_of(h), :]

    def window_step(k):
        exists = win_tiles[k] >= 0
        bias_idx = _BIAS_CAUSAL if k == n_win - 1 else jnp.where(exists, _BIAS_WINDOW_START if k == 0 else _BIAS_NONE, _BIAS_MASKED)
        softmax_step(_SLOT_WIN + k, WIN, jnp.maximum(win_tiles[k], 0), bias_idx, first=(k == 0))

    for k in range(n_win):
        window_step(k)
    flush_state(2)

    for h in range(ATT_HEADS):
        m_ref[h] = jnp.full((1, tq), NEG, F32)
        acc_ref[h] = jnp.zeros((V_AUG, tq), F32)

    def pair_steps(a):
        issue_scores(1, SEL, a + 1)
        softmax_step(0, SEL, a)
        issue_scores(0, SEL, a + 2)
        softmax_step(1, SEL, a + 1)

    def body(quad, carry):
        pair_steps(4 * quad)
        pair_steps(4 * quad + 2)
        return carry

    n_plain = jnp.right_shift(i, 1)
    n_quad = jnp.right_shift(n_plain, 1)
    lax.fori_loop(0, n_quad, body, 0)

    @pl.when(n_plain != 2 * n_quad)
    def _():
        pair_steps(4 * n_quad)

    @pl.when(i == 2 * n_plain)
    def _():
        softmax_step(0, SEL, i, _BIAS_CAUSAL)

    @pl.when(i != 2 * n_plain)
    def _():
        issue_scores(1, SEL, i)
        softmax_step(0, SEL, i - 1)
        softmax_step(1, SEL, i, _BIAS_CAUSAL)

    flush_state(1)


def _nsa(qt, ocmp, sel, kaug, vaug, gt, tri):
    b, _, s = qt.shape
    tq = min(TQ, s)
    assert tq == TK and s % tq == 0 and WINDOW % TK == 0 and _SLOT_WIN + WINDOW // TK + 1 <= _N_SLOTS
    per_batch = lambda shape: pl.BlockSpec((1,) + shape, lambda i, k: (i,) + (0,) * len(shape))
    return pl.pallas_call(
        _nsa_kernel,
        grid=(b, s // tq),
        in_specs=[
            pl.BlockSpec((1, ATT_WIDTH, tq), lambda i, k: (i, 0, k)),
            pl.BlockSpec((1, ATT_WIDTH, tq), lambda i, k: (i, 0, k)),
            pl.BlockSpec((1, KV_GROUPS, K_AUG - HEAD_DIM, tq), lambda i, k: (i, 0, 0, k)),
            per_batch((KV_GROUPS, 2, s, K_AUG)),
            per_batch((KV_GROUPS, 2, V_AUG, s)),
            pl.BlockSpec((1, KV_GROUPS * GATE_ROWS, tq), lambda i, k: (i, 0, k)),
            _whole((4, TK, tq)),
        ],
        out_specs=pl.BlockSpec((1, ATT_WIDTH, tq), lambda i, k: (i, 0, k)),
        out_shape=jax.ShapeDtypeStruct((b, ATT_WIDTH, s), F32),
        scratch_shapes=[
            pltpu.VMEM((2, ATT_HEADS, K_AUG, tq), BF16),
            pltpu.VMEM((_N_SLOTS, ATT_HEADS, TK, tq), BF16),
            pltpu.VMEM((ATT_HEADS, 1, tq), F32),
            pltpu.VMEM((ATT_HEADS, V_AUG, tq), F32),
        ],
        compiler_params=_params("arbitrary", "arbitrary"),
        name="nsa_attention",
    )(qt, ocmp, sel, kaug, vaug, gt, tri)


def _mem_kv_kernel(mem_ref, g_ref, wk_ref, wvt_ref, kx_ref, vxt_ref):
    mn = _rms_rows(mem_ref[0], g_ref[...]).astype(BF16)
    kx_ref[0] = _dot(mn, wk_ref[...]).astype(kx_ref.dtype)
    vxt_ref[0] = _dot_nt(wvt_ref[...], mn).astype(vxt_ref.dtype)


def _mem_kv(mem, g_mem, w_k, w_vt):
    b, m, _ = mem.shape
    return pl.pallas_call(
        _mem_kv_kernel,
        grid=(b,),
        in_specs=[pl.BlockSpec((1, m, D_MODEL), lambda i: (i, 0, 0)), _whole((1, D_MODEL)),
                  _whole((D_MODEL, D_MODEL)), _whole((D_MODEL, D_MODEL))],
        out_specs=(pl.BlockSpec((1, m, D_MODEL), lambda i: (i, 0, 0)),
                   pl.BlockSpec((1, D_MODEL, m), lambda i: (i, 0, 0))),
        out_shape=(jax.ShapeDtypeStruct((b, m, D_MODEL), BF16),
                   jax.ShapeDtypeStruct((b, D_MODEL, m), BF16)),
        compiler_params=_params("arbitrary"),
        name="mem_kv",
    )(mem, g_mem, w_k, w_vt)


def _mix_xattn_kernel(x_ref, yconv_ref, yatt_ref, gatt_ref, woa_ref, wob_ref, gpost_ref, gxpre_ref,
                      wqt_ref, kx_ref, vxt_ref, wo_ref, gxpost_ref, out_ref):
    subs = [slice(c0, c0 + SUB_ROWS) for c0 in range(0, x_ref.shape[1], SUB_ROWS)]
    head = lambda h: slice(h * X_HEAD_DIM, (h + 1) * X_HEAD_DIM)

    def yatt_normed(rows):
        ya = yatt_ref[0, :, rows]
        return (ya * lax.rsqrt(jnp.mean(ya * ya, axis=0, keepdims=True) + EPS) * gatt_ref[...]).astype(BF16)

    yan = [yatt_normed(rows) for rows in subs]
    mixed = [_dot(yconv_ref[0, rows, :], woa_ref[...]) + _dot_tn(y, wob_ref[...]) for rows, y in zip(subs, yan)]
    h1 = [x_ref[0, rows, :] + _rms_rows(m, gpost_ref[...]) for rows, m in zip(subs, mixed)]
    hn = [_rms_rows(h, gxpre_ref[...]).astype(BF16) for h in h1]
    qxt = [(_dot_nt(wqt_ref[...], h) * (X_HEAD_DIM ** -0.5 * math.log2(math.e))).astype(BF16) for h in hn]
    scores = [[_dot(kx_ref[0, :, head(h)], q[head(h), :]) for h in range(X_HEADS)] for q in qxt]

    def attend(sc):
        heads = []
        for h in range(X_HEADS):
            e = jnp.exp2(sc[h] - jnp.max(sc[h], axis=0, keepdims=True))
            o = _dot(vxt_ref[0, head(h), :], e.astype(BF16))
            heads.append(o * (1.0 / jnp.sum(e, axis=0, keepdims=True)))
        return jnp.concatenate(heads, axis=0).astype(BF16)

    oxt = [attend(sc) for sc in scores]
    proj = [_dot_tn(o, wo_ref[...]) for o in oxt]
    for rows, h, p in zip(subs, h1, proj):
        out_ref[0, rows, :] = h + _rms_rows(p, gxpost_ref[...])


def _mix_xattn(x, yconv, yatt_t, g_att, w_out_a, w_out_b, g_post, g_xpre, w_qt, kx, vxt, w_o, g_xpost):
    b, s, _ = x.shape
    m = kx.shape[1]
    tm = min(TM_MIX, s)
    return pl.pallas_call(
        _mix_xattn_kernel,
        grid=(b, s // tm),
        in_specs=[
            pl.BlockSpec((1, tm, D_MODEL), lambda i, j: (i, j, 0)),
            pl.BlockSpec((1, tm, CONV_WIDTH), lambda i, j: (i, j, 0)),
            pl.BlockSpec((1, ATT_WIDTH, tm), lambda i, j: (i, 0, j)),
            _whole((ATT_WIDTH, 1)),
            _whole((CONV_WIDTH, D_MODEL)), _whole((ATT_WIDTH, D_MODEL)),
            _whole((1, D_MODEL)), _whole((1, D_MODEL)),
            _whole((D_MODEL, D_MODEL)),
            pl.BlockSpec((1, m, D_MODEL), lambda i, j: (i, 0, 0)),
            pl.BlockSpec((1, D_MODEL, m), lambda i, j: (i, 0, 0)),
            _whole((D_MODEL, D_MODEL)), _whole((1, D_MODEL)),
        ],
        out_specs=pl.BlockSpec((1, tm, D_MODEL), lambda i, j: (i, j, 0)),
        out_shape=jax.ShapeDtypeStruct((b, s, D_MODEL), F32),
        compiler_params=_params("arbitrary", "arbitrary"),
        name="mix_xattn",
    )(x, yconv, yatt_t, g_att, w_out_a, w_out_b, g_post, g_xpre, w_qt, kx, vxt, w_o, g_xpost)


def _ffn_kernel(h_ref, gpre_ref, wgu_ref, wdown_ref, gpost_ref, out_ref):
    subs = [slice(c0, c0 + TM_FFN_SUB) for c0 in range(0, h_ref.shape[0], TM_FFN_SUB)]
    hn = [_rms_rows(h_ref[rows, :], gpre_ref[...]).astype(BF16) for rows in subs]
    gu = [_dot(v, wgu_ref[...]) for v in hn]
    act = [(v[:, 0:D_FF] * jax.nn.sigmoid(v[:, 0:D_FF]) * v[:, D_FF:2 * D_FF]).astype(BF16) for v in gu]
    down = [_dot(a, wdown_ref[...]) for a in act]
    for rows, d in zip(subs, down):
        out_ref[rows, :] = h_ref[rows, :] + _rms_rows(d, gpost_ref[...])


def _ffn(h, g_pre, w_gu, w_down, g_post):
    n, _ = h.shape
    tm = min(TM_FFN, n)
    return pl.pallas_call(
        _ffn_kernel,
        grid=(n // tm,),
        in_specs=[pl.BlockSpec((tm, D_MODEL), lambda i: (i, 0)), _whole((1, D_MODEL)),
                  _whole((D_MODEL, 2 * D_FF)), _whole((D_FF, D_MODEL)), _whole((1, D_MODEL))],
        out_specs=pl.BlockSpec((tm, D_MODEL), lambda i: (i, 0)),
        out_shape=jax.ShapeDtypeStruct((n, D_MODEL), F32),
        compiler_params=_params("arbitrary"),
        name="ffn",
    )(h, g_pre, w_gu, w_down, g_post)


def _slc_map_t(s):
    n_slab = s // CMP_STRIDE
    ci = np.arange(n_slab)[None, :] * CMP_STRIDE
    sj = np.arange(s // SLC_BLOCK)[:, None] * SLC_BLOCK
    ov = np.clip(np.minimum(ci + CMP_BLOCK, sj + SLC_BLOCK) - np.maximum(ci, sj), 0, None)
    ov[:, n_slab - 1] = 0
    return jnp.asarray(ov / CMP_BLOCK, dtype=F32)


def _tile_biases(tq):
    key = np.arange(TK)[:, None]
    qry = np.arange(tq)[None, :]
    none = np.zeros((TK, tq), np.float32)
    causal = np.where(key <= qry, 0.0, NEG).astype(np.float32)
    window_start = np.where(key > qry, 0.0, NEG).astype(np.float32)
    masked = np.full((TK, tq), NEG, np.float32)
    return jnp.asarray(np.stack([none, causal, window_start, masked]), dtype=BF16)


def kernel(x, mem, positions, norm_mix_pre, w_in, conv_w, pe_kc, w1_kc, w2_kc, pe_vc, w1_vc, w2_vc,
           norm_conv_out, norm_attn_out, w_out, norm_mix_post, norm_x_pre, norm_mem, w_q_x, w_kv_x,
           w_o_x, norm_x_post, norm_ffn_pre, w_gate_up, w_down, norm_ffn_post):
    b, s, _ = x.shape
    row = lambda v: v.reshape(1, -1).astype(F32)

    cuts = np.cumsum([0, CONV_WIDTH, CONV_WIDTH, CONV_WIDTH, ATT_WIDTH] + [KV_WIDTH] * 6 + [3 * ATT_HEADS])
    col = lambda k: w_in[:, cuts[k]:cuts[k + 1]]
    w_row = jnp.concatenate([col(0), col(1), col(2), col(4), col(5)], axis=1).astype(BF16)
    gate_cols = np.zeros((KV_GROUPS * GATE_ROWS,), np.int32)
    gate_live = np.zeros((KV_GROUPS * GATE_ROWS,), np.float32)
    for g in range(KV_GROUPS):
        for c in range(3):
            for r in range(Q_PER_KV):
                gate_cols[g * GATE_ROWS + c * Q_PER_KV + r] = (g * Q_PER_KV + r) * 3 + c
                gate_live[g * GATE_ROWS + c * Q_PER_KV + r] = 1.0
    w_gate = col(10)[:, gate_cols] * gate_live[None, :]
    w_t = jnp.concatenate([col(3), col(8), col(6), col(9), col(7), w_gate], axis=1).T.astype(BF16)

    inv = ROPE_THETA ** (-jnp.arange(HALF, dtype=F32) / HALF)
    yconv, kvc, qt, kaug, vaug, gt = _in_proj(
        x, positions.reshape(b, 1, s), inv.reshape(HALF, 1), row(norm_mix_pre), w_row, w_t,
        conv_w.astype(F32), row(norm_conv_out))

    n_slab = s // CMP_STRIDE
    pos_c = positions[:, np.minimum(np.arange(n_slab) * CMP_STRIDE + CMP_BLOCK - 1, s - 1)]
    kcmp, vcmpt = _compress(
        kvc, pos_c.reshape(b, n_slab, 1), inv.reshape(1, HALF),
        pe_kc.reshape(1, -1), w1_kc.astype(BF16), w2_kc.astype(BF16),
        pe_vc.reshape(1, -1), w1_vc.astype(BF16), w2_vc.T.astype(BF16))

    ocmp, sel = _select(qt, kcmp, vcmpt, gt, _slc_map_t(s))
    yatt_t = _nsa(qt, ocmp, sel, kaug, vaug, gt, _tile_biases(min(TQ, s)))

    kx, vxt = _mem_kv(mem, row(norm_mem), w_kv_x[:, :D_MODEL].astype(BF16), w_kv_x[:, D_MODEL:].T.astype(BF16))
    h2 = _mix_xattn(
        x, yconv, yatt_t, norm_attn_out.reshape(-1, 1).astype(F32),
        w_out[:CONV_WIDTH].astype(BF16), w_out[CONV_WIDTH:].astype(BF16),
        row(norm_mix_post), row(norm_x_pre), w_q_x.T.astype(BF16), kx, vxt, w_o_x.astype(BF16), row(norm_x_post))

    out = _ffn(h2.reshape(b * s, D_MODEL), row(norm_ffn_pre), w_gate_up.astype(BF16), w_down.astype(BF16),
               row(norm_ffn_post))
    return out.reshape(b, s, D_MODEL)
```

```python
import math

import numpy as np
import jax
import jax.numpy as jnp
from jax import lax
from jax.experimental import pallas as pl
from jax.experimental.pallas import tpu as pltpu

D_MODEL = 1024
CONV_WIDTH = 512
CONV_K = 3
ATT_HEADS = 8
HEAD_DIM = 64
HALF = HEAD_DIM // 2
ATT_WIDTH = ATT_HEADS * HEAD_DIM
KV_GROUPS = 2
Q_PER_KV = ATT_HEADS // KV_GROUPS
KV_WIDTH = KV_GROUPS * HEAD_DIM
CMP_BLOCK = 32
CMP_STRIDE = 16
CMP_HIDDEN = 256
SLC_BLOCK = 64
N_SELECT = 16
WINDOW = 512
ROPE_THETA = 10000.0
X_HEADS = 4
X_HEAD_DIM = D_MODEL // X_HEADS
D_FF = 2816
EPS = 1e-6
FORCE = 1e4
NEG = -1e30
GATE_ROWS = 16
K_AUG = 2 * HEAD_DIM
V_AUG = HEAD_DIM + 16
WIN, SEL = 0, 1
_RANK_STEP = 8
_SLOT_WIN, _N_SLOTS = 2, 5
_BIAS_NONE, _BIAS_CAUSAL, _BIAS_WINDOW_START, _BIAS_MASKED = 0, 1, 2, 3

TM_PROJ = 1024
TM_MIX = 1024
TM_FFN = 512
TM_FFN_SUB = 256
SUB_ROWS = 512
TQ = 256
TQ_SELECT = 512
TK = 256
VMEM_LIMIT = 56 * 1024 * 1024

F32 = jnp.float32
BF16 = jnp.bfloat16

_NT = (((1,), (1,)), ((), ()))
_TN = (((0,), (0,)), ((), ()))


def _dot(a, b):
    return jnp.dot(a, b, preferred_element_type=F32)


def _dot_nt(a, b):
    return lax.dot_general(a, b, _NT, preferred_element_type=F32)


def _dot_tn(a, b):
    return lax.dot_general(a, b, _TN, preferred_element_type=F32)


def _rms_rows(x, g):
    return x * lax.rsqrt(jnp.mean(x * x, axis=-1, keepdims=True) + EPS) * g


def _params(*sem):
    return pltpu.CompilerParams(dimension_semantics=sem, vmem_limit_bytes=VMEM_LIMIT)


def _whole(shape):
    nd = len(shape)
    return pl.BlockSpec(shape, lambda *_: (0,) * nd, pipeline_mode=pl.Buffered(1))


def _in_proj_kernel(x_ref, pos_ref, inv_ref, g_ref, wrow_ref, wt_ref, convw_ref, gconv_ref,
                    yconv_ref, kvc_ref, qt_ref, kaug_ref, vaug_ref, gt_ref, carry_ref):
    @pl.when(pl.program_id(1) == 0)
    def _():
        carry_ref[...] = jnp.zeros_like(carry_ref)

    tm = x_ref.shape[1]
    subs = [slice(c0, c0 + SUB_ROWS) for c0 in range(0, tm, SUB_ROWS)]
    u = [_rms_rows(x_ref[0, rows, :], g_ref[...]).astype(BF16) for rows in subs]
    prow = [_dot(v, wrow_ref[...]) for v in u]
    pt = [_dot_nt(wt_ref[...], v) for v in u]
    for rows, pr, ptr in zip(subs, prow, pt):
        _in_proj_epilogue(rows, pl.program_id(1) * tm + rows.start, pr, ptr, pos_ref, inv_ref, convw_ref, gconv_ref,
                          yconv_ref, kvc_ref, qt_ref, kaug_ref, vaug_ref, gt_ref, carry_ref)


def _in_proj_epilogue(rows, tok0, prow, pt, pos_ref, inv_ref, convw_ref, gconv_ref,
                      yconv_ref, kvc_ref, qt_ref, kaug_ref, vaug_ref, gt_ref, carry_ref):
    n = rows.stop - rows.start

    bg = prow[:, 0:CONV_WIDTH]
    z = prow[:, CONV_WIDTH:2 * CONV_WIDTH] * prow[:, 2 * CONV_WIDTH:3 * CONV_WIDTH]
    prev = carry_ref[...]
    row = lax.broadcasted_iota(jnp.int32, (n, CONV_WIDTH), 0)
    z1 = jnp.where(row == 0, prev[7:8, :], pltpu.roll(z, 1, axis=0))
    z2 = jnp.where(row == 0, prev[6:7, :], jnp.where(row == 1, prev[7:8, :], pltpu.roll(z, 2, axis=0)))
    carry_ref[...] = z[n - 8:n, :]
    cw = convw_ref[...]
    yc = bg * (cw[0:1, :] * z2 + cw[1:2, :] * z1 + cw[2:3, :] * z)
    yconv_ref[0, rows, :] = _rms_rows(yc, gconv_ref[...]).astype(yconv_ref.dtype)

    for c in range(2 * KV_GROUPS):
        lo = 3 * CONV_WIDTH + c * HEAD_DIM
        kvc_ref[0, c, rows, :] = prow[:, lo:lo + HEAD_DIM]

    ang = pos_ref[0, :, rows].astype(F32) * inv_ref[...]
    cos, sin = jnp.cos(ang), jnp.sin(ang)

    def rope_t(base):
        x1 = pt[base:base + HALF, :]
        x2 = pt[base + HALF:base + HEAD_DIM, :]
        return x1 * cos - x2 * sin, x2 * cos + x1 * sin

    scale = HEAD_DIM ** -0.5 * math.log2(math.e)
    for h in range(ATT_HEADS):
        r1, r2 = rope_t(h * HEAD_DIM)
        qt_ref[0, h * HEAD_DIM:h * HEAD_DIM + HALF, rows] = (r1 * scale).astype(qt_ref.dtype)
        qt_ref[0, h * HEAD_DIM + HALF:(h + 1) * HEAD_DIM, rows] = (r2 * scale).astype(qt_ref.dtype)

    tok = tok0 + lax.broadcasted_iota(jnp.int32, (n, HEAD_DIM), 0)
    blk_col = lax.broadcasted_iota(jnp.int32, (n, HEAD_DIM), 1)
    onehot = jnp.where(jnp.right_shift(tok, SLC_BLOCK.bit_length() - 1) == blk_col, 1.0, 0.0)
    for br in (WIN, SEL):
        base = ATT_WIDTH + br * KV_WIDTH
        parts = []
        for g in range(KV_GROUPS):
            parts.extend(rope_t(base + g * HEAD_DIM))
        k_rows = jnp.concatenate(parts, axis=0).T
        extra = onehot if br == SEL else jnp.zeros_like(onehot)
        for g in range(KV_GROUPS):
            k_g = k_rows[:, g * HEAD_DIM:(g + 1) * HEAD_DIM]
            kaug_ref[0, g, br, rows, :] = jnp.concatenate([k_g, extra], axis=-1).astype(kaug_ref.dtype)

    v0 = ATT_WIDTH + 2 * KV_WIDTH
    ones_rows = jnp.where(lax.broadcasted_iota(jnp.int32, (V_AUG - HEAD_DIM, n), 0) == 0, 1.0, 0.0)
    for br in (WIN, SEL):
        for g in range(KV_GROUPS):
            lo = v0 + br * KV_WIDTH + g * HEAD_DIM
            vaug_ref[0, g, br, :, rows] = jnp.concatenate([pt[lo:lo + HEAD_DIM, :], ones_rows], axis=0).astype(vaug_ref.dtype)

    g0 = v0 + 2 * KV_WIDTH
    gt_ref[0, :, rows] = jax.nn.sigmoid(pt[g0:g0 + KV_GROUPS * GATE_ROWS, :])


def _in_proj(x, pos_row, inv_col, g_pre, w_row, w_t, conv_w, g_conv):
    b, s, _ = x.shape
    tm = min(TM_PROJ, s)
    n_row, n_t = w_row.shape[1], w_t.shape[0]
    out_shape = (
        jax.ShapeDtypeStruct((b, s, CONV_WIDTH), BF16),
        jax.ShapeDtypeStruct((b, 2 * KV_GROUPS, s, HEAD_DIM), F32),
        jax.ShapeDtypeStruct((b, ATT_WIDTH, s), BF16),
        jax.ShapeDtypeStruct((b, KV_GROUPS, 2, s, K_AUG), BF16),
        jax.ShapeDtypeStruct((b, KV_GROUPS, 2, V_AUG, s), BF16),
        jax.ShapeDtypeStruct((b, KV_GROUPS * GATE_ROWS, s), F32),
    )
    return pl.pallas_call(
        _in_proj_kernel,
        grid=(b, s // tm),
        in_specs=[
            pl.BlockSpec((1, tm, D_MODEL), lambda i, j: (i, j, 0)),
            pl.BlockSpec((1, 1, tm), lambda i, j: (i, 0, j)),
            _whole((HALF, 1)),
            _whole((1, D_MODEL)),
            _whole((D_MODEL, n_row)),
            _whole((n_t, D_MODEL)),
            _whole((CONV_K, CONV_WIDTH)),
            _whole((1, CONV_WIDTH)),
        ],
        out_specs=(
            pl.BlockSpec((1, tm, CONV_WIDTH), lambda i, j: (i, j, 0)),
            pl.BlockSpec((1, 2 * KV_GROUPS, tm, HEAD_DIM), lambda i, j: (i, 0, j, 0)),
            pl.BlockSpec((1, ATT_WIDTH, tm), lambda i, j: (i, 0, j)),
            pl.BlockSpec((1, KV_GROUPS, 2, tm, K_AUG), lambda i, j: (i, 0, 0, j, 0)),
            pl.BlockSpec((1, KV_GROUPS, 2, V_AUG, tm), lambda i, j: (i, 0, 0, 0, j)),
            pl.BlockSpec((1, KV_GROUPS * GATE_ROWS, tm), lambda i, j: (i, 0, j)),
        ),
        out_shape=out_shape,
        scratch_shapes=[pltpu.VMEM((8, CONV_WIDTH), F32)],
        compiler_params=_params("arbitrary", "arbitrary"),
        name="in_proj",
    )(x, pos_row, inv_col, g_pre, w_row, w_t, conv_w, g_conv)


def _compress_kernel(rk_ref, rv_ref, pos_ref, inv_ref, pek_ref, w1k_ref, w2k_ref, pev_ref, w1v_ref, w2vt_ref,
                     kc_ref, vct_ref):
    half_in = CMP_STRIDE * HEAD_DIM
    n_slab = kc_ref.shape[2]

    sides = ((rk_ref, pek_ref, w1k_ref), (rv_ref, pev_ref, w1v_ref))
    a = [jnp.zeros((n_slab, CMP_HIDDEN), F32) for _ in sides]
    b = [jnp.zeros((n_slab, CMP_HIDDEN), F32) for _ in sides]
    per_dot = 256 // HEAD_DIM
    for j0 in range(0, CMP_STRIDE, per_dot):
        lo = j0 * HEAD_DIM
        xcat = [jnp.concatenate([x_ref[0, 0, pl.ds(j, n_slab, stride=CMP_STRIDE), :]
                                 for j in range(j0, j0 + per_dot)], axis=-1).astype(BF16)
                for x_ref, _, _ in sides]
        for n, (_, _, w1_ref) in enumerate(sides):
            a[n] = a[n] + _dot(xcat[n], w1_ref[lo:lo + per_dot * HEAD_DIM, :])
            b[n] = b[n] + _dot(xcat[n], w1_ref[half_in + lo:half_in + lo + per_dot * HEAD_DIM, :])
    hidden = []
    for n, (_, pe_ref, w1_ref) in enumerate(sides):
        pe = jnp.broadcast_to(pe_ref[...], (8, 2 * half_in)).astype(BF16)
        h = a[n] + pltpu.roll(b[n], n_slab - 1, axis=0) + _dot(pe, w1_ref[...])[0:1, :]
        hidden.append((h * jax.nn.sigmoid(h)).astype(BF16))

    kc = _dot(hidden[0], w2k_ref[...])
    ang = pos_ref[0].astype(F32) * inv_ref[...]
    cos, sin = jnp.cos(ang), jnp.sin(ang)
    x1, x2 = kc[:, 0:HALF], kc[:, HALF:HEAD_DIM]
    kc_ref[0, 0] = jnp.concatenate([x1 * cos - x2 * sin, x2 * cos + x1 * sin], axis=-1).astype(kc_ref.dtype)
    vct_ref[0, 0] = _dot_nt(w2vt_ref[...], hidden[1]).astype(vct_ref.dtype)


def _compress(kvc, pos_c, inv_row, pek, w1k, w2k, pev, w1v, w2vt):
    b, _, s, _ = kvc.shape
    g, n_slab = KV_GROUPS, s // CMP_STRIDE
    return pl.pallas_call(
        _compress_kernel,
        grid=(b, g),
        in_specs=[
            pl.BlockSpec((1, 1, s, HEAD_DIM), lambda i, j: (i, j, 0, 0)),
            pl.BlockSpec((1, 1, s, HEAD_DIM), lambda i, j: (i, KV_GROUPS + j, 0, 0)),
            pl.BlockSpec((1, n_slab, 1), lambda i, j: (i, 0, 0)),
            _whole((1, HALF)),
            _whole(pek.shape), _whole(w1k.shape), _whole(w2k.shape),
            _whole(pev.shape), _whole(w1v.shape), _whole(w2vt.shape),
        ],
        out_specs=(
            pl.BlockSpec((1, 1, n_slab, HEAD_DIM), lambda i, j: (i, j, 0, 0)),
            pl.BlockSpec((1, 1, HEAD_DIM, n_slab), lambda i, j: (i, j, 0, 0)),
        ),
        out_shape=(
            jax.ShapeDtypeStruct((b, g, n_slab, HEAD_DIM), BF16),
            jax.ShapeDtypeStruct((b, g, HEAD_DIM, n_slab), BF16),
        ),
        compiler_params=_params("arbitrary", "arbitrary"),
        name="compress",
    )(kvc, kvc, pos_c, inv_row, pek, w1k, w2k, pev, w1v, w2vt)


def _select_kernel(qt_ref, kc_ref, vct_ref, gt_ref, map_ref, ocmp_ref, sel_ref, cs_ref, imp_ref):
    n_cmp = kc_ref.shape[2]
    n_slc = map_ref.shape[0]
    step = pl.program_id(1)
    n_sub = qt_ref.shape[2] // TQ
    subs = [slice(c * TQ, (c + 1) * TQ) for c in range(n_sub)]
    row_of = lambda h: slice(h * HEAD_DIM, (h + 1) * HEAD_DIM)
    heads_of = lambda g: range(g * Q_PER_KV, (g + 1) * Q_PER_KV)
    t_of = [(step * n_sub + c) * TQ + lax.broadcasted_iota(jnp.int32, (1, TQ), 1) for c in range(n_sub)]

    def gate_row(h):
        row = (h // Q_PER_KV) * GATE_ROWS + h % Q_PER_KV
        return gt_ref[0, row:row + 1, :]

    def compressed_branch(n_live):
        rows = min(n_live * (SLC_BLOCK // CMP_STRIDE), n_cmp)
        for c, lanes in enumerate(subs):
            for g in range(KV_GROUPS):
                kc = kc_ref[0, g, 0:rows, :]
                for h in heads_of(g):
                    cs_ref[c, h, 0:rows, :] = _dot(kc, qt_ref[0, row_of(h), lanes])

        cmp_end = lax.broadcasted_iota(jnp.int32, (rows, TQ), 0) * CMP_STRIDE + (CMP_BLOCK - 1)
        psum = []
        for c, lanes in enumerate(subs):
            cbias = jnp.where(cmp_end <= t_of[c], 0.0, NEG)
            has_cmp = t_of[c] >= CMP_BLOCK - 1
            for g in range(KV_GROUPS):
                vct = vct_ref[0, g, :, 0:rows]
                acc = jnp.zeros((rows, TQ), F32)
                for h in heads_of(g):
                    s = cs_ref[c, h, 0:rows, :] + cbias
                    e = jnp.exp2(s - jnp.max(s, axis=0, keepdims=True))
                    inv = jnp.where(has_cmp, 1.0 / jnp.maximum(jnp.sum(e, axis=0, keepdims=True), 1e-30), 0.0)
                    acc = acc + e * inv
                    ocmp_ref[0, row_of(h), lanes] = (gate_row(h)[:, lanes] * inv) * _dot(vct, e.astype(BF16))
                psum.append(acc)

        blk = lax.broadcasted_iota(jnp.int32, (n_live, TQ), 0)
        for c in range(n_sub):
            cur = jnp.right_shift(t_of[c], SLC_BLOCK.bit_length() - 1)
            forced = (blk == 0) | (blk == cur) | (blk == cur - 1)
            for g in range(KV_GROUPS):
                imp = jnp.dot(map_ref[0:n_live, 0:rows], psum[c * KV_GROUPS + g], preferred_element_type=F32,
                              precision=lax.Precision.HIGHEST)
                imp_ref[c, g, 0:n_live, :] = jnp.where(forced, FORCE, jnp.where(blk * SLC_BLOCK > t_of[c], -FORCE, imp))

    def select_blocks(n_live, c, g):
        n_grp = n_live // 8
        imp_g = [imp_ref[c, g, 8 * a:8 * a + 8, :] for a in range(n_grp)]
        rank_g = [jnp.zeros((8, TQ), jnp.int32) for _ in range(n_grp)]
        sub = lax.broadcasted_iota(jnp.int32, (8, TQ), 0)
        for j in range(n_live):
            vj = jnp.broadcast_to(imp_ref[c, g, j:j + 1, :], (8, TQ))
            for a in range(n_grp):
                if 8 * a > j:
                    beats = jnp.where(vj >= imp_g[a], 1, 0)
                elif 8 * a + 7 < j:
                    beats = jnp.where(vj > imp_g[a], 1, 0)
                else:
                    beats = jnp.where(sub > j - 8 * a, jnp.where(vj >= imp_g[a], 1, 0), jnp.where(vj > imp_g[a], 1, 0))
                rank_g[a] = rank_g[a] + beats
        sel = jnp.where(jnp.concatenate(rank_g, axis=0) < min(N_SELECT, n_slc), 0.0, NEG).astype(BF16)
        dead = K_AUG - HEAD_DIM - n_live
        if dead:
            sel = jnp.concatenate([sel, jnp.full((dead, TQ), NEG, BF16)], axis=0)
        sel_ref[0, g, :, subs[c]] = sel

    def select_all(n_live):
        compressed_branch(n_live)
        for c in range(n_sub):
            for g in range(KV_GROUPS):
                select_blocks(n_live, c, g)

    blocks_per_step = qt_ref.shape[2] // SLC_BLOCK
    live_sizes = list(range(_RANK_STEP, n_slc, _RANK_STEP)) + [n_slc]
    for v, n_live in enumerate(live_sizes):
        lo = v * _RANK_STEP // blocks_per_step
        hi = n_live // blocks_per_step if v + 1 < len(live_sizes) else n_slc // blocks_per_step
        pl.when((step >= lo) & (step < hi))(lambda n_live=n_live: select_all(n_live))


def _select(qt, kcmp, vcmpt, gt, slc_map_t):
    b, _, s = qt.shape
    tqs = min(TQ_SELECT, s)
    n_cmp = kcmp.shape[2]
    n_slc = s // SLC_BLOCK
    assert tqs % TQ == 0 and s % tqs == 0 and n_slc % 8 == 0 and n_slc <= K_AUG - HEAD_DIM
    assert _RANK_STEP % 8 == 0 and _RANK_STEP % (tqs // SLC_BLOCK) == 0 and n_slc % (tqs // SLC_BLOCK) == 0
    per_batch = lambda shape: pl.BlockSpec((1,) + shape, lambda i, k: (i,) + (0,) * len(shape))
    return pl.pallas_call(
        _select_kernel,
        grid=(b, s // tqs),
        in_specs=[
            pl.BlockSpec((1, ATT_WIDTH, tqs), lambda i, k: (i, 0, k)),
            per_batch((KV_GROUPS, n_cmp, HEAD_DIM)),
            per_batch((KV_GROUPS, HEAD_DIM, n_cmp)),
            pl.BlockSpec((1, KV_GROUPS * GATE_ROWS, tqs), lambda i, k: (i, 0, k)),
            _whole((n_slc, n_cmp)),
        ],
        out_specs=(
            pl.BlockSpec((1, ATT_WIDTH, tqs), lambda i, k: (i, 0, k)),
            pl.BlockSpec((1, KV_GROUPS, K_AUG - HEAD_DIM, tqs), lambda i, k: (i, 0, 0, k)),
        ),
        out_shape=(
            jax.ShapeDtypeStruct((b, ATT_WIDTH, s), F32),
            jax.ShapeDtypeStruct((b, KV_GROUPS, K_AUG - HEAD_DIM, s), BF16),
        ),
        scratch_shapes=[
            pltpu.VMEM((tqs // TQ, ATT_HEADS, n_cmp, TQ), F32),
            pltpu.VMEM((tqs // TQ, KV_GROUPS, n_slc, TQ), F32),
        ],
        compiler_params=_params("arbitrary", "arbitrary"),
        name="nsa_select",
    )(qt, kcmp, vcmpt, gt, slc_map_t)


def _nsa_kernel(qt_ref, ocmp_ref, sel_ref, kaug_ref, vaug_ref, gt_ref, tri_ref,
                out_ref, rhs_ref, s_ref, m_ref, acc_ref):
    tq = qt_ref.shape[2]
    i = pl.program_id(1)
    row_of = lambda h: slice(h * HEAD_DIM, (h + 1) * HEAD_DIM)
    heads_of = lambda g: range(g * Q_PER_KV, (g + 1) * Q_PER_KV)

    def issue_scores(slot, br, tile):
        for g in range(KV_GROUPS):
            k_t = kaug_ref[0, g, br, pl.ds(pl.multiple_of(tile * TK, TK), TK), :]
            for h in heads_of(g):
                s_ref[slot, h] = _dot(k_t, rhs_ref[br, h]).astype(s_ref.dtype)

    def softmax_step(slot, br, tile, bias_idx=None, first=False):
        for g in range(KV_GROUPS):
            v_t = vaug_ref[0, g, br, :, pl.ds(pl.multiple_of(tile * TK, TK), TK)]
            for h in heads_of(g):
                s = s_ref[slot, h]
                if bias_idx is not None:
                    s = s + tri_ref[bias_idx]
                if first:
                    m_new = jnp.max(s, axis=0, keepdims=True).astype(F32)
                    acc_ref[h] = _dot(v_t, jnp.exp2(s - m_new.astype(s.dtype)))
                else:
                    m_old = m_ref[h]
                    m_new = jnp.maximum(m_old, jnp.max(s, axis=0, keepdims=True).astype(F32))
                    p = jnp.exp2(s - m_new.astype(s.dtype))
                    acc_ref[h] = jnp.exp2(m_old - m_new) * acc_ref[h] + _dot(v_t, p)
                m_ref[h] = m_new

    def gate_row(branch, h):
        row = (h // Q_PER_KV) * GATE_ROWS + branch * Q_PER_KV + h % Q_PER_KV
        return gt_ref[0, row:row + 1, :]

    def flush_state(branch):
        for h in range(ATT_HEADS):
            acc = acc_ref[h]
            o = acc[0:HEAD_DIM, :] * (1.0 / jnp.maximum(acc[HEAD_DIM:HEAD_DIM + 1, :], 1e-30))
            out_ref[0, row_of(h), :] = out_ref[0, row_of(h), :] + gate_row(branch, h) * o

    zeros_half = jnp.zeros((K_AUG - HEAD_DIM, tq), BF16)
    for h in range(ATT_HEADS):
        rhs_ref[WIN, h, 0:HEAD_DIM, :] = qt_ref[0, row_of(h), :]
        rhs_ref[WIN, h, HEAD_DIM:K_AUG, :] = zeros_half
        rhs_ref[SEL, h, 0:HEAD_DIM, :] = qt_ref[0, row_of(h), :]
        rhs_ref[SEL, h, HEAD_DIM:K_AUG, :] = sel_ref[0, h // Q_PER_KV]
    n_win = WINDOW // TK + 1
    win_tiles = [i - (n_win - 1) + k for k in range(n_win)]
    def window_step(k):
        exists = win_tiles[k] >= 0
        bias_idx = _BIAS_CAUSAL if k == n_win - 1 else jnp.where(exists, _BIAS_WINDOW_START if k == 0 else _BIAS_NONE, _BIAS_MASKED)
        softmax_step(_SLOT_WIN + k, WIN, jnp.maximum(win_tiles[k], 0), bias_idx, first=(k == 0))

    issue_scores(_SLOT_WIN, WIN, jnp.maximum(win_tiles[0], 0))
    for h in range(ATT_HEADS):
        out_ref[0, row_of(h), :] = ocmp_ref[0, row_of(h), :]
    for k in range(n_win):
        if k + 1 < n_win:
            issue_scores(_SLOT_WIN + k + 1, WIN, jnp.maximum(win_tiles[k + 1], 0))
        else:
            issue_scores(0, SEL, 0)
        window_step(k)
    flush_state(2)

    for h in range(ATT_HEADS):
        m_ref[h] = jnp.full((1, tq), NEG, F32)
        acc_ref[h] = jnp.zeros((V_AUG, tq), F32)

    def pair_steps(a):
        issue_scores(1, SEL, a + 1)
        softmax_step(0, SEL, a)
        issue_scores(0, SEL, a + 2)
        softmax_step(1, SEL, a + 1)

    def body(quad, carry):
        pair_steps(4 * quad)
        pair_steps(4 * quad + 2)
        return carry

    n_plain = jnp.right_shift(i, 1)
    n_quad = jnp.right_shift(n_plain, 1)
    lax.fori_loop(0, n_quad, body, 0)

    @pl.when(n_plain != 2 * n_quad)
    def _():
        pair_steps(4 * n_quad)

    @pl.when(i == 2 * n_plain)
    def _():
        softmax_step(0, SEL, i, _BIAS_CAUSAL)

    @pl.when(i != 2 * n_plain)
    def _():
        issue_scores(1, SEL, i)
        softmax_step(0, SEL, i - 1)
        softmax_step(1, SEL, i, _BIAS_CAUSAL)

    flush_state(1)


def _nsa(qt, ocmp, sel, kaug, vaug, gt, tri):
    b, _, s = qt.shape
    tq = min(TQ, s)
    assert tq == TK and s % tq == 0 and WINDOW % TK == 0 and _SLOT_WIN + WINDOW // TK + 1 <= _N_SLOTS
    per_batch = lambda shape: pl.BlockSpec((1,) + shape, lambda i, k: (i,) + (0,) * len(shape))
    return pl.pallas_call(
        _nsa_kernel,
        grid=(b, s // tq),
        in_specs=[
            pl.BlockSpec((1, ATT_WIDTH, tq), lambda i, k: (i, 0, k)),
            pl.BlockSpec((1, ATT_WIDTH, tq), lambda i, k: (i, 0, k)),
            pl.BlockSpec((1, KV_GROUPS, K_AUG - HEAD_DIM, tq), lambda i, k: (i, 0, 0, k)),
            per_batch((KV_GROUPS, 2, s, K_AUG)),
            per_batch((KV_GROUPS, 2, V_AUG, s)),
            pl.BlockSpec((1, KV_GROUPS * GATE_ROWS, tq), lambda i, k: (i, 0, k)),
            _whole((4, TK, tq)),
        ],
        out_specs=pl.BlockSpec((1, ATT_WIDTH, tq), lambda i, k: (i, 0, k)),
        out_shape=jax.ShapeDtypeStruct((b, ATT_WIDTH, s), F32),
        scratch_shapes=[
            pltpu.VMEM((2, ATT_HEADS, K_AUG, tq), BF16),
            pltpu.VMEM((_N_SLOTS, ATT_HEADS, TK, tq), BF16),
            pltpu.VMEM((ATT_HEADS, 1, tq), F32),
            pltpu.VMEM((ATT_HEADS, V_AUG, tq), F32),
        ],
        compiler_params=_params("arbitrary", "arbitrary"),
        name="nsa_attention",
    )(qt, ocmp, sel, kaug, vaug, gt, tri)


def _mem_kv_kernel(mem_ref, g_ref, wk_ref, wvt_ref, kx_ref, vxt_ref):
    mn = _rms_rows(mem_ref[0], g_ref[...]).astype(BF16)
    kx_ref[0] = _dot(mn, wk_ref[...]).astype(kx_ref.dtype)
    vxt_ref[0] = _dot_nt(wvt_ref[...], mn).astype(vxt_ref.dtype)


def _mem_kv(mem, g_mem, w_k, w_vt):
    b, m, _ = mem.shape
    return pl.pallas_call(
        _mem_kv_kernel,
        grid=(b,),
        in_specs=[pl.BlockSpec((1, m, D_MODEL), lambda i: (i, 0, 0)), _whole((1, D_MODEL)),
                  _whole((D_MODEL, D_MODEL)), _whole((D_MODEL, D_MODEL))],
        out_specs=(pl.BlockSpec((1, m, D_MODEL), lambda i: (i, 0, 0)),
                   pl.BlockSpec((1, D_MODEL, m), lambda i: (i, 0, 0))),
        out_shape=(jax.ShapeDtypeStruct((b, m, D_MODEL), BF16),
                   jax.ShapeDtypeStruct((b, D_MODEL, m), BF16)),
        compiler_params=_params("arbitrary"),
        name="mem_kv",
    )(mem, g_mem, w_k, w_vt)


def _mix_xattn_kernel(x_ref, yconv_ref, yatt_ref, gatt_ref, woa_ref, wob_ref, gpost_ref, gxpre_ref,
                      wqt_ref, kx_ref, vxt_ref, wo_ref, gxpost_ref, out_ref):
    subs = [slice(c0, c0 + SUB_ROWS) for c0 in range(0, x_ref.shape[1], SUB_ROWS)]
    head = lambda h: slice(h * X_HEAD_DIM, (h + 1) * X_HEAD_DIM)

    def yatt_normed(rows):
        ya = yatt_ref[0, :, rows]
        return (ya * lax.rsqrt(jnp.mean(ya * ya, axis=0, keepdims=True) + EPS) * gatt_ref[...]).astype(BF16)

    yan = [yatt_normed(rows) for rows in subs]
    mixed = [_dot(yconv_ref[0, rows, :], woa_ref[...]) + _dot_tn(y, wob_ref[...]) for rows, y in zip(subs, yan)]
    h1 = [x_ref[0, rows, :] + _rms_rows(m, gpost_ref[...]) for rows, m in zip(subs, mixed)]
    hn = [_rms_rows(h, gxpre_ref[...]).astype(BF16) for h in h1]
    qxt = [(_dot_nt(wqt_ref[...], h) * (X_HEAD_DIM ** -0.5 * math.log2(math.e))).astype(BF16) for h in hn]
    scores = [[_dot(kx_ref[0, :, head(h)], q[head(h), :]) for h in range(X_HEADS)] for q in qxt]

    def attend(sc):
        heads = []
        for h in range(X_HEADS):
            e = jnp.exp2(sc[h] - jnp.max(sc[h], axis=0, keepdims=True))
            o = _dot(vxt_ref[0, head(h), :], e.astype(BF16))
            heads.append(o * (1.0 / jnp.sum(e, axis=0, keepdims=True)))
        return jnp.concatenate(heads, axis=0).astype(BF16)

    oxt = [attend(sc) for sc in scores]
    proj = [_dot_tn(o, wo_ref[...]) for o in oxt]
    for rows, h, p in zip(subs, h1, proj):
        out_ref[0, rows, :] = h + _rms_rows(p, gxpost_ref[...])


def _mix_xattn(x, yconv, yatt_t, g_att, w_out_a, w_out_b, g_post, g_xpre, w_qt, kx, vxt, w_o, g_xpost):
    b, s, _ = x.shape
    m = kx.shape[1]
    tm = min(TM_MIX, s)
    return pl.pallas_call(
        _mix_xattn_kernel,
        grid=(b, s // tm),
        in_specs=[
            pl.BlockSpec((1, tm, D_MODEL), lambda i, j: (i, j, 0)),
            pl.BlockSpec((1, tm, CONV_WIDTH), lambda i, j: (i, j, 0)),
            pl.BlockSpec((1, ATT_WIDTH, tm), lambda i, j: (i, 0, j)),
            _whole((ATT_WIDTH, 1)),
            _whole((CONV_WIDTH, D_MODEL)), _whole((ATT_WIDTH, D_MODEL)),
            _whole((1, D_MODEL)), _whole((1, D_MODEL)),
            _whole((D_MODEL, D_MODEL)),
            pl.BlockSpec((1, m, D_MODEL), lambda i, j: (i, 0, 0)),
            pl.BlockSpec((1, D_MODEL, m), lambda i, j: (i, 0, 0)),
            _whole((D_MODEL, D_MODEL)), _whole((1, D_MODEL)),
        ],
        out_specs=pl.BlockSpec((1, tm, D_MODEL), lambda i, j: (i, j, 0)),
        out_shape=jax.ShapeDtypeStruct((b, s, D_MODEL), F32),
        compiler_params=_params("arbitrary", "arbitrary"),
        name="mix_xattn",
    )(x, yconv, yatt_t, g_att, w_out_a, w_out_b, g_post, g_xpre, w_qt, kx, vxt, w_o, g_xpost)


def _ffn_kernel(h_ref, gpre_ref, wgu_ref, wdown_ref, gpost_ref, out_ref):
    subs = [slice(c0, c0 + TM_FFN_SUB) for c0 in range(0, h_ref.shape[0], TM_FFN_SUB)]
    hn = [_rms_rows(h_ref[rows, :], gpre_ref[...]).astype(BF16) for rows in subs]
    gu = [_dot(v, wgu_ref[...]) for v in hn]
    act = [(v[:, 0:D_FF] * jax.nn.sigmoid(v[:, 0:D_FF]) * v[:, D_FF:2 * D_FF]).astype(BF16) for v in gu]
    down = [_dot(a, wdown_ref[...]) for a in act]
    for rows, d in zip(subs, down):
        out_ref[rows, :] = h_ref[rows, :] + _rms_rows(d, gpost_ref[...])


def _ffn(h, g_pre, w_gu, w_down, g_post):
    n, _ = h.shape
    tm = min(TM_FFN, n)
    return pl.pallas_call(
        _ffn_kernel,
        grid=(n // tm,),
        in_specs=[pl.BlockSpec((tm, D_MODEL), lambda i: (i, 0)), _whole((1, D_MODEL)),
                  _whole((D_MODEL, 2 * D_FF)), _whole((D_FF, D_MODEL)), _whole((1, D_MODEL))],
        out_specs=pl.BlockSpec((tm, D_MODEL), lambda i: (i, 0)),
        out_shape=jax.ShapeDtypeStruct((n, D_MODEL), F32),
        compiler_params=_params("arbitrary"),
        name="ffn",
    )(h, g_pre, w_gu, w_down, g_post)


def _slc_map_t(s):
    n_slab = s // CMP_STRIDE
    ci = np.arange(n_slab)[None, :] * CMP_STRIDE
    sj = np.arange(s // SLC_BLOCK)[:, None] * SLC_BLOCK
    ov = np.clip(np.minimum(ci + CMP_BLOCK, sj + SLC_BLOCK) - np.maximum(ci, sj), 0, None)
    ov[:, n_slab - 1] = 0
    return jnp.asarray(ov / CMP_BLOCK, dtype=F32)


def _tile_biases(tq):
    key = np.arange(TK)[:, None]
    qry = np.arange(tq)[None, :]
    none = np.zeros((TK, tq), np.float32)
    causal = np.where(key <= qry, 0.0, NEG).astype(np.float32)
    window_start = np.where(key > qry, 0.0, NEG).astype(np.float32)
    masked = np.full((TK, tq), NEG, np.float32)
    return jnp.asarray(np.stack([none, causal, window_start, masked]), dtype=BF16)


def kernel(x, mem, positions, norm_mix_pre, w_in, conv_w, pe_kc, w1_kc, w2_kc, pe_vc, w1_vc, w2_vc,
           norm_conv_out, norm_attn_out, w_out, norm_mix_post, norm_x_pre, norm_mem, w_q_x, w_kv_x,
           w_o_x, norm_x_post, norm_ffn_pre, w_gate_up, w_down, norm_ffn_post):
    b, s, _ = x.shape
    row = lambda v: v.reshape(1, -1).astype(F32)

    cuts = np.cumsum([0, CONV_WIDTH, CONV_WIDTH, CONV_WIDTH, ATT_WIDTH] + [KV_WIDTH] * 6 + [3 * ATT_HEADS])
    col = lambda k: w_in[:, cuts[k]:cuts[k + 1]]
    w_row = jnp.concatenate([col(0), col(1), col(2), col(4), col(5)], axis=1).astype(BF16)
    gate_cols = np.zeros((KV_GROUPS * GATE_ROWS,), np.int32)
    gate_live = np.zeros((KV_GROUPS * GATE_ROWS,), np.float32)
    for g in range(KV_GROUPS):
        for c in range(3):
            for r in range(Q_PER_KV):
                gate_cols[g * GATE_ROWS + c * Q_PER_KV + r] = (g * Q_PER_KV + r) * 3 + c
                gate_live[g * GATE_ROWS + c * Q_PER_KV + r] = 1.0
    w_gate = col(10)[:, gate_cols] * gate_live[None, :]
    w_t = jnp.concatenate([col(3), col(8), col(6), col(9), col(7), w_gate], axis=1).T.astype(BF16)

    inv = ROPE_THETA ** (-jnp.arange(HALF, dtype=F32) / HALF)
    yconv, kvc, qt, kaug, vaug, gt = _in_proj(
        x, positions.reshape(b, 1, s), inv.reshape(HALF, 1), row(norm_mix_pre), w_row, w_t,
        conv_w.astype(F32), row(norm_conv_out))

    n_slab = s // CMP_STRIDE
    pos_c = positions[:, np.minimum(np.arange(n_slab) * CMP_STRIDE + CMP_BLOCK - 1, s - 1)]
    kcmp, vcmpt = _compress(
        kvc, pos_c.reshape(b, n_slab, 1), inv.reshape(1, HALF),
        pe_kc.reshape(1, -1), w1_kc.astype(BF16), w2_kc.astype(BF16),
        pe_vc.reshape(1, -1), w1_vc.astype(BF16), w2_vc.T.astype(BF16))

    ocmp, sel = _select(qt, kcmp, vcmpt, gt, _slc_map_t(s))
    yatt_t = _nsa(qt, ocmp, sel, kaug, vaug, gt, _tile_biases(min(TQ, s)))

    kx, vxt = _mem_kv(mem, row(norm_mem), w_kv_x[:, :D_MODEL].astype(BF16), w_kv_x[:, D_MODEL:].T.astype(BF16))
    h2 = _mix_xattn(
        x, yconv, yatt_t, norm_attn_out.reshape(-1, 1).astype(F32),
        w_out[:CONV_WIDTH].astype(BF16), w_out[CONV_WIDTH:].astype(BF16),
        row(norm_mix_post), row(norm_x_pre), w_q_x.T.astype(BF16), kx, vxt, w_o_x.astype(BF16), row(norm_x_post))

    out = _ffn(h2.reshape(b * s, D_MODEL), row(norm_ffn_pre), w_gate_up.astype(BF16), w_down.astype(BF16),
               row(norm_ffn_post))
    return out.reshape(b, s, D_MODEL)
```

```python
import math

import numpy as np
import jax
import jax.numpy as jnp
from jax import lax
from jax.experimental import pallas as pl
from jax.experimental.pallas import tpu as pltpu

D_MODEL = 1024
CONV_WIDTH = 512
CONV_K = 3
ATT_HEADS = 8
HEAD_DIM = 64
HALF = HEAD_DIM // 2
ATT_WIDTH = ATT_HEADS * HEAD_DIM
KV_GROUPS = 2
Q_PER_KV = ATT_HEADS // KV_GROUPS
KV_WIDTH = KV_GROUPS * HEAD_DIM
CMP_BLOCK = 32
CMP_STRIDE = 16
CMP_HIDDEN = 256
SLC_BLOCK = 64
N_SELECT = 16
WINDOW = 512
ROPE_THETA = 10000.0
X_HEADS = 4
X_HEAD_DIM = D_MODEL // X_HEADS
D_FF = 2816
EPS = 1e-6
FORCE = 1e4
NEG = -1e30
GATE_ROWS = 16
K_AUG = 2 * HEAD_DIM
V_AUG = HEAD_DIM + 16
WIN, SEL = 0, 1
_RANK_STEP = 8
_SLOT_WIN, _N_SLOTS = 2, 5
_BIAS_NONE, _BIAS_CAUSAL, _BIAS_WINDOW_START, _BIAS_MASKED = 0, 1, 2, 3

TM_PROJ = 1024
TM_MIX = 1024
TM_FFN = 1024
TM_FFN_SUB = 256
SUB_ROWS = 512
TQ = 256
TQ_SELECT = 512
TK = 256
VMEM_LIMIT = 56 * 1024 * 1024

F32 = jnp.float32
BF16 = jnp.bfloat16

_NT = (((1,), (1,)), ((), ()))
_TN = (((0,), (0,)), ((), ()))


def _dot(a, b):
    return jnp.dot(a, b, preferred_element_type=F32)


def _dot_nt(a, b):
    return lax.dot_general(a, b, _NT, preferred_element_type=F32)


def _dot_tn(a, b):
    return lax.dot_general(a, b, _TN, preferred_element_type=F32)


def _rms_rows(x, g):
    return x * lax.rsqrt(jnp.mean(x * x, axis=-1, keepdims=True) + EPS) * g


def _params(*sem):
    return pltpu.CompilerParams(dimension_semantics=sem, vmem_limit_bytes=VMEM_LIMIT)


def _whole(shape):
    nd = len(shape)
    return pl.BlockSpec(shape, lambda *_: (0,) * nd, pipeline_mode=pl.Buffered(1))


def _in_proj_kernel(x_ref, pos_ref, inv_ref, g_ref, wrow_ref, wt_ref, convw_ref, gconv_ref,
                    yconv_ref, kvc_ref, qt_ref, kaug_ref, vaug_ref, gt_ref, carry_ref):
    @pl.when(pl.program_id(1) == 0)
    def _():
        carry_ref[...] = jnp.zeros_like(carry_ref)

    tm = x_ref.shape[1]
    subs = [slice(c0, c0 + SUB_ROWS) for c0 in range(0, tm, SUB_ROWS)]
    u = [_rms_rows(x_ref[0, rows, :], g_ref[...]).astype(BF16) for rows in subs]
    prow = [_dot(v, wrow_ref[...]) for v in u]
    pt = [_dot_nt(wt_ref[...], v) for v in u]
    for rows, pr, ptr in zip(subs, prow, pt):
        _in_proj_epilogue(rows, pl.program_id(1) * tm + rows.start, pr, ptr, pos_ref, inv_ref, convw_ref, gconv_ref,
                          yconv_ref, kvc_ref, qt_ref, kaug_ref, vaug_ref, gt_ref, carry_ref)


def _in_proj_epilogue(rows, tok0, prow, pt, pos_ref, inv_ref, convw_ref, gconv_ref,
                      yconv_ref, kvc_ref, qt_ref, kaug_ref, vaug_ref, gt_ref, carry_ref):
    n = rows.stop - rows.start

    bg = prow[:, 0:CONV_WIDTH]
    z = prow[:, CONV_WIDTH:2 * CONV_WIDTH] * prow[:, 2 * CONV_WIDTH:3 * CONV_WIDTH]
    prev = carry_ref[...]
    row = lax.broadcasted_iota(jnp.int32, (n, CONV_WIDTH), 0)
    z1 = jnp.where(row == 0, prev[7:8, :], pltpu.roll(z, 1, axis=0))
    z2 = jnp.where(row == 0, prev[6:7, :], jnp.where(row == 1, prev[7:8, :], pltpu.roll(z, 2, axis=0)))
    carry_ref[...] = z[n - 8:n, :]
    cw = convw_ref[...]
    yc = bg * (cw[0:1, :] * z2 + cw[1:2, :] * z1 + cw[2:3, :] * z)
    yconv_ref[0, rows, :] = _rms_rows(yc, gconv_ref[...]).astype(yconv_ref.dtype)

    for c in range(2 * KV_GROUPS):
        lo = 3 * CONV_WIDTH + c * HEAD_DIM
        kvc_ref[0, c, rows, :] = prow[:, lo:lo + HEAD_DIM]

    ang = pos_ref[0, :, rows].astype(F32) * inv_ref[...]
    cos, sin = jnp.cos(ang), jnp.sin(ang)

    def rope_t(base):
        x1 = pt[base:base + HALF, :]
        x2 = pt[base + HALF:base + HEAD_DIM, :]
        return x1 * cos - x2 * sin, x2 * cos + x1 * sin

    scale = HEAD_DIM ** -0.5 * math.log2(math.e)
    for h in range(ATT_HEADS):
        r1, r2 = rope_t(h * HEAD_DIM)
        qt_ref[0, h * HEAD_DIM:h * HEAD_DIM + HALF, rows] = (r1 * scale).astype(qt_ref.dtype)
        qt_ref[0, h * HEAD_DIM + HALF:(h + 1) * HEAD_DIM, rows] = (r2 * scale).astype(qt_ref.dtype)

    tok = tok0 + lax.broadcasted_iota(jnp.int32, (n, HEAD_DIM), 0)
    blk_col = lax.broadcasted_iota(jnp.int32, (n, HEAD_DIM), 1)
    onehot = jnp.where(jnp.right_shift(tok, SLC_BLOCK.bit_length() - 1) == blk_col, 1.0, 0.0)
    for br in (WIN, SEL):
        base = ATT_WIDTH + br * KV_WIDTH
        parts = []
        for g in range(KV_GROUPS):
            parts.extend(rope_t(base + g * HEAD_DIM))
        k_rows = jnp.concatenate(parts, axis=0).T
        extra = onehot if br == SEL else jnp.zeros_like(onehot)
        for g in range(KV_GROUPS):
            k_g = k_rows[:, g * HEAD_DIM:(g + 1) * HEAD_DIM]
            kaug_ref[0, g, br, rows, :] = jnp.concatenate([k_g, extra], axis=-1).astype(kaug_ref.dtype)

    v0 = ATT_WIDTH + 2 * KV_WIDTH
    ones_rows = jnp.where(lax.broadcasted_iota(jnp.int32, (V_AUG - HEAD_DIM, n), 0) == 0, 1.0, 0.0)
    for br in (WIN, SEL):
        for g in range(KV_GROUPS):
            lo = v0 + br * KV_WIDTH + g * HEAD_DIM
            vaug_ref[0, g, br, :, rows] = jnp.concatenate([pt[lo:lo + HEAD_DIM, :], ones_rows], axis=0).astype(vaug_ref.dtype)

    g0 = v0 + 2 * KV_WIDTH
    gt_ref[0, :, rows] = jax.nn.sigmoid(pt[g0:g0 + KV_GROUPS * GATE_ROWS, :])


def _in_proj(x, pos_row, inv_col, g_pre, w_row, w_t, conv_w, g_conv):
    b, s, _ = x.shape
    tm = min(TM_PROJ, s)
    n_row, n_t = w_row.shape[1], w_t.shape[0]
    out_shape = (
        jax.ShapeDtypeStruct((b, s, CONV_WIDTH), BF16),
        jax.ShapeDtypeStruct((b, 2 * KV_GROUPS, s, HEAD_DIM), F32),
        jax.ShapeDtypeStruct((b, ATT_WIDTH, s), BF16),
        jax.ShapeDtypeStruct((b, KV_GROUPS, 2, s, K_AUG), BF16),
        jax.ShapeDtypeStruct((b, KV_GROUPS, 2, V_AUG, s), BF16),
        jax.ShapeDtypeStruct((b, KV_GROUPS * GATE_ROWS, s), F32),
    )
    return pl.pallas_call(
        _in_proj_kernel,
        grid=(b, s // tm),
        in_specs=[
            pl.BlockSpec((1, tm, D_MODEL), lambda i, j: (i, j, 0)),
            pl.BlockSpec((1, 1, tm), lambda i, j: (i, 0, j)),
            _whole((HALF, 1)),
            _whole((1, D_MODEL)),
            _whole((D_MODEL, n_row)),
            _whole((n_t, D_MODEL)),
            _whole((CONV_K, CONV_WIDTH)),
            _whole((1, CONV_WIDTH)),
        ],
        out_specs=(
            pl.BlockSpec((1, tm, CONV_WIDTH), lambda i, j: (i, j, 0)),
            pl.BlockSpec((1, 2 * KV_GROUPS, tm, HEAD_DIM), lambda i, j: (i, 0, j, 0)),
            pl.BlockSpec((1, ATT_WIDTH, tm), lambda i, j: (i, 0, j)),
            pl.BlockSpec((1, KV_GROUPS, 2, tm, K_AUG), lambda i, j: (i, 0, 0, j, 0)),
            pl.BlockSpec((1, KV_GROUPS, 2, V_AUG, tm), lambda i, j: (i, 0, 0, 0, j)),
            pl.BlockSpec((1, KV_GROUPS * GATE_ROWS, tm), lambda i, j: (i, 0, j)),
        ),
        out_shape=out_shape,
        scratch_shapes=[pltpu.VMEM((8, CONV_WIDTH), F32)],
        compiler_params=_params("arbitrary", "arbitrary"),
        name="in_proj",
    )(x, pos_row, inv_col, g_pre, w_row, w_t, conv_w, g_conv)


def _compress_kernel(rk_ref, rv_ref, pos_ref, inv_ref, pek_ref, w1k_ref, w2k_ref, pev_ref, w1v_ref, w2vt_ref,
                     kc_ref, vct_ref):
    half_in = CMP_STRIDE * HEAD_DIM
    n_slab = kc_ref.shape[2]

    sides = ((rk_ref, pek_ref, w1k_ref), (rv_ref, pev_ref, w1v_ref))
    a = [jnp.zeros((n_slab, CMP_HIDDEN), F32) for _ in sides]
    b = [jnp.zeros((n_slab, CMP_HIDDEN), F32) for _ in sides]
    per_dot = 256 // HEAD_DIM
    for j0 in range(0, CMP_STRIDE, per_dot):
        lo = j0 * HEAD_DIM
        xcat = [jnp.concatenate([x_ref[0, 0, pl.ds(j, n_slab, stride=CMP_STRIDE), :]
                                 for j in range(j0, j0 + per_dot)], axis=-1).astype(BF16)
                for x_ref, _, _ in sides]
        for n, (_, _, w1_ref) in enumerate(sides):
            a[n] = a[n] + _dot(xcat[n], w1_ref[lo:lo + per_dot * HEAD_DIM, :])
            b[n] = b[n] + _dot(xcat[n], w1_ref[half_in + lo:half_in + lo + per_dot * HEAD_DIM, :])
    hidden = []
    for n, (_, pe_ref, w1_ref) in enumerate(sides):
        pe = jnp.broadcast_to(pe_ref[...], (8, 2 * half_in)).astype(BF16)
        h = a[n] + pltpu.roll(b[n], n_slab - 1, axis=0) + _dot(pe, w1_ref[...])[0:1, :]
        hidden.append((h * jax.nn.sigmoid(h)).astype(BF16))

    kc = _dot(hidden[0], w2k_ref[...])
    ang = pos_ref[0].astype(F32) * inv_ref[...]
    cos, sin = jnp.cos(ang), jnp.sin(ang)
    x1, x2 = kc[:, 0:HALF], kc[:, HALF:HEAD_DIM]
    kc_ref[0, 0] = jnp.concatenate([x1 * cos - x2 * sin, x2 * cos + x1 * sin], axis=-1).astype(kc_ref.dtype)
    vct_ref[0, 0] = _dot_nt(w2vt_ref[...], hidden[1]).astype(vct_ref.dtype)


def _compress(kvc, pos_c, inv_row, pek, w1k, w2k, pev, w1v, w2vt):
    b, _, s, _ = kvc.shape
    g, n_slab = KV_GROUPS, s // CMP_STRIDE
    return pl.pallas_call(
        _compress_kernel,
        grid=(b, g),
        in_specs=[
            pl.BlockSpec((1, 1, s, HEAD_DIM), lambda i, j: (i, j, 0, 0)),
            pl.BlockSpec((1, 1, s, HEAD_DIM), lambda i, j: (i, KV_GROUPS + j, 0, 0)),
            pl.BlockSpec((1, n_slab, 1), lambda i, j: (i, 0, 0)),
            _whole((1, HALF)),
            _whole(pek.shape), _whole(w1k.shape), _whole(w2k.shape),
            _whole(pev.shape), _whole(w1v.shape), _whole(w2vt.shape),
        ],
        out_specs=(
            pl.BlockSpec((1, 1, n_slab, HEAD_DIM), lambda i, j: (i, j, 0, 0)),
            pl.BlockSpec((1, 1, HEAD_DIM, n_slab), lambda i, j: (i, j, 0, 0)),
        ),
        out_shape=(
            jax.ShapeDtypeStruct((b, g, n_slab, HEAD_DIM), BF16),
            jax.ShapeDtypeStruct((b, g, HEAD_DIM, n_slab), BF16),
        ),
        compiler_params=_params("arbitrary", "arbitrary"),
        name="compress",
    )(kvc, kvc, pos_c, inv_row, pek, w1k, w2k, pev, w1v, w2vt)


def _select_kernel(qt_ref, kc_ref, vct_ref, gt_ref, map_ref, ocmp_ref, sel_ref, cs_ref, imp_ref):
    n_cmp = kc_ref.shape[2]
    n_slc = map_ref.shape[0]
    step = pl.program_id(1)
    n_sub = qt_ref.shape[2] // TQ
    subs = [slice(c * TQ, (c + 1) * TQ) for c in range(n_sub)]
    row_of = lambda h: slice(h * HEAD_DIM, (h + 1) * HEAD_DIM)
    heads_of = lambda g: range(g * Q_PER_KV, (g + 1) * Q_PER_KV)
    t_of = [(step * n_sub + c) * TQ + lax.broadcasted_iota(jnp.int32, (1, TQ), 1) for c in range(n_sub)]

    def gate_row(h):
        row = (h // Q_PER_KV) * GATE_ROWS + h % Q_PER_KV
        return gt_ref[0, row:row + 1, :]

    def compressed_branch(n_live):
        rows = min(n_live * (SLC_BLOCK // CMP_STRIDE), n_cmp)
        for c, lanes in enumerate(subs):
            for g in range(KV_GROUPS):
                kc = kc_ref[0, g, 0:rows, :]
                for h in heads_of(g):
                    cs_ref[c, h, 0:rows, :] = _dot(kc, qt_ref[0, row_of(h), lanes])

        cmp_end = lax.broadcasted_iota(jnp.int32, (rows, TQ), 0) * CMP_STRIDE + (CMP_BLOCK - 1)
        psum = []
        for c, lanes in enumerate(subs):
            cbias = jnp.where(cmp_end <= t_of[c], 0.0, NEG)
            has_cmp = t_of[c] >= CMP_BLOCK - 1
            for g in range(KV_GROUPS):
                vct = vct_ref[0, g, :, 0:rows]
                acc = jnp.zeros((rows, TQ), F32)
                for h in heads_of(g):
                    s = cs_ref[c, h, 0:rows, :] + cbias
                    e = jnp.exp2(s - jnp.max(s, axis=0, keepdims=True))
                    inv = jnp.where(has_cmp, 1.0 / jnp.maximum(jnp.sum(e, axis=0, keepdims=True), 1e-30), 0.0)
                    acc = acc + e * inv
                    ocmp_ref[0, row_of(h), lanes] = (gate_row(h)[:, lanes] * inv) * _dot(vct, e.astype(BF16))
                psum.append(acc)

        blk = lax.broadcasted_iota(jnp.int32, (n_live, TQ), 0)
        for c in range(n_sub):
            cur = jnp.right_shift(t_of[c], SLC_BLOCK.bit_length() - 1)
            forced = (blk == 0) | (blk == cur) | (blk == cur - 1)
            for g in range(KV_GROUPS):
                imp = jnp.dot(map_ref[0:n_live, 0:rows], psum[c * KV_GROUPS + g], preferred_element_type=F32,
                              precision=lax.Precision.HIGHEST)
                imp_ref[c, g, 0:n_live, :] = jnp.where(forced, FORCE, jnp.where(blk * SLC_BLOCK > t_of[c], -FORCE, imp))

    def select_blocks(n_live, c, g):
        n_grp = n_live // 8
        imp_g = [imp_ref[c, g, 8 * a:8 * a + 8, :] for a in range(n_grp)]
        rank_g = [jnp.zeros((8, TQ), jnp.int32) for _ in range(n_grp)]
        sub = lax.broadcasted_iota(jnp.int32, (8, TQ), 0)
        for j in range(n_live):
            vj = jnp.broadcast_to(imp_ref[c, g, j:j + 1, :], (8, TQ))
            for a in range(n_grp):
                if 8 * a > j:
                    beats = jnp.where(vj >= imp_g[a], 1, 0)
                elif 8 * a + 7 < j:
                    beats = jnp.where(vj > imp_g[a], 1, 0)
                else:
                    beats = jnp.where(sub > j - 8 * a, jnp.where(vj >= imp_g[a], 1, 0), jnp.where(vj > imp_g[a], 1, 0))
                rank_g[a] = rank_g[a] + beats
        sel = jnp.where(jnp.concatenate(rank_g, axis=0) < min(N_SELECT, n_slc), 0.0, NEG).astype(BF16)
        dead = K_AUG - HEAD_DIM - n_live
        if dead:
            sel = jnp.concatenate([sel, jnp.full((dead, TQ), NEG, BF16)], axis=0)
        sel_ref[0, g, :, subs[c]] = sel

    def select_all(n_live):
        compressed_branch(n_live)
        for c in range(n_sub):
            for g in range(KV_GROUPS):
                select_blocks(n_live, c, g)

    blocks_per_step = qt_ref.shape[2] // SLC_BLOCK
    live_sizes = list(range(_RANK_STEP, n_slc, _RANK_STEP)) + [n_slc]
    for v, n_live in enumerate(live_sizes):
        lo = v * _RANK_STEP // blocks_per_step
        hi = n_live // blocks_per_step if v + 1 < len(live_sizes) else n_slc // blocks_per_step
        pl.when((step >= lo) & (step < hi))(lambda n_live=n_live: select_all(n_live))


def _select(qt, kcmp, vcmpt, gt, slc_map_t):
    b, _, s = qt.shape
    tqs = min(TQ_SELECT, s)
    n_cmp = kcmp.shape[2]
    n_slc = s // SLC_BLOCK
    assert tqs % TQ == 0 and s % tqs == 0 and n_slc % 8 == 0 and n_slc <= K_AUG - HEAD_DIM
    assert _RANK_STEP % 8 == 0 and _RANK_STEP % (tqs // SLC_BLOCK) == 0 and n_slc % (tqs // SLC_BLOCK) == 0
    per_batch = lambda shape: pl.BlockSpec((1,) + shape, lambda i, k: (i,) + (0,) * len(shape))
    return pl.pallas_call(
        _select_kernel,
        grid=(b, s // tqs),
        in_specs=[
            pl.BlockSpec((1, ATT_WIDTH, tqs), lambda i, k: (i, 0, k)),
            per_batch((KV_GROUPS, n_cmp, HEAD_DIM)),
            per_batch((KV_GROUPS, HEAD_DIM, n_cmp)),
            pl.BlockSpec((1, KV_GROUPS * GATE_ROWS, tqs), lambda i, k: (i, 0, k)),
            _whole((n_slc, n_cmp)),
        ],
        out_specs=(
            pl.BlockSpec((1, ATT_WIDTH, tqs), lambda i, k: (i, 0, k)),
            pl.BlockSpec((1, KV_GROUPS, K_AUG - HEAD_DIM, tqs), lambda i, k: (i, 0, 0, k)),
        ),
        out_shape=(
            jax.ShapeDtypeStruct((b, ATT_WIDTH, s), F32),
            jax.ShapeDtypeStruct((b, KV_GROUPS, K_AUG - HEAD_DIM, s), BF16),
        ),
        scratch_shapes=[
            pltpu.VMEM((tqs // TQ, ATT_HEADS, n_cmp, TQ), F32),
            pltpu.VMEM((tqs // TQ, KV_GROUPS, n_slc, TQ), F32),
        ],
        compiler_params=_params("arbitrary", "arbitrary"),
        name="nsa_select",
    )(qt, kcmp, vcmpt, gt, slc_map_t)


def _nsa_kernel(qt_ref, ocmp_ref, sel_ref, kaug_ref, vaug_ref, gt_ref, tri_ref,
                out_ref, rhs_ref, s_ref, m_ref, acc_ref):
    tq = qt_ref.shape[2]
    i = pl.program_id(1)
    row_of = lambda h: slice(h * HEAD_DIM, (h + 1) * HEAD_DIM)
    heads_of = lambda g: range(g * Q_PER_KV, (g + 1) * Q_PER_KV)

    def issue_scores(slot, br, tile):
        for g in range(KV_GROUPS):
            k_t = kaug_ref[0, g, br, pl.ds(pl.multiple_of(tile * TK, TK), TK), :]
            for h in heads_of(g):
                s_ref[slot, h] = _dot(k_t, rhs_ref[br, h]).astype(s_ref.dtype)

    def softmax_step(slot, br, tile, bias_idx=None, first=False):
        for g in range(KV_GROUPS):
            v_t = vaug_ref[0, g, br, :, pl.ds(pl.multiple_of(tile * TK, TK), TK)]
            for h in heads_of(g):
                s = s_ref[slot, h]
                if bias_idx is not None:
                    s = s + tri_ref[bias_idx]
                if first:
                    m_new = jnp.max(s, axis=0, keepdims=True).astype(F32)
                    acc_ref[h] = _dot(v_t, jnp.exp2(s - m_new.astype(s.dtype)))
                else:
                    m_old = m_ref[h]
                    m_new = jnp.maximum(m_old, jnp.max(s, axis=0, keepdims=True).astype(F32))
                    p = jnp.exp2(s - m_new.astype(s.dtype))
                    acc_ref[h] = jnp.exp2(m_old - m_new) * acc_ref[h] + _dot(v_t, p)
                m_ref[h] = m_new

    def gate_row(branch, h):
        row = (h // Q_PER_KV) * GATE_ROWS + branch * Q_PER_KV + h % Q_PER_KV
        return gt_ref[0, row:row + 1, :]

    def flush_state(branch):
        for h in range(ATT_HEADS):
            acc = acc_ref[h]
            o = acc[0:HEAD_DIM, :] * (1.0 / jnp.maximum(acc[HEAD_DIM:HEAD_DIM + 1, :], 1e-30))
            out_ref[0, row_of(h), :] = out_ref[0, row_of(h), :] + gate_row(branch, h) * o

    zeros_half = jnp.zeros((K_AUG - HEAD_DIM, tq), BF16)
    for h in range(ATT_HEADS):
        rhs_ref[WIN, h, 0:HEAD_DIM, :] = qt_ref[0, row_of(h), :]
        rhs_ref[WIN, h, HEAD_DIM:K_AUG, :] = zeros_half
        rhs_ref[SEL, h, 0:HEAD_DIM, :] = qt_ref[0, row_of(h), :]
        rhs_ref[SEL, h, HEAD_DIM:K_AUG, :] = sel_ref[0, h // Q_PER_KV]
    n_win = WINDOW // TK + 1
    win_tiles = [i - (n_win - 1) + k for k in range(n_win)]
    def window_step(k):
        exists = win_tiles[k] >= 0
        bias_idx = _BIAS_CAUSAL if k == n_win - 1 else jnp.where(exists, _BIAS_WINDOW_START if k == 0 else _BIAS_NONE, _BIAS_MASKED)
        softmax_step(_SLOT_WIN + k, WIN, jnp.maximum(win_tiles[k], 0), bias_idx, first=(k == 0))

    for k in range(n_win):
        issue_scores(_SLOT_WIN + k, WIN, jnp.maximum(win_tiles[k], 0))
    issue_scores(0, SEL, 0)
    for h in range(ATT_HEADS):
        out_ref[0, row_of(h), :] = ocmp_ref[0, row_of(h), :]
    for k in range(n_win):
        window_step(k)
    flush_state(2)

    for h in range(ATT_HEADS):
        m_ref[h] = jnp.full((1, tq), NEG, F32)
        acc_ref[h] = jnp.zeros((V_AUG, tq), F32)

    def pair_steps(a):
        issue_scores(1, SEL, a + 1)
        softmax_step(0, SEL, a)
        issue_scores(0, SEL, a + 2)
        softmax_step(1, SEL, a + 1)

    def body(quad, carry):
        pair_steps(4 * quad)
        pair_steps(4 * quad + 2)
        return carry

    n_plain = jnp.right_shift(i, 1)
    n_quad = jnp.right_shift(n_plain, 1)
    lax.fori_loop(0, n_quad, body, 0)

    @pl.when(n_plain != 2 * n_quad)
    def _():
        pair_steps(4 * n_quad)

    @pl.when(i == 2 * n_plain)
    def _():
        softmax_step(0, SEL, i, _BIAS_CAUSAL)

    @pl.when(i != 2 * n_plain)
    def _():
        issue_scores(1, SEL, i)
        softmax_step(0, SEL, i - 1)
        softmax_step(1, SEL, i, _BIAS_CAUSAL)

    flush_state(1)


def _nsa(qt, ocmp, sel, kaug, vaug, gt, tri):
    b, _, s = qt.shape
    tq = min(TQ, s)
    assert tq == TK and s % tq == 0 and WINDOW % TK == 0 and _SLOT_WIN + WINDOW // TK + 1 <= _N_SLOTS
    per_batch = lambda shape: pl.BlockSpec((1,) + shape, lambda i, k: (i,) + (0,) * len(shape))
    return pl.pallas_call(
        _nsa_kernel,
        grid=(b, s // tq),
        in_specs=[
            pl.BlockSpec((1, ATT_WIDTH, tq), lambda i, k: (i, 0, k)),
            pl.BlockSpec((1, ATT_WIDTH, tq), lambda i, k: (i, 0, k)),
            pl.BlockSpec((1, KV_GROUPS, K_AUG - HEAD_DIM, tq), lambda i, k: (i, 0, 0, k)),
            per_batch((KV_GROUPS, 2, s, K_AUG)),
            per_batch((KV_GROUPS, 2, V_AUG, s)),
            pl.BlockSpec((1, KV_GROUPS * GATE_ROWS, tq), lambda i, k: (i, 0, k)),
            _whole((4, TK, tq)),
        ],
        out_specs=pl.BlockSpec((1, ATT_WIDTH, tq), lambda i, k: (i, 0, k)),
        out_shape=jax.ShapeDtypeStruct((b, ATT_WIDTH, s), F32),
        scratch_shapes=[
            pltpu.VMEM((2, ATT_HEADS, K_AUG, tq), BF16),
            pltpu.VMEM((_N_SLOTS, ATT_HEADS, TK, tq), BF16),
            pltpu.VMEM((ATT_HEADS, 1, tq), F32),
            pltpu.VMEM((ATT_HEADS, V_AUG, tq), F32),
        ],
        compiler_params=_params("arbitrary", "arbitrary"),
        name="nsa_attention",
    )(qt, ocmp, sel, kaug, vaug, gt, tri)


def _mem_kv_kernel(mem_ref, g_ref, wk_ref, wvt_ref, kx_ref, vxt_ref):
    mn = _rms_rows(mem_ref[0], g_ref[...]).astype(BF16)
    kx_ref[0] = _dot(mn, wk_ref[...]).astype(kx_ref.dtype)
    vxt_ref[0] = _dot_nt(wvt_ref[...], mn).astype(vxt_ref.dtype)


def _mem_kv(mem, g_mem, w_k, w_vt):
    b, m, _ = mem.shape
    return pl.pallas_call(
        _mem_kv_kernel,
        grid=(b,),
        in_specs=[pl.BlockSpec((1, m, D_MODEL), lambda i: (i, 0, 0)), _whole((1, D_MODEL)),
                  _whole((D_MODEL, D_MODEL)), _whole((D_MODEL, D_MODEL))],
        out_specs=(pl.BlockSpec((1, m, D_MODEL), lambda i: (i, 0, 0)),
                   pl.BlockSpec((1, D_MODEL, m), lambda i: (i, 0, 0))),
        out_shape=(jax.ShapeDtypeStruct((b, m, D_MODEL), BF16),
                   jax.ShapeDtypeStruct((b, D_MODEL, m), BF16)),
        compiler_params=_params("arbitrary"),
        name="mem_kv",
    )(mem, g_mem, w_k, w_vt)


def _mix_xattn_kernel(x_ref, yconv_ref, yatt_ref, gatt_ref, woa_ref, wob_ref, gpost_ref, gxpre_ref,
                      wqt_ref, kx_ref, vxt_ref, wo_ref, gxpost_ref, out_ref):
    subs = [slice(c0, c0 + SUB_ROWS) for c0 in range(0, x_ref.shape[1], SUB_ROWS)]
    head = lambda h: slice(h * X_HEAD_DIM, (h + 1) * X_HEAD_DIM)

    def yatt_normed(rows):
        ya = yatt_ref[0, :, rows]
        return (ya * lax.rsqrt(jnp.mean(ya * ya, axis=0, keepdims=True) + EPS) * gatt_ref[...]).astype(BF16)

    yan = [yatt_normed(rows) for rows in subs]
    mixed = [_dot(yconv_ref[0, rows, :], woa_ref[...]) + _dot_tn(y, wob_ref[...]) for rows, y in zip(subs, yan)]
    h1 = [x_ref[0, rows, :] + _rms_rows(m, gpost_ref[...]) for rows, m in zip(subs, mixed)]
    hn = [_rms_rows(h, gxpre_ref[...]).astype(BF16) for h in h1]
    qxt = [(_dot_nt(wqt_ref[...], h) * (X_HEAD_DIM ** -0.5 * math.log2(math.e))).astype(BF16) for h in hn]
    scores = [[_dot(kx_ref[0, :, head(h)], q[head(h), :]) for h in range(X_HEADS)] for q in qxt]

    def attend(sc):
        heads = []
        for h in range(X_HEADS):
            e = jnp.exp2(sc[h] - jnp.max(sc[h], axis=0, keepdims=True))
            o = _dot(vxt_ref[0, head(h), :], e.astype(BF16))
            heads.append(o * (1.0 / jnp.sum(e, axis=0, keepdims=True)))
        return jnp.concatenate(heads, axis=0).astype(BF16)

    oxt = [attend(sc) for sc in scores]
    proj = [_dot_tn(o, wo_ref[...]) for o in oxt]
    for rows, h, p in zip(subs, h1, proj):
        out_ref[0, rows, :] = h + _rms_rows(p, gxpost_ref[...])


def _mix_xattn(x, yconv, yatt_t, g_att, w_out_a, w_out_b, g_post, g_xpre, w_qt, kx, vxt, w_o, g_xpost):
    b, s, _ = x.shape
    m = kx.shape[1]
    tm = min(TM_MIX, s)
    return pl.pallas_call(
        _mix_xattn_kernel,
        grid=(b, s // tm),
        in_specs=[
            pl.BlockSpec((1, tm, D_MODEL), lambda i, j: (i, j, 0)),
            pl.BlockSpec((1, tm, CONV_WIDTH), lambda i, j: (i, j, 0)),
            pl.BlockSpec((1, ATT_WIDTH, tm), lambda i, j: (i, 0, j)),
            _whole((ATT_WIDTH, 1)),
            _whole((CONV_WIDTH, D_MODEL)), _whole((ATT_WIDTH, D_MODEL)),
            _whole((1, D_MODEL)), _whole((1, D_MODEL)),
            _whole((D_MODEL, D_MODEL)),
            pl.BlockSpec((1, m, D_MODEL), lambda i, j: (i, 0, 0)),
            pl.BlockSpec((1, D_MODEL, m), lambda i, j: (i, 0, 0)),
            _whole((D_MODEL, D_MODEL)), _whole((1, D_MODEL)),
        ],
        out_specs=pl.BlockSpec((1, tm, D_MODEL), lambda i, j: (i, j, 0)),
        out_shape=jax.ShapeDtypeStruct((b, s, D_MODEL), F32),
        compiler_params=_params("arbitrary", "arbitrary"),
        name="mix_xattn",
    )(x, yconv, yatt_t, g_att, w_out_a, w_out_b, g_post, g_xpre, w_qt, kx, vxt, w_o, g_xpost)


def _ffn_kernel(h_ref, gpre_ref, wgu_ref, wdown_ref, gpost_ref, out_ref):
    subs = [slice(c0, c0 + TM_FFN_SUB) for c0 in range(0, h_ref.shape[0], TM_FFN_SUB)]
    hn = [_rms_rows(h_ref[rows, :], gpre_ref[...]).astype(BF16) for rows in subs]
    gu = [_dot(v, wgu_ref[...]) for v in hn]
    act = [(v[:, 0:D_FF] * jax.nn.sigmoid(v[:, 0:D_FF]) * v[:, D_FF:2 * D_FF]).astype(BF16) for v in gu]
    down = [_dot(a, wdown_ref[...]) for a in act]
    for rows, d in zip(subs, down):
        out_ref[rows, :] = h_ref[rows, :] + _rms_rows(d, gpost_ref[...])


def _ffn(h, g_pre, w_gu, w_down, g_post):
    n, _ = h.shape
    tm = min(TM_FFN, n)
    return pl.pallas_call(
        _ffn_kernel,
        grid=(n // tm,),
        in_specs=[pl.BlockSpec((tm, D_MODEL), lambda i: (i, 0)), _whole((1, D_MODEL)),
                  _whole((D_MODEL, 2 * D_FF)), _whole((D_FF, D_MODEL)), _whole((1, D_MODEL))],
        out_specs=pl.BlockSpec((tm, D_MODEL), lambda i: (i, 0)),
        out_shape=jax.ShapeDtypeStruct((n, D_MODEL), F32),
        compiler_params=_params("arbitrary"),
        name="ffn",
    )(h, g_pre, w_gu, w_down, g_post)


def _slc_map_t(s):
    n_slab = s // CMP_STRIDE
    ci = np.arange(n_slab)[None, :] * CMP_STRIDE
    sj = np.arange(s // SLC_BLOCK)[:, None] * SLC_BLOCK
    ov = np.clip(np.minimum(ci + CMP_BLOCK, sj + SLC_BLOCK) - np.maximum(ci, sj), 0, None)
    ov[:, n_slab - 1] = 0
    return jnp.asarray(ov / CMP_BLOCK, dtype=F32)


def _tile_biases(tq):
    key = np.arange(TK)[:, None]
    qry = np.arange(tq)[None, :]
    none = np.zeros((TK, tq), np.float32)
    causal = np.where(key <= qry, 0.0, NEG).astype(np.float32)
    window_start = np.where(key > qry, 0.0, NEG).astype(np.float32)
    masked = np.full((TK, tq), NEG, np.float32)
    return jnp.asarray(np.stack([none, causal, window_start, masked]), dtype=BF16)


def kernel(x, mem, positions, norm_mix_pre, w_in, conv_w, pe_kc, w1_kc, w2_kc, pe_vc, w1_vc, w2_vc,
           norm_conv_out, norm_attn_out, w_out, norm_mix_post, norm_x_pre, norm_mem, w_q_x, w_kv_x,
           w_o_x, norm_x_post, norm_ffn_pre, w_gate_up, w_down, norm_ffn_post):
    b, s, _ = x.shape
    row = lambda v: v.reshape(1, -1).astype(F32)

    cuts = np.cumsum([0, CONV_WIDTH, CONV_WIDTH, CONV_WIDTH, ATT_WIDTH] + [KV_WIDTH] * 6 + [3 * ATT_HEADS])
    col = lambda k: w_in[:, cuts[k]:cuts[k + 1]]
    w_row = jnp.concatenate([col(0), col(1), col(2), col(4), col(5)], axis=1).astype(BF16)
    gate_cols = np.zeros((KV_GROUPS * GATE_ROWS,), np.int32)
    gate_live = np.zeros((KV_GROUPS * GATE_ROWS,), np.float32)
    for g in range(KV_GROUPS):
        for c in range(3):
            for r in range(Q_PER_KV):
                gate_cols[g * GATE_ROWS + c * Q_PER_KV + r] = (g * Q_PER_KV + r) * 3 + c
                gate_live[g * GATE_ROWS + c * Q_PER_KV + r] = 1.0
    w_gate = col(10)[:, gate_cols] * gate_live[None, :]
    w_t = jnp.concatenate([col(3), col(8), col(6), col(9), col(7), w_gate], axis=1).T.astype(BF16)

    inv = ROPE_THETA ** (-jnp.arange(HALF, dtype=F32) / HALF)
    yconv, kvc, qt, kaug, vaug, gt = _in_proj(
        x, positions.reshape(b, 1, s), inv.reshape(HALF, 1), row(norm_mix_pre), w_row, w_t,
        conv_w.astype(F32), row(norm_conv_out))

    n_slab = s // CMP_STRIDE
    pos_c = positions[:, np.minimum(np.arange(n_slab) * CMP_STRIDE + CMP_BLOCK - 1, s - 1)]
    kcmp, vcmpt = _compress(
        kvc, pos_c.reshape(b, n_slab, 1), inv.reshape(1, HALF),
        pe_kc.reshape(1, -1), w1_kc.astype(BF16), w2_kc.astype(BF16),
        pe_vc.reshape(1, -1), w1_vc.astype(BF16), w2_vc.T.astype(BF16))

    ocmp, sel = _select(qt, kcmp, vcmpt, gt, _slc_map_t(s))
    yatt_t = _nsa(qt, ocmp, sel, kaug, vaug, gt, _tile_biases(min(TQ, s)))

    kx, vxt = _mem_kv(mem, row(norm_mem), w_kv_x[:, :D_MODEL].astype(BF16), w_kv_x[:, D_MODEL:].T.astype(BF16))
    h2 = _mix_xattn(
        x, yconv, yatt_t, norm_attn_out.reshape(-1, 1).astype(F32),
        w_out[:CONV_WIDTH].astype(BF16), w_out[CONV_WIDTH:].astype(BF16),
        row(norm_mix_post), row(norm_x_pre), w_q_x.T.astype(BF16), kx, vxt, w_o_x.astype(BF16), row(norm_x_post))

    out = _ffn(h2.reshape(b * s, D_MODEL), row(norm_ffn_pre), w_gate_up.astype(BF16), w_down.astype(BF16),
               row(norm_ffn_post))
    return out.reshape(b, s, D_MODEL)
```

```python
import math

import numpy as np
import jax
import jax.numpy as jnp
from jax import lax
from jax.experimental import pallas as pl
from jax.experimental.pallas import tpu as pltpu

D_MODEL = 1024
CONV_WIDTH = 512
CONV_K = 3
ATT_HEADS = 8
HEAD_DIM = 64
HALF = HEAD_DIM // 2
ATT_WIDTH = ATT_HEADS * HEAD_DIM
KV_GROUPS = 2
Q_PER_KV = ATT_HEADS // KV_GROUPS
KV_WIDTH = KV_GROUPS * HEAD_DIM
CMP_BLOCK = 32
CMP_STRIDE = 16
CMP_HIDDEN = 256
SLC_BLOCK = 64
N_SELECT = 16
WINDOW = 512
ROPE_THETA = 10000.0
X_HEADS = 4
X_HEAD_DIM = D_MODEL // X_HEADS
D_FF = 2816
EPS = 1e-6
FORCE = 1e4
NEG = -1e30
GATE_ROWS = 16
K_AUG = 2 * HEAD_DIM
V_AUG = HEAD_DIM + 16
WIN, SEL = 0, 1
_RANK_STEP = 8
_SLOT_WIN, _N_SLOTS = 2, 5
_BIAS_NONE, _BIAS_CAUSAL, _BIAS_WINDOW_START, _BIAS_MASKED = 0, 1, 2, 3

TM_PROJ = 1024
TM_MIX = 1024
TM_FFN = 1024
TM_FFN_SUB = 256
SUB_ROWS = 512
TQ = 256
TQ_SELECT = 512
TK = 256
VMEM_LIMIT = 56 * 1024 * 1024

F32 = jnp.float32
BF16 = jnp.bfloat16

_NT = (((1,), (1,)), ((), ()))
_TN = (((0,), (0,)), ((), ()))


def _dot(a, b):
    return jnp.dot(a, b, preferred_element_type=F32)


def _dot_nt(a, b):
    return lax.dot_general(a, b, _NT, preferred_element_type=F32)


def _dot_tn(a, b):
    return lax.dot_general(a, b, _TN, preferred_element_type=F32)


def _rms_rows(x, g):
    return x * lax.rsqrt(jnp.mean(x * x, axis=-1, keepdims=True) + EPS) * g


def _params(*sem):
    return pltpu.CompilerParams(dimension_semantics=sem, vmem_limit_bytes=VMEM_LIMIT)


def _whole(shape):
    nd = len(shape)
    return pl.BlockSpec(shape, lambda *_: (0,) * nd, pipeline_mode=pl.Buffered(1))


def _in_proj_kernel(x_ref, pos_ref, inv_ref, g_ref, wrow_ref, wt_ref, convw_ref, gconv_ref,
                    yconv_ref, kvc_ref, qt_ref, kaug_ref, vaug_ref, gt_ref, carry_ref):
    @pl.when(pl.program_id(1) == 0)
    def _():
        carry_ref[...] = jnp.zeros_like(carry_ref)

    tm = x_ref.shape[1]
    subs = [slice(c0, c0 + SUB_ROWS) for c0 in range(0, tm, SUB_ROWS)]
    u = [_rms_rows(x_ref[0, rows, :], g_ref[...]).astype(BF16) for rows in subs]
    prow = [_dot(v, wrow_ref[...]) for v in u]
    pt = [_dot_nt(wt_ref[...], v) for v in u]
    for rows, pr, ptr in zip(subs, prow, pt):
        _in_proj_epilogue(rows, pl.program_id(1) * tm + rows.start, pr, ptr, pos_ref, inv_ref, convw_ref, gconv_ref,
                          yconv_ref, kvc_ref, qt_ref, kaug_ref, vaug_ref, gt_ref, carry_ref)


def _in_proj_epilogue(rows, tok0, prow, pt, pos_ref, inv_ref, convw_ref, gconv_ref,
                      yconv_ref, kvc_ref, qt_ref, kaug_ref, vaug_ref, gt_ref, carry_ref):
    n = rows.stop - rows.start

    bg = prow[:, 0:CONV_WIDTH]
    z = prow[:, CONV_WIDTH:2 * CONV_WIDTH] * prow[:, 2 * CONV_WIDTH:3 * CONV_WIDTH]
    prev = carry_ref[...]
    row = lax.broadcasted_iota(jnp.int32, (n, CONV_WIDTH), 0)
    z1 = jnp.where(row == 0, prev[7:8, :], pltpu.roll(z, 1, axis=0))
    z2 = jnp.where(row == 0, prev[6:7, :], jnp.where(row == 1, prev[7:8, :], pltpu.roll(z, 2, axis=0)))
    carry_ref[...] = z[n - 8:n, :]
    cw = convw_ref[...]
    yc = bg * (cw[0:1, :] * z2 + cw[1:2, :] * z1 + cw[2:3, :] * z)
    yconv_ref[0, rows, :] = _rms_rows(yc, gconv_ref[...]).astype(yconv_ref.dtype)

    for c in range(2 * KV_GROUPS):
        lo = 3 * CONV_WIDTH + c * HEAD_DIM
        kvc_ref[0, c, rows, :] = prow[:, lo:lo + HEAD_DIM]

    ang = pos_ref[0, :, rows].astype(F32) * inv_ref[...]
    cos, sin = jnp.cos(ang), jnp.sin(ang)

    def rope_t(base):
        x1 = pt[base:base + HALF, :]
        x2 = pt[base + HALF:base + HEAD_DIM, :]
        return x1 * cos - x2 * sin, x2 * cos + x1 * sin

    scale = HEAD_DIM ** -0.5 * math.log2(math.e)
    for h in range(ATT_HEADS):
        r1, r2 = rope_t(h * HEAD_DIM)
        qt_ref[0, h * HEAD_DIM:h * HEAD_DIM + HALF, rows] = (r1 * scale).astype(qt_ref.dtype)
        qt_ref[0, h * HEAD_DIM + HALF:(h + 1) * HEAD_DIM, rows] = (r2 * scale).astype(qt_ref.dtype)

    tok = tok0 + lax.broadcasted_iota(jnp.int32, (n, HEAD_DIM), 0)
    blk_col = lax.broadcasted_iota(jnp.int32, (n, HEAD_DIM), 1)
    onehot = jnp.where(jnp.right_shift(tok, SLC_BLOCK.bit_length() - 1) == blk_col, 1.0, 0.0)
    for br in (WIN, SEL):
        base = ATT_WIDTH + br * KV_WIDTH
        parts = []
        for g in range(KV_GROUPS):
            parts.extend(rope_t(base + g * HEAD_DIM))
        k_rows = jnp.concatenate(parts, axis=0).T
        extra = onehot if br == SEL else jnp.zeros_like(onehot)
        for g in range(KV_GROUPS):
            k_g = k_rows[:, g * HEAD_DIM:(g + 1) * HEAD_DIM]
            kaug_ref[0, g, br, rows, :] = jnp.concatenate([k_g, extra], axis=-1).astype(kaug_ref.dtype)

    v0 = ATT_WIDTH + 2 * KV_WIDTH
    ones_rows = jnp.where(lax.broadcasted_iota(jnp.int32, (V_AUG - HEAD_DIM, n), 0) == 0, 1.0, 0.0)
    for br in (WIN, SEL):
        for g in range(KV_GROUPS):
            lo = v0 + br * KV_WIDTH + g * HEAD_DIM
            vaug_ref[0, g, br, :, rows] = jnp.concatenate([pt[lo:lo + HEAD_DIM, :], ones_rows], axis=0).astype(vaug_ref.dtype)

    g0 = v0 + 2 * KV_WIDTH
    gt_ref[0, :, rows] = jax.nn.sigmoid(pt[g0:g0 + KV_GROUPS * GATE_ROWS, :])


def _in_proj(x, pos_row, inv_col, g_pre, w_row, w_t, conv_w, g_conv):
    b, s, _ = x.shape
    tm = min(TM_PROJ, s)
    n_row, n_t = w_row.shape[1], w_t.shape[0]
    out_shape = (
        jax.ShapeDtypeStruct((b, s, CONV_WIDTH), BF16),
        jax.ShapeDtypeStruct((b, 2 * KV_GROUPS, s, HEAD_DIM), F32),
        jax.ShapeDtypeStruct((b, ATT_WIDTH, s), BF16),
        jax.ShapeDtypeStruct((b, KV_GROUPS, 2, s, K_AUG), BF16),
        jax.ShapeDtypeStruct((b, KV_GROUPS, 2, V_AUG, s), BF16),
        jax.ShapeDtypeStruct((b, KV_GROUPS * GATE_ROWS, s), F32),
    )
    return pl.pallas_call(
        _in_proj_kernel,
        grid=(b, s // tm),
        in_specs=[
            pl.BlockSpec((1, tm, D_MODEL), lambda i, j: (i, j, 0)),
            pl.BlockSpec((1, 1, tm), lambda i, j: (i, 0, j)),
            _whole((HALF, 1)),
            _whole((1, D_MODEL)),
            _whole((D_MODEL, n_row)),
            _whole((n_t, D_MODEL)),
            _whole((CONV_K, CONV_WIDTH)),
            _whole((1, CONV_WIDTH)),
        ],
        out_specs=(
            pl.BlockSpec((1, tm, CONV_WIDTH), lambda i, j: (i, j, 0)),
            pl.BlockSpec((1, 2 * KV_GROUPS, tm, HEAD_DIM), lambda i, j: (i, 0, j, 0)),
            pl.BlockSpec((1, ATT_WIDTH, tm), lambda i, j: (i, 0, j)),
            pl.BlockSpec((1, KV_GROUPS, 2, tm, K_AUG), lambda i, j: (i, 0, 0, j, 0)),
            pl.BlockSpec((1, KV_GROUPS, 2, V_AUG, tm), lambda i, j: (i, 0, 0, 0, j)),
            pl.BlockSpec((1, KV_GROUPS * GATE_ROWS, tm), lambda i, j: (i, 0, j)),
        ),
        out_shape=out_shape,
        scratch_shapes=[pltpu.VMEM((8, CONV_WIDTH), F32)],
        compiler_params=_params("arbitrary", "arbitrary"),
        name="in_proj",
    )(x, pos_row, inv_col, g_pre, w_row, w_t, conv_w, g_conv)


def _compress_kernel(rk_ref, rv_ref, pos_ref, inv_ref, pek_ref, w1k_ref, w2k_ref, pev_ref, w1v_ref, w2vt_ref,
                     kc_ref, vct_ref):
    half_in = CMP_STRIDE * HEAD_DIM
    n_slab = kc_ref.shape[2]

    sides = ((rk_ref, pek_ref, w1k_ref), (rv_ref, pev_ref, w1v_ref))
    a = [jnp.zeros((n_slab, CMP_HIDDEN), F32) for _ in sides]
    b = [jnp.zeros((n_slab, CMP_HIDDEN), F32) for _ in sides]
    per_dot = 256 // HEAD_DIM
    for j0 in range(0, CMP_STRIDE, per_dot):
        lo = j0 * HEAD_DIM
        xcat = [jnp.concatenate([x_ref[0, 0, pl.ds(j, n_slab, stride=CMP_STRIDE), :]
                                 for j in range(j0, j0 + per_dot)], axis=-1).astype(BF16)
                for x_ref, _, _ in sides]
        for n, (_, _, w1_ref) in enumerate(sides):
            a[n] = a[n] + _dot(xcat[n], w1_ref[lo:lo + per_dot * HEAD_DIM, :])
            b[n] = b[n] + _dot(xcat[n], w1_ref[half_in + lo:half_in + lo + per_dot * HEAD_DIM, :])
    hidden = []
    for n, (_, pe_ref, w1_ref) in enumerate(sides):
        pe = jnp.broadcast_to(pe_ref[...], (8, 2 * half_in)).astype(BF16)
        h = a[n] + pltpu.roll(b[n], n_slab - 1, axis=0) + _dot(pe, w1_ref[...])[0:1, :]
        hidden.append((h * jax.nn.sigmoid(h)).astype(BF16))

    kc = _dot(hidden[0], w2k_ref[...])
    ang = pos_ref[0].astype(F32) * inv_ref[...]
    cos, sin = jnp.cos(ang), jnp.sin(ang)
    x1, x2 = kc[:, 0:HALF], kc[:, HALF:HEAD_DIM]
    kc_ref[0, 0] = jnp.concatenate([x1 * cos - x2 * sin, x2 * cos + x1 * sin], axis=-1).astype(kc_ref.dtype)
    vct_ref[0, 0] = _dot_nt(w2vt_ref[...], hidden[1]).astype(vct_ref.dtype)


def _compress(kvc, pos_c, inv_row, pek, w1k, w2k, pev, w1v, w2vt):
    b, _, s, _ = kvc.shape
    g, n_slab = KV_GROUPS, s // CMP_STRIDE
    return pl.pallas_call(
        _compress_kernel,
        grid=(b, g),
        in_specs=[
            pl.BlockSpec((1, 1, s, HEAD_DIM), lambda i, j: (i, j, 0, 0)),
            pl.BlockSpec((1, 1, s, HEAD_DIM), lambda i, j: (i, KV_GROUPS + j, 0, 0)),
            pl.BlockSpec((1, n_slab, 1), lambda i, j: (i, 0, 0)),
            _whole((1, HALF)),
            _whole(pek.shape), _whole(w1k.shape), _whole(w2k.shape),
            _whole(pev.shape), _whole(w1v.shape), _whole(w2vt.shape),
        ],
        out_specs=(
            pl.BlockSpec((1, 1, n_slab, HEAD_DIM), lambda i, j: (i, j, 0, 0)),
            pl.BlockSpec((1, 1, HEAD_DIM, n_slab), lambda i, j: (i, j, 0, 0)),
        ),
        out_shape=(
            jax.ShapeDtypeStruct((b, g, n_slab, HEAD_DIM), BF16),
            jax.ShapeDtypeStruct((b, g, HEAD_DIM, n_slab), BF16),
        ),
        compiler_params=_params("arbitrary", "arbitrary"),
        name="compress",
    )(kvc, kvc, pos_c, inv_row, pek, w1k, w2k, pev, w1v, w2vt)


def _select_kernel(qt_ref, kc_ref, vct_ref, gt_ref, map_ref, ocmp_ref, sel_ref, cs_ref, imp_ref):
    n_cmp = kc_ref.shape[2]
    n_slc = map_ref.shape[0]
    step = pl.program_id(1)
    n_sub = qt_ref.shape[2] // TQ
    subs = [slice(c * TQ, (c + 1) * TQ) for c in range(n_sub)]
    row_of = lambda h: slice(h * HEAD_DIM, (h + 1) * HEAD_DIM)
    heads_of = lambda g: range(g * Q_PER_KV, (g + 1) * Q_PER_KV)
    t_of = [(step * n_sub + c) * TQ + lax.broadcasted_iota(jnp.int32, (1, TQ), 1) for c in range(n_sub)]

    def gate_row(h):
        row = (h // Q_PER_KV) * GATE_ROWS + h % Q_PER_KV
        return gt_ref[0, row:row + 1, :]

    def compressed_branch(n_live):
        rows = min(n_live * (SLC_BLOCK // CMP_STRIDE), n_cmp)
        for c, lanes in enumerate(subs):
            for g in range(KV_GROUPS):
                kc = kc_ref[0, g, 0:rows, :]
                for h in heads_of(g):
                    cs_ref[c, h, 0:rows, :] = _dot(kc, qt_ref[0, row_of(h), lanes])

        cmp_end = lax.broadcasted_iota(jnp.int32, (rows, TQ), 0) * CMP_STRIDE + (CMP_BLOCK - 1)
        psum = []
        for c, lanes in enumerate(subs):
            cbias = jnp.where(cmp_end <= t_of[c], 0.0, NEG)
            has_cmp = t_of[c] >= CMP_BLOCK - 1
            for g in range(KV_GROUPS):
                vct = vct_ref[0, g, :, 0:rows]
                acc = jnp.zeros((rows, TQ), F32)
                for h in heads_of(g):
                    s = cs_ref[c, h, 0:rows, :] + cbias
                    e = jnp.exp2(s - jnp.max(s, axis=0, keepdims=True))
                    inv = jnp.where(has_cmp, 1.0 / jnp.maximum(jnp.sum(e, axis=0, keepdims=True), 1e-30), 0.0)
                    acc = acc + e * inv
                    ocmp_ref[0, row_of(h), lanes] = (gate_row(h)[:, lanes] * inv) * _dot(vct, e.astype(BF16))
                psum.append(acc)

        blk = lax.broadcasted_iota(jnp.int32, (n_live, TQ), 0)
        for c in range(n_sub):
            cur = jnp.right_shift(t_of[c], SLC_BLOCK.bit_length() - 1)
            forced = (blk == 0) | (blk == cur) | (blk == cur - 1)
            for g in range(KV_GROUPS):
                imp = jnp.dot(map_ref[0:n_live, 0:rows], psum[c * KV_GROUPS + g], preferred_element_type=F32,
                              precision=lax.Precision.HIGHEST)
                imp_ref[c, g, 0:n_live, :] = jnp.where(forced, FORCE, jnp.where(blk * SLC_BLOCK > t_of[c], -FORCE, imp))

    def select_blocks(n_live, c, g):
        n_grp = n_live // 8
        imp_g = [imp_ref[c, g, 8 * a:8 * a + 8, :] for a in range(n_grp)]
        rank_g = [jnp.zeros((8, TQ), jnp.int32) for _ in range(n_grp)]
        sub = lax.broadcasted_iota(jnp.int32, (8, TQ), 0)
        for j in range(n_live):
            vj = jnp.broadcast_to(imp_ref[c, g, j:j + 1, :], (8, TQ))
            for a in range(n_grp):
                if 8 * a > j:
                    beats = jnp.where(vj >= imp_g[a], 1, 0)
                elif 8 * a + 7 < j:
                    beats = jnp.where(vj > imp_g[a], 1, 0)
                else:
                    beats = jnp.where(sub > j - 8 * a, jnp.where(vj >= imp_g[a], 1, 0), jnp.where(vj > imp_g[a], 1, 0))
                rank_g[a] = rank_g[a] + beats
        sel = jnp.where(jnp.concatenate(rank_g, axis=0) < min(N_SELECT, n_slc), 0.0, NEG).astype(BF16)
        dead = K_AUG - HEAD_DIM - n_live
        if dead:
            sel = jnp.concatenate([sel, jnp.full((dead, TQ), NEG, BF16)], axis=0)
        sel_ref[0, g, :, subs[c]] = sel

    def select_all(n_live):
        compressed_branch(n_live)
        for c in range(n_sub):
            for g in range(KV_GROUPS):
                select_blocks(n_live, c, g)

    blocks_per_step = qt_ref.shape[2] // SLC_BLOCK
    live_sizes = list(range(_RANK_STEP, n_slc, _RANK_STEP)) + [n_slc]
    for v, n_live in enumerate(live_sizes):
        lo = v * _RANK_STEP // blocks_per_step
        hi = n_live // blocks_per_step if v + 1 < len(live_sizes) else n_slc // blocks_per_step
        pl.when((step >= lo) & (step < hi))(lambda n_live=n_live: select_all(n_live))


def _select(qt, kcmp, vcmpt, gt, slc_map_t):
    b, _, s = qt.shape
    tqs = min(TQ_SELECT, s)
    n_cmp = kcmp.shape[2]
    n_slc = s // SLC_BLOCK
    assert tqs % TQ == 0 and s % tqs == 0 and n_slc % 8 == 0 and n_slc <= K_AUG - HEAD_DIM
    assert _RANK_STEP % 8 == 0 and _RANK_STEP % (tqs // SLC_BLOCK) == 0 and n_slc % (tqs // SLC_BLOCK) == 0
    per_batch = lambda shape: pl.BlockSpec((1,) + shape, lambda i, k: (i,) + (0,) * len(shape))
    return pl.pallas_call(
        _select_kernel,
        grid=(b, s // tqs),
        in_specs=[
            pl.BlockSpec((1, ATT_WIDTH, tqs), lambda i, k: (i, 0, k)),
            per_batch((KV_GROUPS, n_cmp, HEAD_DIM)),
            per_batch((KV_GROUPS, HEAD_DIM, n_cmp)),
            pl.BlockSpec((1, KV_GROUPS * GATE_ROWS, tqs), lambda i, k: (i, 0, k)),
            _whole((n_slc, n_cmp)),
        ],
        out_specs=(
            pl.BlockSpec((1, ATT_WIDTH, tqs), lambda i, k: (i, 0, k)),
            pl.BlockSpec((1, KV_GROUPS, K_AUG - HEAD_DIM, tqs), lambda i, k: (i, 0, 0, k)),
        ),
        out_shape=(
            jax.ShapeDtypeStruct((b, ATT_WIDTH, s), F32),
            jax.ShapeDtypeStruct((b, KV_GROUPS, K_AUG - HEAD_DIM, s), BF16),
        ),
        scratch_shapes=[
            pltpu.VMEM((tqs // TQ, ATT_HEADS, n_cmp, TQ), F32),
            pltpu.VMEM((tqs // TQ, KV_GROUPS, n_slc, TQ), F32),
        ],
        compiler_params=_params("arbitrary", "arbitrary"),
        name="nsa_select",
    )(qt, kcmp, vcmpt, gt, slc_map_t)


def _nsa_kernel(qt_ref, ocmp_ref, sel_ref, kaug_ref, vaug_ref, gt_ref, tri_ref,
                out_ref, rhs_ref, s_ref, m_ref, acc_ref):
    tq = qt_ref.shape[2]
    i = pl.program_id(1)
    row_of = lambda h: slice(h * HEAD_DIM, (h + 1) * HEAD_DIM)
    heads_of = lambda g: range(g * Q_PER_KV, (g + 1) * Q_PER_KV)

    def issue_scores(slot, br, tile):
        for g in range(KV_GROUPS):
            k_t = kaug_ref[0, g, br, pl.ds(pl.multiple_of(tile * TK, TK), TK), :]
            for h in heads_of(g):
                s_ref[slot, h] = _dot(k_t, rhs_ref[br, h]).astype(s_ref.dtype)

    def softmax_step(slot, br, tile, bias_idx=None, first=False):
        for g in range(KV_GROUPS):
            v_t = vaug_ref[0, g, br, :, pl.ds(pl.multiple_of(tile * TK, TK), TK)]
            for h in heads_of(g):
                s = s_ref[slot, h]
                if bias_idx is not None:
                    s = s + tri_ref[bias_idx]
                if first:
                    m_new = jnp.max(s, axis=0, keepdims=True).astype(F32)
                    acc_ref[h] = _dot(v_t, jnp.exp2(s - m_new.astype(s.dtype)))
                else:
                    m_old = m_ref[h]
                    m_new = jnp.maximum(m_old, jnp.max(s, axis=0, keepdims=True).astype(F32))
                    p = jnp.exp2(s - m_new.astype(s.dtype))
                    acc_ref[h] = jnp.exp2(m_old - m_new) * acc_ref[h] + _dot(v_t, p)
                m_ref[h] = m_new

    def gate_row(branch, h):
        row = (h // Q_PER_KV) * GATE_ROWS + branch * Q_PER_KV + h % Q_PER_KV
        return gt_ref[0, row:row + 1, :]

    def flush_state(branch):
        for h in range(ATT_HEADS):
            acc = acc_ref[h]
            o = acc[0:HEAD_DIM, :] * (1.0 / jnp.maximum(acc[HEAD_DIM:HEAD_DIM + 1, :], 1e-30))
            out_ref[0, row_of(h), :] = out_ref[0, row_of(h), :] + gate_row(branch, h) * o

    zeros_half = jnp.zeros((K_AUG - HEAD_DIM, tq), BF16)
    for h in range(ATT_HEADS):
        rhs_ref[WIN, h, 0:HEAD_DIM, :] = qt_ref[0, row_of(h), :]
        rhs_ref[WIN, h, HEAD_DIM:K_AUG, :] = zeros_half
        rhs_ref[SEL, h, 0:HEAD_DIM, :] = qt_ref[0, row_of(h), :]
        rhs_ref[SEL, h, HEAD_DIM:K_AUG, :] = sel_ref[0, h // Q_PER_KV]
    n_win = WINDOW // TK + 1
    win_tiles = [i - (n_win - 1) + k for k in range(n_win)]
    def window_step(k):
        exists = win_tiles[k] >= 0
        bias_idx = _BIAS_CAUSAL if k == n_win - 1 else jnp.where(exists, _BIAS_WINDOW_START if k == 0 else _BIAS_NONE, _BIAS_MASKED)
        softmax_step(_SLOT_WIN + k, WIN, jnp.maximum(win_tiles[k], 0), bias_idx, first=(k == 0))

    for k in range(n_win):
        issue_scores(_SLOT_WIN + k, WIN, jnp.maximum(win_tiles[k], 0))
    issue_scores(0, SEL, 0)
    for h in range(ATT_HEADS):
        out_ref[0, row_of(h), :] = ocmp_ref[0, row_of(h), :]
    for k in range(n_win):
        window_step(k)
    flush_state(2)

    for h in range(ATT_HEADS):
        m_ref[h] = jnp.full((1, tq), NEG, F32)
        acc_ref[h] = jnp.zeros((V_AUG, tq), F32)

    def pair_steps(a):
        issue_scores(1, SEL, a + 1)
        softmax_step(0, SEL, a)
        issue_scores(0, SEL, a + 2)
        softmax_step(1, SEL, a + 1)

    def body(quad, carry):
        pair_steps(4 * quad)
        pair_steps(4 * quad + 2)
        return carry

    n_plain = jnp.right_shift(i, 1)
    n_quad = jnp.right_shift(n_plain, 1)
    lax.fori_loop(0, n_quad, body, 0)

    @pl.when(n_plain != 2 * n_quad)
    def _():
        pair_steps(4 * n_quad)

    @pl.when(i == 2 * n_plain)
    def _():
        softmax_step(0, SEL, i, _BIAS_CAUSAL)

    @pl.when(i != 2 * n_plain)
    def _():
        issue_scores(1, SEL, i)
        softmax_step(0, SEL, i - 1)
        softmax_step(1, SEL, i, _BIAS_CAUSAL)

    flush_state(1)


def _nsa(qt, ocmp, sel, kaug, vaug, gt, tri):
    b, _, s = qt.shape
    tq = min(TQ, s)
    assert tq == TK and s % tq == 0 and WINDOW % TK == 0 and _SLOT_WIN + WINDOW // TK + 1 <= _N_SLOTS
    per_batch = lambda shape: pl.BlockSpec((1,) + shape, lambda i, k: (i,) + (0,) * len(shape))
    return pl.pallas_call(
        _nsa_kernel,
        grid=(b, s // tq),
        in_specs=[
            pl.BlockSpec((1, ATT_WIDTH, tq), lambda i, k: (i, 0, k)),
            pl.BlockSpec((1, ATT_WIDTH, tq), lambda i, k: (i, 0, k)),
            pl.BlockSpec((1, KV_GROUPS, K_AUG - HEAD_DIM, tq), lambda i, k: (i, 0, 0, k)),
            per_batch((KV_GROUPS, 2, s, K_AUG)),
            per_batch((KV_GROUPS, 2, V_AUG, s)),
            pl.BlockSpec((1, KV_GROUPS * GATE_ROWS, tq), lambda i, k: (i, 0, k)),
            _whole((4, TK, tq)),
        ],
        out_specs=pl.BlockSpec((1, ATT_WIDTH, tq), lambda i, k: (i, 0, k)),
        out_shape=jax.ShapeDtypeStruct((b, ATT_WIDTH, s), F32),
        scratch_shapes=[
            pltpu.VMEM((2, ATT_HEADS, K_AUG, tq), BF16),
            pltpu.VMEM((_N_SLOTS, ATT_HEADS, TK, tq), BF16),
            pltpu.VMEM((ATT_HEADS, 1, tq), F32),
            pltpu.VMEM((ATT_HEADS, V_AUG, tq), F32),
        ],
        compiler_params=_params("arbitrary", "arbitrary"),
        name="nsa_attention",
    )(qt, ocmp, sel, kaug, vaug, gt, tri)


def _mem_kv_kernel(mem_ref, g_ref, wk_ref, wvt_ref, kx_ref, vxt_ref):
    mn = _rms_rows(mem_ref[0], g_ref[...]).astype(BF16)
    kx_ref[0] = _dot(mn, wk_ref[...]).astype(kx_ref.dtype)
    vxt_ref[0] = _dot_nt(wvt_ref[...], mn).astype(vxt_ref.dtype)


def _mem_kv(mem, g_mem, w_k, w_vt):
    b, m, _ = mem.shape
    return pl.pallas_call(
        _mem_kv_kernel,
        grid=(b,),
        in_specs=[pl.BlockSpec((1, m, D_MODEL), lambda i: (i, 0, 0)), _whole((1, D_MODEL)),
                  _whole((D_MODEL, D_MODEL)), _whole((D_MODEL, D_MODEL))],
        out_specs=(pl.BlockSpec((1, m, D_MODEL), lambda i: (i, 0, 0)),
                   pl.BlockSpec((1, D_MODEL, m), lambda i: (i, 0, 0))),
        out_shape=(jax.ShapeDtypeStruct((b, m, D_MODEL), BF16),
                   jax.ShapeDtypeStruct((b, D_MODEL, m), BF16)),
        compiler_params=_params("arbitrary"),
        name="mem_kv",
    )(mem, g_mem, w_k, w_vt)


def _mix_xattn_kernel(x_ref, yconv_ref, yatt_ref, gatt_ref, woa_ref, wob_ref, gpost_ref, gxpre_ref,
                      wqt_ref, kx_ref, vxt_ref, wo_ref, gxpost_ref, out_ref):
    subs = [slice(c0, c0 + SUB_ROWS) for c0 in range(0, x_ref.shape[1], SUB_ROWS)]
    head = lambda h: slice(h * X_HEAD_DIM, (h + 1) * X_HEAD_DIM)

    def yatt_normed(rows):
        ya = yatt_ref[0, :, rows]
        return (ya * lax.rsqrt(jnp.mean(ya * ya, axis=0, keepdims=True) + EPS) * gatt_ref[...]).astype(BF16)

    yan = [yatt_normed(rows) for rows in subs]
    mixed = [_dot(yconv_ref[0, rows, :], woa_ref[...]) + _dot_tn(y, wob_ref[...]) for rows, y in zip(subs, yan)]
    h1 = [x_ref[0, rows, :] + _rms_rows(m, gpost_ref[...]) for rows, m in zip(subs, mixed)]
    hn = [_rms_rows(h, gxpre_ref[...]).astype(BF16) for h in h1]
    qxt = [(_dot_nt(wqt_ref[...], h) * (X_HEAD_DIM ** -0.5 * math.log2(math.e))).astype(BF16) for h in hn]
    scores = [[_dot(kx_ref[0, :, head(h)], q[head(h), :]) for h in range(X_HEADS)] for q in qxt]

    def attend(sc):
        heads = []
        for h in range(X_HEADS):
            e = jnp.exp2(sc[h] - jnp.max(sc[h], axis=0, keepdims=True))
            o = _dot(vxt_ref[0, head(h), :], e.astype(BF16))
            heads.append(o * (1.0 / jnp.sum(e, axis=0, keepdims=True)))
        return jnp.concatenate(heads, axis=0).astype(BF16)

    oxt = [attend(sc) for sc in scores]
    proj = [_dot_tn(o, wo_ref[...]) for o in oxt]
    for rows, h, p in zip(subs, h1, proj):
        out_ref[0, rows, :] = h + _rms_rows(p, gxpost_ref[...])


def _mix_xattn(x, yconv, yatt_t, g_att, w_out_a, w_out_b, g_post, g_xpre, w_qt, kx, vxt, w_o, g_xpost):
    b, s, _ = x.shape
    m = kx.shape[1]
    tm = min(TM_MIX, s)
    return pl.pallas_call(
        _mix_xattn_kernel,
        grid=(b, s // tm),
        in_specs=[
            pl.BlockSpec((1, tm, D_MODEL), lambda i, j: (i, j, 0)),
            pl.BlockSpec((1, tm, CONV_WIDTH), lambda i, j: (i, j, 0)),
            pl.BlockSpec((1, ATT_WIDTH, tm), lambda i, j: (i, 0, j)),
            _whole((ATT_WIDTH, 1)),
            _whole((CONV_WIDTH, D_MODEL)), _whole((ATT_WIDTH, D_MODEL)),
            _whole((1, D_MODEL)), _whole((1, D_MODEL)),
            _whole((D_MODEL, D_MODEL)),
            pl.BlockSpec((1, m, D_MODEL), lambda i, j: (i, 0, 0)),
            pl.BlockSpec((1, D_MODEL, m), lambda i, j: (i, 0, 0)),
            _whole((D_MODEL, D_MODEL)), _whole((1, D_MODEL)),
        ],
        out_specs=pl.BlockSpec((1, tm, D_MODEL), lambda i, j: (i, j, 0)),
        out_shape=jax.ShapeDtypeStruct((b, s, D_MODEL), F32),
        compiler_params=_params("arbitrary", "arbitrary"),
        name="mix_xattn",
    )(x, yconv, yatt_t, g_att, w_out_a, w_out_b, g_post, g_xpre, w_qt, kx, vxt, w_o, g_xpost)


def _ffn_kernel(h_ref, gpre_ref, wgu_ref, wdown_ref, gpost_ref, out_ref):
    subs = [slice(c0, c0 + TM_FFN_SUB) for c0 in range(0, h_ref.shape[0], TM_FFN_SUB)]
    hn = [_rms_rows(h_ref[rows, :], gpre_ref[...]).astype(BF16) for rows in subs]
    gu = [_dot(v, wgu_ref[...]) for v in hn]
    act = [(v[:, 0:D_FF] * jax.nn.sigmoid(v[:, 0:D_FF]) * v[:, D_FF:2 * D_FF]).astype(BF16) for v in gu]
    down = [_dot(a, wdown_ref[...]) for a in act]
    for rows, d in zip(subs, down):
        out_ref[rows, :] = h_ref[rows, :] + _rms_rows(d, gpost_ref[...])


def _ffn(h, g_pre, w_gu, w_down, g_post):
    n, _ = h.shape
    tm = min(TM_FFN, n)
    return pl.pallas_call(
        _ffn_kernel,
        grid=(n // tm,),
        in_specs=[pl.BlockSpec((tm, D_MODEL), lambda i: (i, 0)), _whole((1, D_MODEL)),
                  _whole((D_MODEL, 2 * D_FF)), _whole((D_FF, D_MODEL)), _whole((1, D_MODEL))],
        out_specs=pl.BlockSpec((tm, D_MODEL), lambda i: (i, 0)),
        out_shape=jax.ShapeDtypeStruct((n, D_MODEL), F32),
        compiler_params=_params("arbitrary"),
        name="ffn",
    )(h, g_pre, w_gu, w_down, g_post)


def _slc_map_t(s):
    n_slab = s // CMP_STRIDE
    ci = np.arange(n_slab)[None, :] * CMP_STRIDE
    sj = np.arange(s // SLC_BLOCK)[:, None] * SLC_BLOCK
    ov = np.clip(np.minimum(ci + CMP_BLOCK, sj + SLC_BLOCK) - np.maximum(ci, sj), 0, None)
    ov[:, n_slab - 1] = 0
    return jnp.asarray(ov / CMP_BLOCK, dtype=F32)


def _tile_biases(tq):
    key = np.arange(TK)[:, None]
    qry = np.arange(tq)[None, :]
    none = np.zeros((TK, tq), np.float32)
    causal = np.where(key <= qry, 0.0, NEG).astype(np.float32)
    window_start = np.where(key > qry, 0.0, NEG).astype(np.float32)
    masked = np.full((TK, tq), NEG, np.float32)
    return jnp.asarray(np.stack([none, causal, window_start, masked]), dtype=BF16)


def kernel(x, mem, positions, norm_mix_pre, w_in, conv_w, pe_kc, w1_kc, w2_kc, pe_vc, w1_vc, w2_vc,
           norm_conv_out, norm_attn_out, w_out, norm_mix_post, norm_x_pre, norm_mem, w_q_x, w_kv_x,
           w_o_x, norm_x_post, norm_ffn_pre, w_gate_up, w_down, norm_ffn_post):
    b, s, _ = x.shape
    row = lambda v: v.reshape(1, -1).astype(F32)

    cuts = np.cumsum([0, CONV_WIDTH, CONV_WIDTH, CONV_WIDTH, ATT_WIDTH] + [KV_WIDTH] * 6 + [3 * ATT_HEADS])
    w_in = w_in.astype(BF16)
    col = lambda k: w_in[:, cuts[k]:cuts[k + 1]]
    w_row = jnp.concatenate([col(0), col(1), col(2), col(4), col(5)], axis=1)
    gate_cols = np.zeros((KV_GROUPS * GATE_ROWS,), np.int32)
    gate_live = np.zeros((KV_GROUPS * GATE_ROWS,), np.float32)
    for g in range(KV_GROUPS):
        for c in range(3):
            for r in range(Q_PER_KV):
                gate_cols[g * GATE_ROWS + c * Q_PER_KV + r] = (g * Q_PER_KV + r) * 3 + c
                gate_live[g * GATE_ROWS + c * Q_PER_KV + r] = 1.0
    w_gate = col(10)[:, gate_cols] * jnp.asarray(gate_live, BF16)[None, :]
    w_t = jnp.concatenate([col(3), col(8), col(6), col(9), col(7), w_gate], axis=1).T

    inv = ROPE_THETA ** (-jnp.arange(HALF, dtype=F32) / HALF)
    yconv, kvc, qt, kaug, vaug, gt = _in_proj(
        x, positions.reshape(b, 1, s), inv.reshape(HALF, 1), row(norm_mix_pre), w_row, w_t,
        conv_w.astype(F32), row(norm_conv_out))

    n_slab = s // CMP_STRIDE
    pos_c = positions[:, np.minimum(np.arange(n_slab) * CMP_STRIDE + CMP_BLOCK - 1, s - 1)]
    kcmp, vcmpt = _compress(
        kvc, pos_c.reshape(b, n_slab, 1), inv.reshape(1, HALF),
        pe_kc.reshape(1, -1), w1_kc.astype(BF16), w2_kc.astype(BF16),
        pe_vc.reshape(1, -1), w1_vc.astype(BF16), w2_vc.astype(BF16).T)

    ocmp, sel = _select(qt, kcmp, vcmpt, gt, _slc_map_t(s))
    yatt_t = _nsa(qt, ocmp, sel, kaug, vaug, gt, _tile_biases(min(TQ, s)))

    w_kv_x = w_kv_x.astype(BF16)
    kx, vxt = _mem_kv(mem, row(norm_mem), w_kv_x[:, :D_MODEL], w_kv_x[:, D_MODEL:].T)
    h2 = _mix_xattn(
        x, yconv, yatt_t, norm_attn_out.reshape(-1, 1).astype(F32),
        w_out[:CONV_WIDTH].astype(BF16), w_out[CONV_WIDTH:].astype(BF16),
        row(norm_mix_post), row(norm_x_pre), w_q_x.astype(BF16).T, kx, vxt, w_o_x.astype(BF16), row(norm_x_post))

    out = _ffn(h2.reshape(b * s, D_MODEL), row(norm_ffn_pre), w_gate_up.astype(BF16), w_down.astype(BF16),
               row(norm_ffn_post))
    return out.reshape(b, s, D_MODEL)
```

```python
import math

import numpy as np
import jax
import jax.numpy as jnp
from jax import lax
from jax.experimental import pallas as pl
from jax.experimental.pallas import tpu as pltpu

D_MODEL = 1024
CONV_WIDTH = 512
CONV_K = 3
ATT_HEADS = 8
HEAD_DIM = 64
HALF = HEAD_DIM // 2
ATT_WIDTH = ATT_HEADS * HEAD_DIM
KV_GROUPS = 2
Q_PER_KV = ATT_HEADS // KV_GROUPS
KV_WIDTH = KV_GROUPS * HEAD_DIM
CMP_BLOCK = 32
CMP_STRIDE = 16
CMP_HIDDEN = 256
SLC_BLOCK = 64
N_SELECT = 16
WINDOW = 512
ROPE_THETA = 10000.0
X_HEADS = 4
X_HEAD_DIM = D_MODEL // X_HEADS
D_FF = 2816
EPS = 1e-6
FORCE = 1e4
NEG = -1e30
GATE_ROWS = 16
K_AUG = 2 * HEAD_DIM
V_AUG = HEAD_DIM + 16
WIN, SEL = 0, 1
_RANK_STEP = 8
_SLOT_WIN, _N_SLOTS = 2, 5
_BIAS_NONE, _BIAS_CAUSAL, _BIAS_WINDOW_START, _BIAS_MASKED = 0, 1, 2, 3

TM_PROJ = 1024
TM_MIX = 1024
TM_FFN = 1024
TM_FFN_SUB = 256
SUB_ROWS = 512
TQ = 256
TQ_SELECT = 512
TK = 256
VMEM_LIMIT = 56 * 1024 * 1024

F32 = jnp.float32
BF16 = jnp.bfloat16

_NT = (((1,), (1,)), ((), ()))
_TN = (((0,), (0,)), ((), ()))


def _dot(a, b):
    return jnp.dot(a, b, preferred_element_type=F32)


def _dot_nt(a, b):
    return lax.dot_general(a, b, _NT, preferred_element_type=F32)


def _dot_tn(a, b):
    return lax.dot_general(a, b, _TN, preferred_element_type=F32)


def _rms_rows(x, g):
    return x * lax.rsqrt(jnp.mean(x * x, axis=-1, keepdims=True) + EPS) * g


def _params(*sem):
    return pltpu.CompilerParams(dimension_semantics=sem, vmem_limit_bytes=VMEM_LIMIT)


def _whole(shape):
    nd = len(shape)
    return pl.BlockSpec(shape, lambda *_: (0,) * nd, pipeline_mode=pl.Buffered(1))


def _in_proj_kernel(x_ref, pos_ref, inv_ref, g_ref, wrow_ref, wt_ref, convw_ref, gconv_ref,
                    yconv_ref, kvc_ref, qt_ref, kaug_ref, vaug_ref, gt_ref, carry_ref):
    @pl.when(pl.program_id(1) == 0)
    def _():
        carry_ref[...] = jnp.zeros_like(carry_ref)

    tm = x_ref.shape[1]
    subs = [slice(c0, c0 + SUB_ROWS) for c0 in range(0, tm, SUB_ROWS)]
    u = [_rms_rows(x_ref[0, rows, :], g_ref[...]).astype(BF16) for rows in subs]
    prow = [_dot(v, wrow_ref[...]) for v in u]
    pt = [_dot_nt(wt_ref[...], v) for v in u]
    for rows, pr, ptr in zip(subs, prow, pt):
        _in_proj_epilogue(rows, pl.program_id(1) * tm + rows.start, pr, ptr, pos_ref, inv_ref, convw_ref, gconv_ref,
                          yconv_ref, kvc_ref, qt_ref, kaug_ref, vaug_ref, gt_ref, carry_ref)


def _in_proj_epilogue(rows, tok0, prow, pt, pos_ref, inv_ref, convw_ref, gconv_ref,
                      yconv_ref, kvc_ref, qt_ref, kaug_ref, vaug_ref, gt_ref, carry_ref):
    n = rows.stop - rows.start

    bg = prow[:, 0:CONV_WIDTH]
    z = prow[:, CONV_WIDTH:2 * CONV_WIDTH] * prow[:, 2 * CONV_WIDTH:3 * CONV_WIDTH]
    prev = carry_ref[...]
    row = lax.broadcasted_iota(jnp.int32, (n, CONV_WIDTH), 0)
    z1 = jnp.where(row == 0, prev[7:8, :], pltpu.roll(z, 1, axis=0))
    z2 = jnp.where(row == 0, prev[6:7, :], jnp.where(row == 1, prev[7:8, :], pltpu.roll(z, 2, axis=0)))
    carry_ref[...] = z[n - 8:n, :]
    cw = convw_ref[...]
    yc = bg * (cw[0:1, :] * z2 + cw[1:2, :] * z1 + cw[2:3, :] * z)
    yconv_ref[0, rows, :] = _rms_rows(yc, gconv_ref[...]).astype(yconv_ref.dtype)

    for c in range(2 * KV_GROUPS):
        lo = 3 * CONV_WIDTH + c * HEAD_DIM
        kvc_ref[0, c, rows, :] = prow[:, lo:lo + HEAD_DIM]

    ang = pos_ref[0, :, rows].astype(F32) * inv_ref[...]
    cos, sin = jnp.cos(ang), jnp.sin(ang)

    def rope_t(base):
        x1 = pt[base:base + HALF, :]
        x2 = pt[base + HALF:base + HEAD_DIM, :]
        return x1 * cos - x2 * sin, x2 * cos + x1 * sin

    scale = HEAD_DIM ** -0.5 * math.log2(math.e)
    for h in range(ATT_HEADS):
        r1, r2 = rope_t(h * HEAD_DIM)
        qt_ref[0, h * HEAD_DIM:h * HEAD_DIM + HALF, rows] = (r1 * scale).astype(qt_ref.dtype)
        qt_ref[0, h * HEAD_DIM + HALF:(h + 1) * HEAD_DIM, rows] = (r2 * scale).astype(qt_ref.dtype)

    tok = tok0 + lax.broadcasted_iota(jnp.int32, (n, HEAD_DIM), 0)
    blk_col = lax.broadcasted_iota(jnp.int32, (n, HEAD_DIM), 1)
    onehot = jnp.where(jnp.right_shift(tok, SLC_BLOCK.bit_length() - 1) == blk_col, 1.0, 0.0)
    for br in (WIN, SEL):
        base = ATT_WIDTH + br * KV_WIDTH
        parts = []
        for g in range(KV_GROUPS):
            parts.extend(rope_t(base + g * HEAD_DIM))
        k_rows = jnp.concatenate(parts, axis=0).T
        extra = onehot if br == SEL else jnp.zeros_like(onehot)
        for g in range(KV_GROUPS):
            k_g = k_rows[:, g * HEAD_DIM:(g + 1) * HEAD_DIM]
            kaug_ref[0, g, br, rows, :] = jnp.concatenate([k_g, extra], axis=-1).astype(kaug_ref.dtype)

    v0 = ATT_WIDTH + 2 * KV_WIDTH
    ones_rows = jnp.where(lax.broadcasted_iota(jnp.int32, (V_AUG - HEAD_DIM, n), 0) == 0, 1.0, 0.0)
    for br in (WIN, SEL):
        for g in range(KV_GROUPS):
            lo = v0 + br * KV_WIDTH + g * HEAD_DIM
            vaug_ref[0, g, br, :, rows] = jnp.concatenate([pt[lo:lo + HEAD_DIM, :], ones_rows], axis=0).astype(vaug_ref.dtype)

    g0 = v0 + 2 * KV_WIDTH
    gt_ref[0, :, rows] = jax.nn.sigmoid(pt[g0:g0 + KV_GROUPS * GATE_ROWS, :])


def _in_proj(x, pos_row, inv_col, g_pre, w_row, w_t, conv_w, g_conv):
    b, s, _ = x.shape
    tm = min(TM_PROJ, s)
    n_row, n_t = w_row.shape[1], w_t.shape[0]
    out_shape = (
        jax.ShapeDtypeStruct((b, s, CONV_WIDTH), BF16),
        jax.ShapeDtypeStruct((b, 2 * KV_GROUPS, s, HEAD_DIM), F32),
        jax.ShapeDtypeStruct((b, ATT_WIDTH, s), BF16),
        jax.ShapeDtypeStruct((b, KV_GROUPS, 2, s, K_AUG), BF16),
        jax.ShapeDtypeStruct((b, KV_GROUPS, 2, V_AUG, s), BF16),
        jax.ShapeDtypeStruct((b, KV_GROUPS * GATE_ROWS, s), F32),
    )
    return pl.pallas_call(
        _in_proj_kernel,
        grid=(b, s // tm),
        in_specs=[
            pl.BlockSpec((1, tm, D_MODEL), lambda i, j: (i, j, 0)),
            pl.BlockSpec((1, 1, tm), lambda i, j: (i, 0, j)),
            _whole((HALF, 1)),
            _whole((1, D_MODEL)),
            _whole((D_MODEL, n_row)),
            _whole((n_t, D_MODEL)),
            _whole((CONV_K, CONV_WIDTH)),
            _whole((1, CONV_WIDTH)),
        ],
        out_specs=(
            pl.BlockSpec((1, tm, CONV_WIDTH), lambda i, j: (i, j, 0)),
            pl.BlockSpec((1, 2 * KV_GROUPS, tm, HEAD_DIM), lambda i, j: (i, 0, j, 0)),
            pl.BlockSpec((1, ATT_WIDTH, tm), lambda i, j: (i, 0, j)),
            pl.BlockSpec((1, KV_GROUPS, 2, tm, K_AUG), lambda i, j: (i, 0, 0, j, 0)),
            pl.BlockSpec((1, KV_GROUPS, 2, V_AUG, tm), lambda i, j: (i, 0, 0, 0, j)),
            pl.BlockSpec((1, KV_GROUPS * GATE_ROWS, tm), lambda i, j: (i, 0, j)),
        ),
        out_shape=out_shape,
        scratch_shapes=[pltpu.VMEM((8, CONV_WIDTH), F32)],
        compiler_params=_params("arbitrary", "arbitrary"),
        name="in_proj",
    )(x, pos_row, inv_col, g_pre, w_row, w_t, conv_w, g_conv)


def _compress_kernel(rk_ref, rv_ref, pos_ref, inv_ref, pek_ref, w1k_ref, w2k_ref, pev_ref, w1v_ref, w2vt_ref,
                     kc_ref, vct_ref):
    half_in = CMP_STRIDE * HEAD_DIM
    n_slab = kc_ref.shape[2]

    sides = ((rk_ref, pek_ref, w1k_ref), (rv_ref, pev_ref, w1v_ref))
    a = [jnp.zeros((n_slab, CMP_HIDDEN), F32) for _ in sides]
    b = [jnp.zeros((n_slab, CMP_HIDDEN), F32) for _ in sides]
    per_dot = 256 // HEAD_DIM
    for j0 in range(0, CMP_STRIDE, per_dot):
        lo = j0 * HEAD_DIM
        xcat = [jnp.concatenate([x_ref[0, 0, pl.ds(j, n_slab, stride=CMP_STRIDE), :]
                                 for j in range(j0, j0 + per_dot)], axis=-1).astype(BF16)
                for x_ref, _, _ in sides]
        for n, (_, _, w1_ref) in enumerate(sides):
            a[n] = a[n] + _dot(xcat[n], w1_ref[lo:lo + per_dot * HEAD_DIM, :])
            b[n] = b[n] + _dot(xcat[n], w1_ref[half_in + lo:half_in + lo + per_dot * HEAD_DIM, :])
    hidden = []
    for n, (_, pe_ref, w1_ref) in enumerate(sides):
        pe = jnp.broadcast_to(pe_ref[...], (8, 2 * half_in)).astype(BF16)
        h = a[n] + pltpu.roll(b[n], n_slab - 1, axis=0) + _dot(pe, w1_ref[...])[0:1, :]
        hidden.append((h * jax.nn.sigmoid(h)).astype(BF16))

    kc = _dot(hidden[0], w2k_ref[...])
    ang = pos_ref[0].astype(F32) * inv_ref[...]
    cos, sin = jnp.cos(ang), jnp.sin(ang)
    x1, x2 = kc[:, 0:HALF], kc[:, HALF:HEAD_DIM]
    kc_ref[0, 0] = jnp.concatenate([x1 * cos - x2 * sin, x2 * cos + x1 * sin], axis=-1).astype(kc_ref.dtype)
    vct_ref[0, 0] = _dot_nt(w2vt_ref[...], hidden[1]).astype(vct_ref.dtype)


def _compress(kvc, pos_c, inv_row, pek, w1k, w2k, pev, w1v, w2vt):
    b, _, s, _ = kvc.shape
    g, n_slab = KV_GROUPS, s // CMP_STRIDE
    return pl.pallas_call(
        _compress_kernel,
        grid=(b, g),
        in_specs=[
            pl.BlockSpec((1, 1, s, HEAD_DIM), lambda i, j: (i, j, 0, 0)),
            pl.BlockSpec((1, 1, s, HEAD_DIM), lambda i, j: (i, KV_GROUPS + j, 0, 0)),
            pl.BlockSpec((1, n_slab, 1), lambda i, j: (i, 0, 0)),
            _whole((1, HALF)),
            _whole(pek.shape), _whole(w1k.shape), _whole(w2k.shape),
            _whole(pev.shape), _whole(w1v.shape), _whole(w2vt.shape),
        ],
        out_specs=(
            pl.BlockSpec((1, 1, n_slab, HEAD_DIM), lambda i, j: (i, j, 0, 0)),
            pl.BlockSpec((1, 1, HEAD_DIM, n_slab), lambda i, j: (i, j, 0, 0)),
        ),
        out_shape=(
            jax.ShapeDtypeStruct((b, g, n_slab, HEAD_DIM), BF16),
            jax.ShapeDtypeStruct((b, g, HEAD_DIM, n_slab), BF16),
        ),
        compiler_params=_params("arbitrary", "arbitrary"),
        name="compress",
    )(kvc, kvc, pos_c, inv_row, pek, w1k, w2k, pev, w1v, w2vt)


def _select_kernel(qt_ref, kc_ref, vct_ref, gt_ref, map_ref, ocmp_ref, sel_ref, cs_ref, imp_ref, psum_ref):
    n_cmp = kc_ref.shape[2]
    n_slc = map_ref.shape[0]
    step = pl.program_id(1)
    n_sub = qt_ref.shape[2] // TQ
    subs = [slice(c * TQ, (c + 1) * TQ) for c in range(n_sub)]
    row_of = lambda h: slice(h * HEAD_DIM, (h + 1) * HEAD_DIM)
    heads_of = lambda g: range(g * Q_PER_KV, (g + 1) * Q_PER_KV)
    t_of = [(step * n_sub + c) * TQ + lax.broadcasted_iota(jnp.int32, (1, TQ), 1) for c in range(n_sub)]

    def gate_row(h):
        row = (h // Q_PER_KV) * GATE_ROWS + h % Q_PER_KV
        return gt_ref[0, row:row + 1, :]

    def compressed_branch(n_live):
        rows = min(n_live * (SLC_BLOCK // CMP_STRIDE), n_cmp)
        for c, lanes in enumerate(subs):
            for g in range(KV_GROUPS):
                kc = kc_ref[0, g, 0:rows, :]
                for h in heads_of(g):
                    cs_ref[c, h, 0:rows, :] = _dot(kc, qt_ref[0, row_of(h), lanes])

        cmp_end = lax.broadcasted_iota(jnp.int32, (rows, TQ), 0) * CMP_STRIDE + (CMP_BLOCK - 1)
        for c, lanes in enumerate(subs):
            cbias = jnp.where(cmp_end <= t_of[c], 0.0, NEG)
            has_cmp = t_of[c] >= CMP_BLOCK - 1
            for g in range(KV_GROUPS):
                vct = vct_ref[0, g, :, 0:rows]
                acc = jnp.zeros((rows, TQ), F32)
                for h in heads_of(g):
                    s = cs_ref[c, h, 0:rows, :] + cbias
                    e = jnp.exp2(s - jnp.max(s, axis=0, keepdims=True))
                    inv = jnp.where(has_cmp, 1.0 / jnp.maximum(jnp.sum(e, axis=0, keepdims=True), 1e-30), 0.0)
                    acc = acc + e * inv
                    ocmp_ref[0, row_of(h), lanes] = (gate_row(h)[:, lanes] * inv) * _dot(vct, e.astype(BF16))
                for lt in range(TQ // 128):
                    psum_ref[c, g, lt, 0:rows, :] = acc[:, lt * 128:(lt + 1) * 128]

        blk = lax.broadcasted_iota(jnp.int32, (n_live, TQ), 0)
        for c in range(n_sub):
            cur = jnp.right_shift(t_of[c], SLC_BLOCK.bit_length() - 1)
            forced = (blk == 0) | (blk == cur) | (blk == cur - 1)
            for g in range(KV_GROUPS):
                every4 = lambda k: jnp.concatenate(
                    [psum_ref[c, g, lt, pl.ds(k, n_live, stride=SLC_BLOCK // CMP_STRIDE), :] for lt in range(TQ // 128)],
                    axis=-1)
                before = jnp.where(blk == 0, 0.0, pltpu.roll(every4(3), 1, axis=0))
                imp = every4(0) + every4(1) + every4(2) + 0.5 * (every4(3) + before)
                imp_ref[c, g, 0:n_live, :] = jnp.where(forced, FORCE, jnp.where(blk * SLC_BLOCK > t_of[c], -FORCE, imp))

    def select_blocks(n_live, c, g):
        n_grp = n_live // 8
        imp_g = [imp_ref[c, g, 8 * a:8 * a + 8, :] for a in range(n_grp)]
        rank_g = [jnp.zeros((8, TQ), jnp.int32) for _ in range(n_grp)]
        sub = lax.broadcasted_iota(jnp.int32, (8, TQ), 0)
        for j in range(n_live):
            vj = jnp.broadcast_to(imp_ref[c, g, j:j + 1, :], (8, TQ))
            for a in range(n_grp):
                if 8 * a > j:
                    beats = jnp.where(vj >= imp_g[a], 1, 0)
                elif 8 * a + 7 < j:
                    beats = jnp.where(vj > imp_g[a], 1, 0)
                else:
                    beats = jnp.where(sub > j - 8 * a, jnp.where(vj >= imp_g[a], 1, 0), jnp.where(vj > imp_g[a], 1, 0))
                rank_g[a] = rank_g[a] + beats
        sel = jnp.where(jnp.concatenate(rank_g, axis=0) < min(N_SELECT, n_slc), 0.0, NEG).astype(BF16)
        dead = K_AUG - HEAD_DIM - n_live
        if dead:
            sel = jnp.concatenate([sel, jnp.full((dead, TQ), NEG, BF16)], axis=0)
        sel_ref[0, g, :, subs[c]] = sel

    def select_all(n_live):
        compressed_branch(n_live)
        for c in range(n_sub):
            for g in range(KV_GROUPS):
                select_blocks(n_live, c, g)

    blocks_per_step = qt_ref.shape[2] // SLC_BLOCK
    live_sizes = list(range(_RANK_STEP, n_slc, _RANK_STEP)) + [n_slc]
    for v, n_live in enumerate(live_sizes):
        lo = v * _RANK_STEP // blocks_per_step
        hi = n_live // blocks_per_step if v + 1 < len(live_sizes) else n_slc // blocks_per_step
        pl.when((step >= lo) & (step < hi))(lambda n_live=n_live: select_all(n_live))


def _select(qt, kcmp, vcmpt, gt, slc_map_t):
    b, _, s = qt.shape
    tqs = min(TQ_SELECT, s)
    n_cmp = kcmp.shape[2]
    n_slc = s // SLC_BLOCK
    assert tqs % TQ == 0 and s % tqs == 0 and n_slc % 8 == 0 and n_slc <= K_AUG - HEAD_DIM
    assert _RANK_STEP % 8 == 0 and _RANK_STEP % (tqs // SLC_BLOCK) == 0 and n_slc % (tqs // SLC_BLOCK) == 0
    per_batch = lambda shape: pl.BlockSpec((1,) + shape, lambda i, k: (i,) + (0,) * len(shape))
    return pl.pallas_call(
        _select_kernel,
        grid=(b, s // tqs),
        in_specs=[
            pl.BlockSpec((1, ATT_WIDTH, tqs), lambda i, k: (i, 0, k)),
            per_batch((KV_GROUPS, n_cmp, HEAD_DIM)),
            per_batch((KV_GROUPS, HEAD_DIM, n_cmp)),
            pl.BlockSpec((1, KV_GROUPS * GATE_ROWS, tqs), lambda i, k: (i, 0, k)),
            _whole((n_slc, n_cmp)),
        ],
        out_specs=(
            pl.BlockSpec((1, ATT_WIDTH, tqs), lambda i, k: (i, 0, k)),
            pl.BlockSpec((1, KV_GROUPS, K_AUG - HEAD_DIM, tqs), lambda i, k: (i, 0, 0, k)),
        ),
        out_shape=(
            jax.ShapeDtypeStruct((b, ATT_WIDTH, s), F32),
            jax.ShapeDtypeStruct((b, KV_GROUPS, K_AUG - HEAD_DIM, s), BF16),
        ),
        scratch_shapes=[
            pltpu.VMEM((tqs // TQ, ATT_HEADS, n_cmp, TQ), F32),
            pltpu.VMEM((tqs // TQ, KV_GROUPS, n_slc, TQ), F32),
            pltpu.VMEM((tqs // TQ, KV_GROUPS, TQ // 128, n_cmp, 128), F32),
        ],
        compiler_params=_params("arbitrary", "arbitrary"),
        name="nsa_select",
    )(qt, kcmp, vcmpt, gt, slc_map_t)


def _nsa_kernel(qt_ref, ocmp_ref, sel_ref, kaug_ref, vaug_ref, gt_ref, tri_ref,
                out_ref, rhs_ref, s_ref, m_ref, acc_ref):
    tq = qt_ref.shape[2]
    i = pl.program_id(1)
    row_of = lambda h: slice(h * HEAD_DIM, (h + 1) * HEAD_DIM)
    heads_of = lambda g: range(g * Q_PER_KV, (g + 1) * Q_PER_KV)

    def issue_scores(slot, br, tile):
        for g in range(KV_GROUPS):
            k_t = kaug_ref[0, g, br, pl.ds(pl.multiple_of(tile * TK, TK), TK), :]
            for h in heads_of(g):
                s_ref[slot, h] = _dot(k_t, rhs_ref[br, h]).astype(s_ref.dtype)

    def softmax_step(slot, br, tile, bias_idx=None, first=False):
        for g in range(KV_GROUPS):
            v_t = vaug_ref[0, g, br, :, pl.ds(pl.multiple_of(tile * TK, TK), TK)]
            for h in heads_of(g):
                s = s_ref[slot, h]
                if bias_idx is not None:
                    s = s + tri_ref[bias_idx]
                if first:
                    m_new = jnp.max(s, axis=0, keepdims=True).astype(F32)
                    acc_ref[h] = _dot(v_t, jnp.exp2(s - m_new.astype(s.dtype)))
                else:
                    m_old = m_ref[h]
                    m_new = jnp.maximum(m_old, jnp.max(s, axis=0, keepdims=True).astype(F32))
                    p = jnp.exp2(s - m_new.astype(s.dtype))
                    acc_ref[h] = jnp.exp2(m_old - m_new) * acc_ref[h] + _dot(v_t, p)
                m_ref[h] = m_new

    def gate_row(branch, h):
        row = (h // Q_PER_KV) * GATE_ROWS + branch * Q_PER_KV + h % Q_PER_KV
        return gt_ref[0, row:row + 1, :]

    def flush_state(branch):
        for h in range(ATT_HEADS):
            acc = acc_ref[h]
            o = acc[0:HEAD_DIM, :] * (1.0 / jnp.maximum(acc[HEAD_DIM:HEAD_DIM + 1, :], 1e-30))
            out_ref[0, row_of(h), :] = out_ref[0, row_of(h), :] + gate_row(branch, h) * o

    zeros_half = jnp.zeros((K_AUG - HEAD_DIM, tq), BF16)
    for h in range(ATT_HEADS):
        rhs_ref[WIN, h, 0:HEAD_DIM, :] = qt_ref[0, row_of(h), :]
        rhs_ref[WIN, h, HEAD_DIM:K_AUG, :] = zeros_half
        rhs_ref[SEL, h, 0:HEAD_DIM, :] = qt_ref[0, row_of(h), :]
        rhs_ref[SEL, h, HEAD_DIM:K_AUG, :] = sel_ref[0, h // Q_PER_KV]
    n_win = WINDOW // TK + 1
    win_tiles = [i - (n_win - 1) + k for k in range(n_win)]
    def window_step(k):
        exists = win_tiles[k] >= 0
        bias_idx = _BIAS_CAUSAL if k == n_win - 1 else jnp.where(exists, _BIAS_WINDOW_START if k == 0 else _BIAS_NONE, _BIAS_MASKED)
        softmax_step(_SLOT_WIN + k, WIN, jnp.maximum(win_tiles[k], 0), bias_idx, first=(k == 0))

    for k in range(n_win):
        issue_scores(_SLOT_WIN + k, WIN, jnp.maximum(win_tiles[k], 0))
    issue_scores(0, SEL, 0)
    for h in range(ATT_HEADS):
        out_ref[0, row_of(h), :] = ocmp_ref[0, row_of(h), :]
    for k in range(n_win):
        window_step(k)
    flush_state(2)

    for h in range(ATT_HEADS):
        m_ref[h] = jnp.full((1, tq), NEG, F32)
        acc_ref[h] = jnp.zeros((V_AUG, tq), F32)

    def pair_steps(a):
        issue_scores(1, SEL, a + 1)
        softmax_step(0, SEL, a)
        issue_scores(0, SEL, a + 2)
        softmax_step(1, SEL, a + 1)

    def body(quad, carry):
        pair_steps(4 * quad)
        pair_steps(4 * quad + 2)
        return carry

    n_plain = jnp.right_shift(i, 1)
    n_quad = jnp.right_shift(n_plain, 1)
    lax.fori_loop(0, n_quad, body, 0)

    @pl.when(n_plain != 2 * n_quad)
    def _():
        pair_steps(4 * n_quad)

    @pl.when(i == 2 * n_plain)
    def _():
        softmax_step(0, SEL, i, _BIAS_CAUSAL)

    @pl.when(i != 2 * n_plain)
    def _():
        issue_scores(1, SEL, i)
        softmax_step(0, SEL, i - 1)
        softmax_step(1, SEL, i, _BIAS_CAUSAL)

    flush_state(1)


def _nsa(qt, ocmp, sel, kaug, vaug, gt, tri):
    b, _, s = qt.shape
    tq = min(TQ, s)
    assert tq == TK and s % tq == 0 and WINDOW % TK == 0 and _SLOT_WIN + WINDOW // TK + 1 <= _N_SLOTS
    per_batch = lambda shape: pl.BlockSpec((1,) + shape, lambda i, k: (i,) + (0,) * len(shape))
    return pl.pallas_call(
        _nsa_kernel,
        grid=(b, s // tq),
        in_specs=[
            pl.BlockSpec((1, ATT_WIDTH, tq), lambda i, k: (i, 0, k)),
            pl.BlockSpec((1, ATT_WIDTH, tq), lambda i, k: (i, 0, k)),
            pl.BlockSpec((1, KV_GROUPS, K_AUG - HEAD_DIM, tq), lambda i, k: (i, 0, 0, k)),
            per_batch((KV_GROUPS, 2, s, K_AUG)),
            per_batch((KV_GROUPS, 2, V_AUG, s)),
            pl.BlockSpec((1, KV_GROUPS * GATE_ROWS, tq), lambda i, k: (i, 0, k)),
            _whole((4, TK, tq)),
        ],
        out_specs=pl.BlockSpec((1, ATT_WIDTH, tq), lambda i, k: (i, 0, k)),
        out_shape=jax.ShapeDtypeStruct((b, ATT_WIDTH, s), F32),
        scratch_shapes=[
            pltpu.VMEM((2, ATT_HEADS, K_AUG, tq), BF16),
            pltpu.VMEM((_N_SLOTS, ATT_HEADS, TK, tq), BF16),
            pltpu.VMEM((ATT_HEADS, 1, tq), F32),
            pltpu.VMEM((ATT_HEADS, V_AUG, tq), F32),
        ],
        compiler_params=_params("arbitrary", "arbitrary"),
        name="nsa_attention",
    )(qt, ocmp, sel, kaug, vaug, gt, tri)


def _mem_kv_kernel(mem_ref, g_ref, wk_ref, wvt_ref, kx_ref, vxt_ref):
    mn = _rms_rows(mem_ref[0], g_ref[...]).astype(BF16)
    kx_ref[0] = _dot(mn, wk_ref[...]).astype(kx_ref.dtype)
    vxt_ref[0] = _dot_nt(wvt_ref[...], mn).astype(vxt_ref.dtype)


def _mem_kv(mem, g_mem, w_k, w_vt):
    b, m, _ = mem.shape
    return pl.pallas_call(
        _mem_kv_kernel,
        grid=(b,),
        in_specs=[pl.BlockSpec((1, m, D_MODEL), lambda i: (i, 0, 0)), _whole((1, D_MODEL)),
                  _whole((D_MODEL, D_MODEL)), _whole((D_MODEL, D_MODEL))],
        out_specs=(pl.BlockSpec((1, m, D_MODEL), lambda i: (i, 0, 0)),
                   pl.BlockSpec((1, D_MODEL, m), lambda i: (i, 0, 0))),
        out_shape=(jax.ShapeDtypeStruct((b, m, D_MODEL), BF16),
                   jax.ShapeDtypeStruct((b, D_MODEL, m), BF16)),
        compiler_params=_params("arbitrary"),
        name="mem_kv",
    )(mem, g_mem, w_k, w_vt)


def _mix_xattn_kernel(x_ref, yconv_ref, yatt_ref, gatt_ref, woa_ref, wob_ref, gpost_ref, gxpre_ref,
                      wqt_ref, kx_ref, vxt_ref, wo_ref, gxpost_ref, out_ref):
    subs = [slice(c0, c0 + SUB_ROWS) for c0 in range(0, x_ref.shape[1], SUB_ROWS)]
    head = lambda h: slice(h * X_HEAD_DIM, (h + 1) * X_HEAD_DIM)

    def yatt_normed(rows):
        ya = yatt_ref[0, :, rows]
        return (ya * lax.rsqrt(jnp.mean(ya * ya, axis=0, keepdims=True) + EPS) * gatt_ref[...]).astype(BF16)

    yan = [yatt_normed(rows) for rows in subs]
    mixed = [_dot(yconv_ref[0, rows, :], woa_ref[...]) + _dot_tn(y, wob_ref[...]) for rows, y in zip(subs, yan)]
    h1 = [x_ref[0, rows, :] + _rms_rows(m, gpost_ref[...]) for rows, m in zip(subs, mixed)]
    hn = [_rms_rows(h, gxpre_ref[...]).astype(BF16) for h in h1]
    qxt = [(_dot_nt(wqt_ref[...], h) * (X_HEAD_DIM ** -0.5 * math.log2(math.e))).astype(BF16) for h in hn]
    scores = [[_dot(kx_ref[0, :, head(h)], q[head(h), :]) for h in range(X_HEADS)] for q in qxt]

    def attend(sc):
        heads = []
        for h in range(X_HEADS):
            e = jnp.exp2(sc[h] - jnp.max(sc[h], axis=0, keepdims=True))
            o = _dot(vxt_ref[0, head(h), :], e.astype(BF16))
            heads.append(o * (1.0 / jnp.sum(e, axis=0, keepdims=True)))
        return jnp.concatenate(heads, axis=0).astype(BF16)

    oxt = [attend(sc) for sc in scores]
    proj = [_dot_tn(o, wo_ref[...]) for o in oxt]
    for rows, h, p in zip(subs, h1, proj):
        out_ref[0, rows, :] = h + _rms_rows(p, gxpost_ref[...])


def _mix_xattn(x, yconv, yatt_t, g_att, w_out_a, w_out_b, g_post, g_xpre, w_qt, kx, vxt, w_o, g_xpost):
    b, s, _ = x.shape
    m = kx.shape[1]
    tm = min(TM_MIX, s)
    return pl.pallas_call(
        _mix_xattn_kernel,
        grid=(b, s // tm),
        in_specs=[
            pl.BlockSpec((1, tm, D_MODEL), lambda i, j: (i, j, 0)),
            pl.BlockSpec((1, tm, CONV_WIDTH), lambda i, j: (i, j, 0)),
            pl.BlockSpec((1, ATT_WIDTH, tm), lambda i, j: (i, 0, j)),
            _whole((ATT_WIDTH, 1)),
            _whole((CONV_WIDTH, D_MODEL)), _whole((ATT_WIDTH, D_MODEL)),
            _whole((1, D_MODEL)), _whole((1, D_MODEL)),
            _whole((D_MODEL, D_MODEL)),
            pl.BlockSpec((1, m, D_MODEL), lambda i, j: (i, 0, 0)),
            pl.BlockSpec((1, D_MODEL, m), lambda i, j: (i, 0, 0)),
            _whole((D_MODEL, D_MODEL)), _whole((1, D_MODEL)),
        ],
        out_specs=pl.BlockSpec((1, tm, D_MODEL), lambda i, j: (i, j, 0)),
        out_shape=jax.ShapeDtypeStruct((b, s, D_MODEL), F32),
        compiler_params=_params("arbitrary", "arbitrary"),
        name="mix_xattn",
    )(x, yconv, yatt_t, g_att, w_out_a, w_out_b, g_post, g_xpre, w_qt, kx, vxt, w_o, g_xpost)


def _ffn_kernel(h_ref, gpre_ref, wgu_ref, wdown_ref, gpost_ref, out_ref):
    subs = [slice(c0, c0 + TM_FFN_SUB) for c0 in range(0, h_ref.shape[0], TM_FFN_SUB)]
    hn = [_rms_rows(h_ref[rows, :], gpre_ref[...]).astype(BF16) for rows in subs]
    gu = [_dot(v, wgu_ref[...]) for v in hn]
    act = [(v[:, 0:D_FF] * jax.nn.sigmoid(v[:, 0:D_FF]) * v[:, D_FF:2 * D_FF]).astype(BF16) for v in gu]
    down = [_dot(a, wdown_ref[...]) for a in act]
    for rows, d in zip(subs, down):
        out_ref[rows, :] = h_ref[rows, :] + _rms_rows(d, gpost_ref[...])


def _ffn(h, g_pre, w_gu, w_down, g_post):
    n, _ = h.shape
    tm = min(TM_FFN, n)
    return pl.pallas_call(
        _ffn_kernel,
        grid=(n // tm,),
        in_specs=[pl.BlockSpec((tm, D_MODEL), lambda i: (i, 0)), _whole((1, D_MODEL)),
                  _whole((D_MODEL, 2 * D_FF)), _whole((D_FF, D_MODEL)), _whole((1, D_MODEL))],
        out_specs=pl.BlockSpec((tm, D_MODEL), lambda i: (i, 0)),
        out_shape=jax.ShapeDtypeStruct((n, D_MODEL), F32),
        compiler_params=_params("arbitrary"),
        name="ffn",
    )(h, g_pre, w_gu, w_down, g_post)


def _slc_map_t(s):
    n_slab = s // CMP_STRIDE
    ci = np.arange(n_slab)[None, :] * CMP_STRIDE
    sj = np.arange(s // SLC_BLOCK)[:, None] * SLC_BLOCK
    ov = np.clip(np.minimum(ci + CMP_BLOCK, sj + SLC_BLOCK) - np.maximum(ci, sj), 0, None)
    ov[:, n_slab - 1] = 0
    return jnp.asarray(ov / CMP_BLOCK, dtype=F32)


def _tile_biases(tq):
    key = np.arange(TK)[:, None]
    qry = np.arange(tq)[None, :]
    none = np.zeros((TK, tq), np.float32)
    causal = np.where(key <= qry, 0.0, NEG).astype(np.float32)
    window_start = np.where(key > qry, 0.0, NEG).astype(np.float32)
    masked = np.full((TK, tq), NEG, np.float32)
    return jnp.asarray(np.stack([none, causal, window_start, masked]), dtype=BF16)


def kernel(x, mem, positions, norm_mix_pre, w_in, conv_w, pe_kc, w1_kc, w2_kc, pe_vc, w1_vc, w2_vc,
           norm_conv_out, norm_attn_out, w_out, norm_mix_post, norm_x_pre, norm_mem, w_q_x, w_kv_x,
           w_o_x, norm_x_post, norm_ffn_pre, w_gate_up, w_down, norm_ffn_post):
    b, s, _ = x.shape
    row = lambda v: v.reshape(1, -1).astype(F32)

    cuts = np.cumsum([0, CONV_WIDTH, CONV_WIDTH, CONV_WIDTH, ATT_WIDTH] + [KV_WIDTH] * 6 + [3 * ATT_HEADS])
    w_in = w_in.astype(BF16)
    col = lambda k: w_in[:, cuts[k]:cuts[k + 1]]
    w_row = jnp.concatenate([col(0), col(1), col(2), col(4), col(5)], axis=1)
    gate_cols = np.zeros((KV_GROUPS * GATE_ROWS,), np.int32)
    gate_live = np.zeros((KV_GROUPS * GATE_ROWS,), np.float32)
    for g in range(KV_GROUPS):
        for c in range(3):
            for r in range(Q_PER_KV):
                gate_cols[g * GATE_ROWS + c * Q_PER_KV + r] = (g * Q_PER_KV + r) * 3 + c
                gate_live[g * GATE_ROWS + c * Q_PER_KV + r] = 1.0
    w_gate = col(10)[:, gate_cols] * jnp.asarray(gate_live, BF16)[None, :]
    w_t = jnp.concatenate([col(3), col(8), col(6), col(9), col(7), w_gate], axis=1).T

    inv = ROPE_THETA ** (-jnp.arange(HALF, dtype=F32) / HALF)
    yconv, kvc, qt, kaug, vaug, gt = _in_proj(
        x, positions.reshape(b, 1, s), inv.reshape(HALF, 1), row(norm_mix_pre), w_row, w_t,
        conv_w.astype(F32), row(norm_conv_out))

    n_slab = s // CMP_STRIDE
    pos_c = positions[:, np.minimum(np.arange(n_slab) * CMP_STRIDE + CMP_BLOCK - 1, s - 1)]
    kcmp, vcmpt = _compress(
        kvc, pos_c.reshape(b, n_slab, 1), inv.reshape(1, HALF),
        pe_kc.reshape(1, -1), w1_kc.astype(BF16), w2_kc.astype(BF16),
        pe_vc.reshape(1, -1), w1_vc.astype(BF16), w2_vc.astype(BF16).T)

    ocmp, sel = _select(qt, kcmp, vcmpt, gt, _slc_map_t(s))
    yatt_t = _nsa(qt, ocmp, sel, kaug, vaug, gt, _tile_biases(min(TQ, s)))

    w_kv_x = w_kv_x.astype(BF16)
    kx, vxt = _mem_kv(mem, row(norm_mem), w_kv_x[:, :D_MODEL], w_kv_x[:, D_MODEL:].T)
    h2 = _mix_xattn(
        x, yconv, yatt_t, norm_attn_out.reshape(-1, 1).astype(F32),
        w_out[:CONV_WIDTH].astype(BF16), w_out[CONV_WIDTH:].astype(BF16),
        row(norm_mix_post), row(norm_x_pre), w_q_x.astype(BF16).T, kx, vxt, w_o_x.astype(BF16), row(norm_x_post))

    out = _ffn(h2.reshape(b * s, D_MODEL), row(norm_ffn_pre), w_gate_up.astype(BF16), w_down.astype(BF16),
               row(norm_ffn_post))
    return out.reshape(b, s, D_MODEL)
```

```python
import math

import numpy as np
import jax
import jax.numpy as jnp
from jax import lax
from jax.experimental import pallas as pl
from jax.experimental.pallas import tpu as pltpu

D_MODEL = 1024
CONV_WIDTH = 512
CONV_K = 3
ATT_HEADS = 8
HEAD_DIM = 64
HALF = HEAD_DIM // 2
ATT_WIDTH = ATT_HEADS * HEAD_DIM
KV_GROUPS = 2
Q_PER_KV = ATT_HEADS // KV_GROUPS
KV_WIDTH = KV_GROUPS * HEAD_DIM
CMP_BLOCK = 32
CMP_STRIDE = 16
CMP_HIDDEN = 256
SLC_BLOCK = 64
N_SELECT = 16
WINDOW = 512
ROPE_THETA = 10000.0
X_HEADS = 4
X_HEAD_DIM = D_MODEL // X_HEADS
D_FF = 2816
EPS = 1e-6
FORCE = 1e4
NEG = -1e30
GATE_ROWS = 16
K_AUG = 2 * HEAD_DIM
V_AUG = HEAD_DIM + 16
WIN, SEL = 0, 1
_RANK_STEP = 8
_SLOT_WIN, _N_SLOTS = 2, 5
_BIAS_NONE, _BIAS_CAUSAL, _BIAS_WINDOW_START, _BIAS_MASKED = 0, 1, 2, 3

TM_PROJ = 1024
TM_MIX = 1024
TM_FFN = 1024
TM_FFN_SUB = 256
SUB_ROWS = 512
TQ = 256
TQ_SELECT = 512
TK = 256
VMEM_LIMIT = 56 * 1024 * 1024

F32 = jnp.float32
BF16 = jnp.bfloat16

_NT = (((1,), (1,)), ((), ()))
_TN = (((0,), (0,)), ((), ()))


def _dot(a, b):
    return jnp.dot(a, b, preferred_element_type=F32)


def _dot_nt(a, b):
    return lax.dot_general(a, b, _NT, preferred_element_type=F32)


def _dot_tn(a, b):
    return lax.dot_general(a, b, _TN, preferred_element_type=F32)


def _rms_rows(x, g):
    return x * lax.rsqrt(jnp.mean(x * x, axis=-1, keepdims=True) + EPS) * g


def _params(*sem):
    return pltpu.CompilerParams(dimension_semantics=sem, vmem_limit_bytes=VMEM_LIMIT)


def _whole(shape):
    nd = len(shape)
    return pl.BlockSpec(shape, lambda *_: (0,) * nd, pipeline_mode=pl.Buffered(1))


def _in_proj_kernel(x_ref, pos_ref, inv_ref, g_ref, wrow_ref, wt_ref, convw_ref, gconv_ref,
                    yconv_ref, kvc_ref, qt_ref, kaug_ref, vaug_ref, gt_ref, carry_ref):
    @pl.when(pl.program_id(1) == 0)
    def _():
        carry_ref[...] = jnp.zeros_like(carry_ref)

    tm = x_ref.shape[1]
    subs = [slice(c0, c0 + SUB_ROWS) for c0 in range(0, tm, SUB_ROWS)]
    u = [_rms_rows(x_ref[0, rows, :], g_ref[...]).astype(BF16) for rows in subs]
    prow = [_dot(v, wrow_ref[...]) for v in u]
    pt = [_dot_nt(wt_ref[...], v) for v in u]
    for rows, pr, ptr in zip(subs, prow, pt):
        _in_proj_epilogue(rows, pl.program_id(1) * tm + rows.start, pr, ptr, pos_ref, inv_ref, convw_ref, gconv_ref,
                          yconv_ref, kvc_ref, qt_ref, kaug_ref, vaug_ref, gt_ref, carry_ref)


def _in_proj_epilogue(rows, tok0, prow, pt, pos_ref, inv_ref, convw_ref, gconv_ref,
                      yconv_ref, kvc_ref, qt_ref, kaug_ref, vaug_ref, gt_ref, carry_ref):
    n = rows.stop - rows.start

    bg = prow[:, 0:CONV_WIDTH]
    z = prow[:, CONV_WIDTH:2 * CONV_WIDTH] * prow[:, 2 * CONV_WIDTH:3 * CONV_WIDTH]
    prev = carry_ref[...]
    row = lax.broadcasted_iota(jnp.int32, (n, CONV_WIDTH), 0)
    z1 = jnp.where(row == 0, prev[7:8, :], pltpu.roll(z, 1, axis=0))
    z2 = jnp.where(row == 0, prev[6:7, :], jnp.where(row == 1, prev[7:8, :], pltpu.roll(z, 2, axis=0)))
    carry_ref[...] = z[n - 8:n, :]
    cw = convw_ref[...]
    yc = bg * (cw[0:1, :] * z2 + cw[1:2, :] * z1 + cw[2:3, :] * z)
    yconv_ref[0, rows, :] = _rms_rows(yc, gconv_ref[...]).astype(yconv_ref.dtype)

    for c in range(2 * KV_GROUPS):
        lo = 3 * CONV_WIDTH + c * HEAD_DIM
        kvc_ref[0, c, rows, :] = prow[:, lo:lo + HEAD_DIM]

    ang = pos_ref[0, :, rows].astype(F32) * inv_ref[...]
    cos, sin = jnp.cos(ang), jnp.sin(ang)

    def rope_t(base):
        x1 = pt[base:base + HALF, :]
        x2 = pt[base + HALF:base + HEAD_DIM, :]
        return x1 * cos - x2 * sin, x2 * cos + x1 * sin

    scale = HEAD_DIM ** -0.5 * math.log2(math.e)
    for h in range(ATT_HEADS):
        r1, r2 = rope_t(h * HEAD_DIM)
        qt_ref[0, h * HEAD_DIM:h * HEAD_DIM + HALF, rows] = (r1 * scale).astype(qt_ref.dtype)
        qt_ref[0, h * HEAD_DIM + HALF:(h + 1) * HEAD_DIM, rows] = (r2 * scale).astype(qt_ref.dtype)

    tok = tok0 + lax.broadcasted_iota(jnp.int32, (n, HEAD_DIM), 0)
    blk_col = lax.broadcasted_iota(jnp.int32, (n, HEAD_DIM), 1)
    onehot = jnp.where(jnp.right_shift(tok, SLC_BLOCK.bit_length() - 1) == blk_col, 1.0, 0.0)
    for br in (WIN, SEL):
        base = ATT_WIDTH + br * KV_WIDTH
        parts = []
        for g in range(KV_GROUPS):
            parts.extend(rope_t(base + g * HEAD_DIM))
        k_rows = jnp.concatenate(parts, axis=0).T
        extra = onehot if br == SEL else jnp.zeros_like(onehot)
        for g in range(KV_GROUPS):
            k_g = k_rows[:, g * HEAD_DIM:(g + 1) * HEAD_DIM]
            kaug_ref[0, g, br, rows, :] = jnp.concatenate([k_g, extra], axis=-1).astype(kaug_ref.dtype)

    v0 = ATT_WIDTH + 2 * KV_WIDTH
    ones_rows = jnp.where(lax.broadcasted_iota(jnp.int32, (V_AUG - HEAD_DIM, n), 0) == 0, 1.0, 0.0)
    for br in (WIN, SEL):
        for g in range(KV_GROUPS):
            lo = v0 + br * KV_WIDTH + g * HEAD_DIM
            vaug_ref[0, g, br, :, rows] = jnp.concatenate([pt[lo:lo + HEAD_DIM, :], ones_rows], axis=0).astype(vaug_ref.dtype)

    g0 = v0 + 2 * KV_WIDTH
    gt_ref[0, :, rows] = jax.nn.sigmoid(pt[g0:g0 + KV_GROUPS * GATE_ROWS, :])


def _in_proj(x, pos_row, inv_col, g_pre, w_row, w_t, conv_w, g_conv):
    b, s, _ = x.shape
    tm = min(TM_PROJ, s)
    n_row, n_t = w_row.shape[1], w_t.shape[0]
    out_shape = (
        jax.ShapeDtypeStruct((b, s, CONV_WIDTH), BF16),
        jax.ShapeDtypeStruct((b, 2 * KV_GROUPS, s, HEAD_DIM), F32),
        jax.ShapeDtypeStruct((b, ATT_WIDTH, s), BF16),
        jax.ShapeDtypeStruct((b, KV_GROUPS, 2, s, K_AUG), BF16),
        jax.ShapeDtypeStruct((b, KV_GROUPS, 2, V_AUG, s), BF16),
        jax.ShapeDtypeStruct((b, KV_GROUPS * GATE_ROWS, s), F32),
    )
    return pl.pallas_call(
        _in_proj_kernel,
        grid=(b, s // tm),
        in_specs=[
            pl.BlockSpec((1, tm, D_MODEL), lambda i, j: (i, j, 0)),
            pl.BlockSpec((1, 1, tm), lambda i, j: (i, 0, j)),
            _whole((HALF, 1)),
            _whole((1, D_MODEL)),
            _whole((D_MODEL, n_row)),
            _whole((n_t, D_MODEL)),
            _whole((CONV_K, CONV_WIDTH)),
            _whole((1, CONV_WIDTH)),
        ],
        out_specs=(
            pl.BlockSpec((1, tm, CONV_WIDTH), lambda i, j: (i, j, 0)),
            pl.BlockSpec((1, 2 * KV_GROUPS, tm, HEAD_DIM), lambda i, j: (i, 0, j, 0)),
            pl.BlockSpec((1, ATT_WIDTH, tm), lambda i, j: (i, 0, j)),
            pl.BlockSpec((1, KV_GROUPS, 2, tm, K_AUG), lambda i, j: (i, 0, 0, j, 0)),
            pl.BlockSpec((1, KV_GROUPS, 2, V_AUG, tm), lambda i, j: (i, 0, 0, 0, j)),
            pl.BlockSpec((1, KV_GROUPS * GATE_ROWS, tm), lambda i, j: (i, 0, j)),
        ),
        out_shape=out_shape,
        scratch_shapes=[pltpu.VMEM((8, CONV_WIDTH), F32)],
        compiler_params=_params("arbitrary", "arbitrary"),
        name="in_proj",
    )(x, pos_row, inv_col, g_pre, w_row, w_t, conv_w, g_conv)


def _compress_kernel(rk_ref, rv_ref, pos_ref, inv_ref, pek_ref, w1k_ref, w2k_ref, pev_ref, w1v_ref, w2vt_ref,
                     kc_ref, vct_ref):
    half_in = CMP_STRIDE * HEAD_DIM
    n_slab = kc_ref.shape[2]

    sides = ((rk_ref, pek_ref, w1k_ref), (rv_ref, pev_ref, w1v_ref))
    a = [jnp.zeros((n_slab, CMP_HIDDEN), F32) for _ in sides]
    b = [jnp.zeros((n_slab, CMP_HIDDEN), F32) for _ in sides]
    per_dot = 256 // HEAD_DIM
    for j0 in range(0, CMP_STRIDE, per_dot):
        lo = j0 * HEAD_DIM
        xcat = [jnp.concatenate([x_ref[0, 0, pl.ds(j, n_slab, stride=CMP_STRIDE), :]
                                 for j in range(j0, j0 + per_dot)], axis=-1).astype(BF16)
                for x_ref, _, _ in sides]
        for n, (_, _, w1_ref) in enumerate(sides):
            a[n] = a[n] + _dot(xcat[n], w1_ref[lo:lo + per_dot * HEAD_DIM, :])
            b[n] = b[n] + _dot(xcat[n], w1_ref[half_in + lo:half_in + lo + per_dot * HEAD_DIM, :])
    hidden = []
    for n, (_, pe_ref, w1_ref) in enumerate(sides):
        pe = jnp.broadcast_to(pe_ref[...], (8, 2 * half_in)).astype(BF16)
        h = a[n] + pltpu.roll(b[n], n_slab - 1, axis=0) + _dot(pe, w1_ref[...])[0:1, :]
        hidden.append((h * jax.nn.sigmoid(h)).astype(BF16))

    kc = _dot(hidden[0], w2k_ref[...])
    ang = pos_ref[0].astype(F32) * inv_ref[...]
    cos, sin = jnp.cos(ang), jnp.sin(ang)
    x1, x2 = kc[:, 0:HALF], kc[:, HALF:HEAD_DIM]
    kc_ref[0, 0] = jnp.concatenate([x1 * cos - x2 * sin, x2 * cos + x1 * sin], axis=-1).astype(kc_ref.dtype)
    vct_ref[0, 0] = _dot_nt(w2vt_ref[...], hidden[1]).astype(vct_ref.dtype)


def _compress(kvc, pos_c, inv_row, pek, w1k, w2k, pev, w1v, w2vt):
    b, _, s, _ = kvc.shape
    g, n_slab = KV_GROUPS, s // CMP_STRIDE
    return pl.pallas_call(
        _compress_kernel,
        grid=(b, g),
        in_specs=[
            pl.BlockSpec((1, 1, s, HEAD_DIM), lambda i, j: (i, j, 0, 0)),
            pl.BlockSpec((1, 1, s, HEAD_DIM), lambda i, j: (i, KV_GROUPS + j, 0, 0)),
            pl.BlockSpec((1, n_slab, 1), lambda i, j: (i, 0, 0)),
            _whole((1, HALF)),
            _whole(pek.shape), _whole(w1k.shape), _whole(w2k.shape),
            _whole(pev.shape), _whole(w1v.shape), _whole(w2vt.shape),
        ],
        out_specs=(
            pl.BlockSpec((1, 1, n_slab, HEAD_DIM), lambda i, j: (i, j, 0, 0)),
            pl.BlockSpec((1, 1, HEAD_DIM, n_slab), lambda i, j: (i, j, 0, 0)),
        ),
        out_shape=(
            jax.ShapeDtypeStruct((b, g, n_slab, HEAD_DIM), BF16),
            jax.ShapeDtypeStruct((b, g, HEAD_DIM, n_slab), BF16),
        ),
        compiler_params=_params("arbitrary", "arbitrary"),
        name="compress",
    )(kvc, kvc, pos_c, inv_row, pek, w1k, w2k, pev, w1v, w2vt)


def _select_kernel(qt_ref, kc_ref, vct_ref, gt_ref, map_ref, ocmp_ref, sel_ref, cs_ref, imp_ref):
    n_cmp = kc_ref.shape[2]
    n_slc = map_ref.shape[0]
    step = pl.program_id(1)
    n_sub = qt_ref.shape[2] // TQ
    subs = [slice(c * TQ, (c + 1) * TQ) for c in range(n_sub)]
    row_of = lambda h: slice(h * HEAD_DIM, (h + 1) * HEAD_DIM)
    heads_of = lambda g: range(g * Q_PER_KV, (g + 1) * Q_PER_KV)
    t_of = [(step * n_sub + c) * TQ + lax.broadcasted_iota(jnp.int32, (1, TQ), 1) for c in range(n_sub)]

    def gate_row(h):
        row = (h // Q_PER_KV) * GATE_ROWS + h % Q_PER_KV
        return gt_ref[0, row:row + 1, :]

    def compressed_branch(n_live):
        rows = min(n_live * (SLC_BLOCK // CMP_STRIDE), n_cmp)
        for c, lanes in enumerate(subs):
            for g in range(KV_GROUPS):
                kc = kc_ref[0, g, 0:rows, :]
                for h in heads_of(g):
                    cs_ref[c, h, 0:rows, :] = _dot(kc, qt_ref[0, row_of(h), lanes])

        cmp_end = lax.broadcasted_iota(jnp.int32, (rows, TQ), 0) * CMP_STRIDE + (CMP_BLOCK - 1)
        psum = []
        for c, lanes in enumerate(subs):
            cbias = jnp.where(cmp_end <= t_of[c], 0.0, NEG)
            has_cmp = t_of[c] >= CMP_BLOCK - 1
            for g in range(KV_GROUPS):
                vct = vct_ref[0, g, :, 0:rows]
                acc = jnp.zeros((rows, TQ), F32)
                for h in heads_of(g):
                    s = cs_ref[c, h, 0:rows, :] + cbias
                    e = jnp.exp2(s - jnp.max(s, axis=0, keepdims=True))
                    inv = jnp.where(has_cmp, 1.0 / jnp.maximum(jnp.sum(e, axis=0, keepdims=True), 1e-30), 0.0)
                    acc = acc + e * inv
                    ocmp_ref[0, row_of(h), lanes] = (gate_row(h)[:, lanes] * inv) * _dot(vct, e.astype(BF16))
                psum.append(acc)

        blk = lax.broadcasted_iota(jnp.int32, (n_live, TQ), 0)
        for c in range(n_sub):
            cur = jnp.right_shift(t_of[c], SLC_BLOCK.bit_length() - 1)
            forced = (blk == 0) | (blk == cur) | (blk == cur - 1)
            for g in range(KV_GROUPS):
                imp = jnp.dot(map_ref[0:n_live, 0:rows], psum[c * KV_GROUPS + g], preferred_element_type=F32,
                              precision=lax.Precision.HIGHEST)
                imp_ref[c, g, 0:n_live, :] = jnp.where(forced, FORCE, jnp.where(blk * SLC_BLOCK > t_of[c], -FORCE, imp))

    def select_blocks(n_live, c, g):
        n_grp = n_live // 8
        imp_g = [imp_ref[c, g, 8 * a:8 * a + 8, :] for a in range(n_grp)]
        rank_g = [jnp.zeros((8, TQ), jnp.int32) for _ in range(n_grp)]
        sub = lax.broadcasted_iota(jnp.int32, (8, TQ), 0)
        for j in range(n_live):
            vj = jnp.broadcast_to(imp_ref[c, g, j:j + 1, :], (8, TQ))
            for a in range(n_grp):
                if 8 * a > j:
                    beats = jnp.where(vj >= imp_g[a], 1, 0)
                elif 8 * a + 7 < j:
                    beats = jnp.where(vj > imp_g[a], 1, 0)
                else:
                    beats = jnp.where(sub > j - 8 * a, jnp.where(vj >= imp_g[a], 1, 0), jnp.where(vj > imp_g[a], 1, 0))
                rank_g[a] = rank_g[a] + beats
        sel = jnp.where(jnp.concatenate(rank_g, axis=0) < min(N_SELECT, n_slc), 0.0, NEG).astype(BF16)
        dead = K_AUG - HEAD_DIM - n_live
        if dead:
            sel = jnp.concatenate([sel, jnp.full((dead, TQ), NEG, BF16)], axis=0)
        sel_ref[0, g, :, subs[c]] = sel

    def select_all(n_live):
        compressed_branch(n_live)
        for c in range(n_sub):
            for g in range(KV_GROUPS):
                select_blocks(n_live, c, g)

    blocks_per_step = qt_ref.shape[2] // SLC_BLOCK
    live_sizes = list(range(_RANK_STEP, n_slc, _RANK_STEP)) + [n_slc]
    for v, n_live in enumerate(live_sizes):
        lo = v * _RANK_STEP // blocks_per_step
        hi = n_live // blocks_per_step if v + 1 < len(live_sizes) else n_slc // blocks_per_step
        pl.when((step >= lo) & (step < hi))(lambda n_live=n_live: select_all(n_live))


def _select(qt, kcmp, vcmpt, gt, slc_map_t):
    b, _, s = qt.shape
    tqs = min(TQ_SELECT, s)
    n_cmp = kcmp.shape[2]
    n_slc = s // SLC_BLOCK
    assert tqs % TQ == 0 and s % tqs == 0 and n_slc % 8 == 0 and n_slc <= K_AUG - HEAD_DIM
    assert _RANK_STEP % 8 == 0 and _RANK_STEP % (tqs // SLC_BLOCK) == 0 and n_slc % (tqs // SLC_BLOCK) == 0
    per_batch = lambda shape: pl.BlockSpec((1,) + shape, lambda i, k: (i,) + (0,) * len(shape))
    return pl.pallas_call(
        _select_kernel,
        grid=(b, s // tqs),
        in_specs=[
            pl.BlockSpec((1, ATT_WIDTH, tqs), lambda i, k: (i, 0, k)),
            per_batch((KV_GROUPS, n_cmp, HEAD_DIM)),
            per_batch((KV_GROUPS, HEAD_DIM, n_cmp)),
            pl.BlockSpec((1, KV_GROUPS * GATE_ROWS, tqs), lambda i, k: (i, 0, k)),
            _whole((n_slc, n_cmp)),
        ],
        out_specs=(
            pl.BlockSpec((1, ATT_WIDTH, tqs), lambda i, k: (i, 0, k)),
            pl.BlockSpec((1, KV_GROUPS, K_AUG - HEAD_DIM, tqs), lambda i, k: (i, 0, 0, k)),
        ),
        out_shape=(
            jax.ShapeDtypeStruct((b, ATT_WIDTH, s), F32),
            jax.ShapeDtypeStruct((b, KV_GROUPS, K_AUG - HEAD_DIM, s), BF16),
        ),
        scratch_shapes=[
            pltpu.VMEM((tqs // TQ, ATT_HEADS, n_cmp, TQ), F32),
            pltpu.VMEM((tqs // TQ, KV_GROUPS, n_slc, TQ), F32),
        ],
        compiler_params=_params("arbitrary", "arbitrary"),
        name="nsa_select",
    )(qt, kcmp, vcmpt, gt, slc_map_t)


def _nsa_kernel(qt_ref, ocmp_ref, sel_ref, kaug_ref, vaug_ref, gt_ref, tri_ref,
                out_ref, rhs_ref, s_ref, m_ref, acc_ref):
    tq = qt_ref.shape[2]
    i = pl.program_id(1)
    row_of = lambda h: slice(h * HEAD_DIM, (h + 1) * HEAD_DIM)
    heads_of = lambda g: range(g * Q_PER_KV, (g + 1) * Q_PER_KV)

    def issue_scores(slot, br, tile):
        for g in range(KV_GROUPS):
            k_t = kaug_ref[0, g, br, pl.ds(pl.multiple_of(tile * TK, TK), TK), :]
            for h in heads_of(g):
                s_ref[slot, h] = _dot(k_t, rhs_ref[br, h]).astype(s_ref.dtype)

    def softmax_step(slot, br, tile, bias_idx=None, first=False):
        for g in range(KV_GROUPS):
            v_t = vaug_ref[0, g, br, :, pl.ds(pl.multiple_of(tile * TK, TK), TK)]
            for h in heads_of(g):
                s = s_ref[slot, h]
                if bias_idx is not None:
                    s = s + tri_ref[bias_idx]
                if first:
                    m_new = jnp.max(s, axis=0, keepdims=True).astype(F32)
                    acc_ref[h] = _dot(v_t, jnp.exp2(s - m_new.astype(s.dtype)))
                else:
                    m_old = m_ref[h]
                    m_new = jnp.maximum(m_old, jnp.max(s, axis=0, keepdims=True).astype(F32))
                    p = jnp.exp2(s - m_new.astype(s.dtype))
                    acc_ref[h] = jnp.exp2(m_old - m_new) * acc_ref[h] + _dot(v_t, p)
                m_ref[h] = m_new

    def gate_row(branch, h):
        row = (h // Q_PER_KV) * GATE_ROWS + branch * Q_PER_KV + h % Q_PER_KV
        return gt_ref[0, row:row + 1, :]

    def flush_state(branch):
        for h in range(ATT_HEADS):
            acc = acc_ref[h]
            o = acc[0:HEAD_DIM, :] * (1.0 / jnp.maximum(acc[HEAD_DIM:HEAD_DIM + 1, :], 1e-30))
            out_ref[0, row_of(h), :] = out_ref[0, row_of(h), :] + gate_row(branch, h) * o

    zeros_half = jnp.zeros((K_AUG - HEAD_DIM, tq), BF16)
    for h in range(ATT_HEADS):
        rhs_ref[WIN, h, 0:HEAD_DIM, :] = qt_ref[0, row_of(h), :]
        rhs_ref[WIN, h, HEAD_DIM:K_AUG, :] = zeros_half
        rhs_ref[SEL, h, 0:HEAD_DIM, :] = qt_ref[0, row_of(h), :]
        rhs_ref[SEL, h, HEAD_DIM:K_AUG, :] = sel_ref[0, h // Q_PER_KV]
    n_win = WINDOW // TK + 1
    win_tiles = [i - (n_win - 1) + k for k in range(n_win)]
    def window_step(k):
        exists = win_tiles[k] >= 0
        bias_idx = _BIAS_CAUSAL if k == n_win - 1 else jnp.where(exists, _BIAS_WINDOW_START if k == 0 else _BIAS_NONE, _BIAS_MASKED)
        softmax_step(_SLOT_WIN + k, WIN, jnp.maximum(win_tiles[k], 0), bias_idx, first=(k == 0))

    for k in range(n_win):
        issue_scores(_SLOT_WIN + k, WIN, jnp.maximum(win_tiles[k], 0))
    issue_scores(0, SEL, 0)
    for h in range(ATT_HEADS):
        out_ref[0, row_of(h), :] = ocmp_ref[0, row_of(h), :]
    for k in range(n_win):
        window_step(k)
    flush_state(2)

    for h in range(ATT_HEADS):
        m_ref[h] = jnp.full((1, tq), NEG, F32)
        acc_ref[h] = jnp.zeros((V_AUG, tq), F32)

    def pair_steps(a):
        issue_scores(1, SEL, a + 1)
        softmax_step(0, SEL, a)
        issue_scores(0, SEL, a + 2)
        softmax_step(1, SEL, a + 1)

    def body(quad, carry):
        pair_steps(4 * quad)
        pair_steps(4 * quad + 2)
        return carry

    n_plain = jnp.right_shift(i, 1)
    n_quad = jnp.right_shift(n_plain, 1)
    lax.fori_loop(0, n_quad, body, 0)

    @pl.when(n_plain != 2 * n_quad)
    def _():
        pair_steps(4 * n_quad)

    @pl.when(i == 2 * n_plain)
    def _():
        softmax_step(0, SEL, i, _BIAS_CAUSAL)

    @pl.when(i != 2 * n_plain)
    def _():
        issue_scores(1, SEL, i)
        softmax_step(0, SEL, i - 1)
        softmax_step(1, SEL, i, _BIAS_CAUSAL)

    flush_state(1)


def _nsa(qt, ocmp, sel, kaug, vaug, gt, tri):
    b, _, s = qt.shape
    tq = min(TQ, s)
    assert tq == TK and s % tq == 0 and WINDOW % TK == 0 and _SLOT_WIN + WINDOW // TK + 1 <= _N_SLOTS
    per_batch = lambda shape: pl.BlockSpec((1,) + shape, lambda i, k: (i,) + (0,) * len(shape))
    return pl.pallas_call(
        _nsa_kernel,
        grid=(b, s // tq),
        in_specs=[
            pl.BlockSpec((1, ATT_WIDTH, tq), lambda i, k: (i, 0, k)),
            pl.BlockSpec((1, ATT_WIDTH, tq), lambda i, k: (i, 0, k)),
            pl.BlockSpec((1, KV_GROUPS, K_AUG - HEAD_DIM, tq), lambda i, k: (i, 0, 0, k)),
            per_batch((KV_GROUPS, 2, s, K_AUG)),
            per_batch((KV_GROUPS, 2, V_AUG, s)),
            pl.BlockSpec((1, KV_GROUPS * GATE_ROWS, tq), lambda i, k: (i, 0, k)),
            _whole((4, TK, tq)),
        ],
        out_specs=pl.BlockSpec((1, ATT_WIDTH, tq), lambda i, k: (i, 0, k)),
        out_shape=jax.ShapeDtypeStruct((b, ATT_WIDTH, s), F32),
        scratch_shapes=[
            pltpu.VMEM((2, ATT_HEADS, K_AUG, tq), BF16),
            pltpu.VMEM((_N_SLOTS, ATT_HEADS, TK, tq), BF16),
            pltpu.VMEM((ATT_HEADS, 1, tq), F32),
            pltpu.VMEM((ATT_HEADS, V_AUG, tq), F32),
        ],
        compiler_params=_params("arbitrary", "arbitrary"),
        name="nsa_attention",
    )(qt, ocmp, sel, kaug, vaug, gt, tri)


def _mix_xattn_kernel(x_ref, yconv_ref, yatt_ref, gatt_ref, woa_ref, wob_ref, gpost_ref, gxpre_ref,
                      wqt_ref, mem_ref, gmem_ref, wk_ref, wvt_ref, wo_ref, gxpost_ref, out_ref, kx_ref, vxt_ref):
    @pl.when(pl.program_id(1) == 0)
    def _():
        mn = _rms_rows(mem_ref[0], gmem_ref[...]).astype(BF16)
        kx_ref[...] = _dot(mn, wk_ref[...]).astype(BF16)
        vxt_ref[...] = _dot_nt(wvt_ref[...], mn).astype(BF16)

    subs = [slice(c0, c0 + SUB_ROWS) for c0 in range(0, x_ref.shape[1], SUB_ROWS)]
    head = lambda h: slice(h * X_HEAD_DIM, (h + 1) * X_HEAD_DIM)

    def yatt_normed(rows):
        ya = yatt_ref[0, :, rows]
        return (ya * lax.rsqrt(jnp.mean(ya * ya, axis=0, keepdims=True) + EPS) * gatt_ref[...]).astype(BF16)

    yan = [yatt_normed(rows) for rows in subs]
    mixed = [_dot(yconv_ref[0, rows, :], woa_ref[...]) + _dot_tn(y, wob_ref[...]) for rows, y in zip(subs, yan)]
    h1 = [x_ref[0, rows, :] + _rms_rows(m, gpost_ref[...]) for rows, m in zip(subs, mixed)]
    hn = [_rms_rows(h, gxpre_ref[...]).astype(BF16) for h in h1]
    qxt = [(_dot_nt(wqt_ref[...], h) * (X_HEAD_DIM ** -0.5 * math.log2(math.e))).astype(BF16) for h in hn]
    scores = [[_dot(kx_ref[:, head(h)], q[head(h), :]) for h in range(X_HEADS)] for q in qxt]

    def attend(sc):
        heads = []
        for h in range(X_HEADS):
            e = jnp.exp2(sc[h] - jnp.max(sc[h], axis=0, keepdims=True))
            o = _dot(vxt_ref[head(h), :], e.astype(BF16))
            heads.append(o * (1.0 / jnp.sum(e, axis=0, keepdims=True)))
        return jnp.concatenate(heads, axis=0).astype(BF16)

    oxt = [attend(sc) for sc in scores]
    proj = [_dot_tn(o, wo_ref[...]) for o in oxt]
    for rows, h, p in zip(subs, h1, proj):
        out_ref[0, rows, :] = h + _rms_rows(p, gxpost_ref[...])


def _mix_xattn(x, yconv, yatt_t, g_att, w_out_a, w_out_b, g_post, g_xpre, w_qt, mem, g_mem, w_k, w_vt, w_o, g_xpost):
    b, s, _ = x.shape
    m = mem.shape[1]
    tm = min(TM_MIX, s)
    return pl.pallas_call(
        _mix_xattn_kernel,
        grid=(b, s // tm),
        in_specs=[
            pl.BlockSpec((1, tm, D_MODEL), lambda i, j: (i, j, 0)),
            pl.BlockSpec((1, tm, CONV_WIDTH), lambda i, j: (i, j, 0)),
            pl.BlockSpec((1, ATT_WIDTH, tm), lambda i, j: (i, 0, j)),
            _whole((ATT_WIDTH, 1)),
            _whole((CONV_WIDTH, D_MODEL)), _whole((ATT_WIDTH, D_MODEL)),
            _whole((1, D_MODEL)), _whole((1, D_MODEL)),
            _whole((D_MODEL, D_MODEL)),
            pl.BlockSpec((1, m, D_MODEL), lambda i, j: (i, 0, 0)),
            _whole((1, D_MODEL)), _whole((D_MODEL, D_MODEL)), _whole((D_MODEL, D_MODEL)),
            _whole((D_MODEL, D_MODEL)), _whole((1, D_MODEL)),
        ],
        out_specs=pl.BlockSpec((1, tm, D_MODEL), lambda i, j: (i, j, 0)),
        out_shape=jax.ShapeDtypeStruct((b, s, D_MODEL), F32),
        scratch_shapes=[
            pltpu.VMEM((m, D_MODEL), BF16),
            pltpu.VMEM((D_MODEL, m), BF16),
        ],
        compiler_params=_params("arbitrary", "arbitrary"),
        name="mix_xattn",
    )(x, yconv, yatt_t, g_att, w_out_a, w_out_b, g_post, g_xpre, w_qt, mem, g_mem, w_k, w_vt, w_o, g_xpost)


def _ffn_kernel(h_ref, gpre_ref, wgu_ref, wdown_ref, gpost_ref, out_ref):
    subs = [slice(c0, c0 + TM_FFN_SUB) for c0 in range(0, h_ref.shape[0], TM_FFN_SUB)]
    hn = [_rms_rows(h_ref[rows, :], gpre_ref[...]).astype(BF16) for rows in subs]
    gu = [_dot(v, wgu_ref[...]) for v in hn]
    act = [(v[:, 0:D_FF] * jax.nn.sigmoid(v[:, 0:D_FF]) * v[:, D_FF:2 * D_FF]).astype(BF16) for v in gu]
    down = [_dot(a, wdown_ref[...]) for a in act]
    for rows, d in zip(subs, down):
        out_ref[rows, :] = h_ref[rows, :] + _rms_rows(d, gpost_ref[...])


def _ffn(h, g_pre, w_gu, w_down, g_post):
    n, _ = h.shape
    tm = min(TM_FFN, n)
    return pl.pallas_call(
        _ffn_kernel,
        grid=(n // tm,),
        in_specs=[pl.BlockSpec((tm, D_MODEL), lambda i: (i, 0)), _whole((1, D_MODEL)),
                  _whole((D_MODEL, 2 * D_FF)), _whole((D_FF, D_MODEL)), _whole((1, D_MODEL))],
        out_specs=pl.BlockSpec((tm, D_MODEL), lambda i: (i, 0)),
        out_shape=jax.ShapeDtypeStruct((n, D_MODEL), F32),
        compiler_params=_params("arbitrary"),
        name="ffn",
    )(h, g_pre, w_gu, w_down, g_post)


def _slc_map_t(s):
    n_slab = s // CMP_STRIDE
    ci = np.arange(n_slab)[None, :] * CMP_STRIDE
    sj = np.arange(s // SLC_BLOCK)[:, None] * SLC_BLOCK
    ov = np.clip(np.minimum(ci + CMP_BLOCK, sj + SLC_BLOCK) - np.maximum(ci, sj), 0, None)
    ov[:, n_slab - 1] = 0
    return jnp.asarray(ov / CMP_BLOCK, dtype=F32)


def _tile_biases(tq):
    key = np.arange(TK)[:, None]
    qry = np.arange(tq)[None, :]
    none = np.zeros((TK, tq), np.float32)
    causal = np.where(key <= qry, 0.0, NEG).astype(np.float32)
    window_start = np.where(key > qry, 0.0, NEG).astype(np.float32)
    masked = np.full((TK, tq), NEG, np.float32)
    return jnp.asarray(np.stack([none, causal, window_start, masked]), dtype=BF16)


def kernel(x, mem, positions, norm_mix_pre, w_in, conv_w, pe_kc, w1_kc, w2_kc, pe_vc, w1_vc, w2_vc,
           norm_conv_out, norm_attn_out, w_out, norm_mix_post, norm_x_pre, norm_mem, w_q_x, w_kv_x,
           w_o_x, norm_x_post, norm_ffn_pre, w_gate_up, w_down, norm_ffn_post):
    b, s, _ = x.shape
    row = lambda v: v.reshape(1, -1).astype(F32)

    cuts = np.cumsum([0, CONV_WIDTH, CONV_WIDTH, CONV_WIDTH, ATT_WIDTH] + [KV_WIDTH] * 6 + [3 * ATT_HEADS])
    w_in = w_in.astype(BF16)
    col = lambda k: w_in[:, cuts[k]:cuts[k + 1]]
    w_row = jnp.concatenate([col(0), col(1), col(2), col(4), col(5)], axis=1)
    gate_cols = np.zeros((KV_GROUPS * GATE_ROWS,), np.int32)
    gate_live = np.zeros((KV_GROUPS * GATE_ROWS,), np.float32)
    for g in range(KV_GROUPS):
        for c in range(3):
            for r in range(Q_PER_KV):
                gate_cols[g * GATE_ROWS + c * Q_PER_KV + r] = (g * Q_PER_KV + r) * 3 + c
                gate_live[g * GATE_ROWS + c * Q_PER_KV + r] = 1.0
    w_gate = col(10)[:, gate_cols] * jnp.asarray(gate_live, BF16)[None, :]
    w_t = jnp.concatenate([col(3), col(8), col(6), col(9), col(7), w_gate], axis=1).T

    inv = ROPE_THETA ** (-jnp.arange(HALF, dtype=F32) / HALF)
    yconv, kvc, qt, kaug, vaug, gt = _in_proj(
        x, positions.reshape(b, 1, s), inv.reshape(HALF, 1), row(norm_mix_pre), w_row, w_t,
        conv_w.astype(F32), row(norm_conv_out))

    n_slab = s // CMP_STRIDE
    pos_c = positions[:, np.minimum(np.arange(n_slab) * CMP_STRIDE + CMP_BLOCK - 1, s - 1)]
    kcmp, vcmpt = _compress(
        kvc, pos_c.reshape(b, n_slab, 1), inv.reshape(1, HALF),
        pe_kc.reshape(1, -1), w1_kc.astype(BF16), w2_kc.astype(BF16),
        pe_vc.reshape(1, -1), w1_vc.astype(BF16), w2_vc.astype(BF16).T)

    ocmp, sel = _select(qt, kcmp, vcmpt, gt, _slc_map_t(s))
    yatt_t = _nsa(qt, ocmp, sel, kaug, vaug, gt, _tile_biases(min(TQ, s)))

    w_kv_x = w_kv_x.astype(BF16)
    h2 = _mix_xattn(
        x, yconv, yatt_t, norm_attn_out.reshape(-1, 1).astype(F32),
        w_out[:CONV_WIDTH].astype(BF16), w_out[CONV_WIDTH:].astype(BF16),
        row(norm_mix_post), row(norm_x_pre), w_q_x.astype(BF16).T,
        mem, row(norm_mem), w_kv_x[:, :D_MODEL], w_kv_x[:, D_MODEL:].T, w_o_x.astype(BF16), row(norm_x_post))

    out = _ffn(h2.reshape(b * s, D_MODEL), row(norm_ffn_pre), w_gate_up.astype(BF16), w_down.astype(BF16),
               row(norm_ffn_post))
    return out.reshape(b, s, D_MODEL)
```

```python
import math

import numpy as np
import jax
import jax.numpy as jnp
from jax import lax
from jax.experimental import pallas as pl
from jax.experimental.pallas import tpu as pltpu

D_MODEL = 1024
CONV_WIDTH = 512
CONV_K = 3
ATT_HEADS = 8
HEAD_DIM = 64
HALF = HEAD_DIM // 2
ATT_WIDTH = ATT_HEADS * HEAD_DIM
KV_GROUPS = 2
Q_PER_KV = ATT_HEADS // KV_GROUPS
KV_WIDTH = KV_GROUPS * HEAD_DIM
CMP_BLOCK = 32
CMP_STRIDE = 16
CMP_HIDDEN = 256
SLC_BLOCK = 64
N_SELECT = 16
WINDOW = 512
ROPE_THETA = 10000.0
X_HEADS = 4
X_HEAD_DIM = D_MODEL // X_HEADS
D_FF = 2816
EPS = 1e-6
FORCE = 1e4
NEG = -1e30
GATE_ROWS = 16
K_AUG = 2 * HEAD_DIM
V_AUG = HEAD_DIM + 16
WIN, SEL = 0, 1
_RANK_STEP = 8
_SLOT_WIN, _N_SLOTS = 2, 5
_BIAS_NONE, _BIAS_CAUSAL, _BIAS_WINDOW_START, _BIAS_MASKED = 0, 1, 2, 3

TM_PROJ = 1024
TM_MIX = 1024
TM_FFN = 1024
TM_FFN_SUB = 256
SUB_ROWS = 512
TQ = 256
TQ_SELECT = 512
TK = 256
VMEM_LIMIT = 56 * 1024 * 1024

F32 = jnp.float32
BF16 = jnp.bfloat16

_NT = (((1,), (1,)), ((), ()))
_TN = (((0,), (0,)), ((), ()))


def _dot(a, b):
    return jnp.dot(a, b, preferred_element_type=F32)


def _dot_nt(a, b):
    return lax.dot_general(a, b, _NT, preferred_element_type=F32)


def _dot_tn(a, b):
    return lax.dot_general(a, b, _TN, preferred_element_type=F32)


def _rms_rows(x, g):
    return x * lax.rsqrt(jnp.mean(x * x, axis=-1, keepdims=True) + EPS) * g


def _params(*sem):
    return pltpu.CompilerParams(dimension_semantics=sem, vmem_limit_bytes=VMEM_LIMIT)


def _whole(shape):
    nd = len(shape)
    return pl.BlockSpec(shape, lambda *_: (0,) * nd, pipeline_mode=pl.Buffered(1))


def _in_proj_kernel(x_ref, pos_ref, inv_ref, g_ref, wrow_ref, wt_ref, convw_ref, gconv_ref,
                    yconv_ref, kvc_ref, qt_ref, kaug_ref, vaug_ref, gt_ref, carry_ref):
    @pl.when(pl.program_id(1) == 0)
    def _():
        carry_ref[...] = jnp.zeros_like(carry_ref)

    tm = x_ref.shape[1]
    subs = [slice(c0, c0 + SUB_ROWS) for c0 in range(0, tm, SUB_ROWS)]
    u = [_rms_rows(x_ref[0, rows, :], g_ref[...]).astype(BF16) for rows in subs]
    prow = [_dot(v, wrow_ref[...]) for v in u]
    pt = [_dot_nt(wt_ref[...], v) for v in u]
    for rows, pr, ptr in zip(subs, prow, pt):
        _in_proj_epilogue(rows, pl.program_id(1) * tm + rows.start, pr, ptr, pos_ref, inv_ref, convw_ref, gconv_ref,
                          yconv_ref, kvc_ref, qt_ref, kaug_ref, vaug_ref, gt_ref, carry_ref)


def _in_proj_epilogue(rows, tok0, prow, pt, pos_ref, inv_ref, convw_ref, gconv_ref,
                      yconv_ref, kvc_ref, qt_ref, kaug_ref, vaug_ref, gt_ref, carry_ref):
    n = rows.stop - rows.start

    bg = prow[:, 0:CONV_WIDTH]
    z = prow[:, CONV_WIDTH:2 * CONV_WIDTH] * prow[:, 2 * CONV_WIDTH:3 * CONV_WIDTH]
    prev = carry_ref[...]
    row = lax.broadcasted_iota(jnp.int32, (n, CONV_WIDTH), 0)
    z1 = jnp.where(row == 0, prev[7:8, :], pltpu.roll(z, 1, axis=0))
    z2 = jnp.where(row == 0, prev[6:7, :], jnp.where(row == 1, prev[7:8, :], pltpu.roll(z, 2, axis=0)))
    carry_ref[...] = z[n - 8:n, :]
    cw = convw_ref[...]
    yc = bg * (cw[0:1, :] * z2 + cw[1:2, :] * z1 + cw[2:3, :] * z)
    yconv_ref[0, rows, :] = _rms_rows(yc, gconv_ref[...]).astype(yconv_ref.dtype)

    for c in range(2 * KV_GROUPS):
        lo = 3 * CONV_WIDTH + c * HEAD_DIM
        kvc_ref[0, c, rows, :] = prow[:, lo:lo + HEAD_DIM]

    ang = pos_ref[0, :, rows].astype(F32) * inv_ref[...]
    cos, sin = jnp.cos(ang), jnp.sin(ang)

    def rope_t(base):
        x1 = pt[base:base + HALF, :]
        x2 = pt[base + HALF:base + HEAD_DIM, :]
        return x1 * cos - x2 * sin, x2 * cos + x1 * sin

    scale = HEAD_DIM ** -0.5 * math.log2(math.e)
    for h in range(ATT_HEADS):
        r1, r2 = rope_t(h * HEAD_DIM)
        qt_ref[0, h * HEAD_DIM:h * HEAD_DIM + HALF, rows] = (r1 * scale).astype(qt_ref.dtype)
        qt_ref[0, h * HEAD_DIM + HALF:(h + 1) * HEAD_DIM, rows] = (r2 * scale).astype(qt_ref.dtype)

    tok = tok0 + lax.broadcasted_iota(jnp.int32, (n, HEAD_DIM), 0)
    blk_col = lax.broadcasted_iota(jnp.int32, (n, HEAD_DIM), 1)
    onehot = jnp.where(jnp.right_shift(tok, SLC_BLOCK.bit_length() - 1) == blk_col, 1.0, 0.0)
    for br in (WIN, SEL):
        base = ATT_WIDTH + br * KV_WIDTH
        parts = []
        for g in range(KV_GROUPS):
            parts.extend(rope_t(base + g * HEAD_DIM))
        k_rows = jnp.concatenate(parts, axis=0).T
        extra = onehot if br == SEL else jnp.zeros_like(onehot)
        for g in range(KV_GROUPS):
            k_g = k_rows[:, g * HEAD_DIM:(g + 1) * HEAD_DIM]
            kaug_ref[0, g, br, rows, :] = jnp.concatenate([k_g, extra], axis=-1).astype(kaug_ref.dtype)

    v0 = ATT_WIDTH + 2 * KV_WIDTH
    ones_rows = jnp.where(lax.broadcasted_iota(jnp.int32, (V_AUG - HEAD_DIM, n), 0) == 0, 1.0, 0.0)
    for br in (WIN, SEL):
        for g in range(KV_GROUPS):
            lo = v0 + br * KV_WIDTH + g * HEAD_DIM
            vaug_ref[0, g, br, :, rows] = jnp.concatenate([pt[lo:lo + HEAD_DIM, :], ones_rows], axis=0).astype(vaug_ref.dtype)

    g0 = v0 + 2 * KV_WIDTH
    gt_ref[0, :, rows] = jax.nn.sigmoid(pt[g0:g0 + KV_GROUPS * GATE_ROWS, :])


def _in_proj(x, pos_row, inv_col, g_pre, w_row, w_t, conv_w, g_conv):
    b, s, _ = x.shape
    tm = min(TM_PROJ, s)
    n_row, n_t = w_row.shape[1], w_t.shape[0]
    out_shape = (
        jax.ShapeDtypeStruct((b, s, CONV_WIDTH), BF16),
        jax.ShapeDtypeStruct((b, 2 * KV_GROUPS, s, HEAD_DIM), F32),
        jax.ShapeDtypeStruct((b, ATT_WIDTH, s), BF16),
        jax.ShapeDtypeStruct((b, KV_GROUPS, 2, s, K_AUG), BF16),
        jax.ShapeDtypeStruct((b, KV_GROUPS, 2, V_AUG, s), BF16),
        jax.ShapeDtypeStruct((b, KV_GROUPS * GATE_ROWS, s), F32),
    )
    return pl.pallas_call(
        _in_proj_kernel,
        grid=(b, s // tm),
        in_specs=[
            pl.BlockSpec((1, tm, D_MODEL), lambda i, j: (i, j, 0)),
            pl.BlockSpec((1, 1, tm), lambda i, j: (i, 0, j)),
            _whole((HALF, 1)),
            _whole((1, D_MODEL)),
            _whole((D_MODEL, n_row)),
            _whole((n_t, D_MODEL)),
            _whole((CONV_K, CONV_WIDTH)),
            _whole((1, CONV_WIDTH)),
        ],
        out_specs=(
            pl.BlockSpec((1, tm, CONV_WIDTH), lambda i, j: (i, j, 0)),
            pl.BlockSpec((1, 2 * KV_GROUPS, tm, HEAD_DIM), lambda i, j: (i, 0, j, 0)),
            pl.BlockSpec((1, ATT_WIDTH, tm), lambda i, j: (i, 0, j)),
            pl.BlockSpec((1, KV_GROUPS, 2, tm, K_AUG), lambda i, j: (i, 0, 0, j, 0)),
            pl.BlockSpec((1, KV_GROUPS, 2, V_AUG, tm), lambda i, j: (i, 0, 0, 0, j)),
            pl.BlockSpec((1, KV_GROUPS * GATE_ROWS, tm), lambda i, j: (i, 0, j)),
        ),
        out_shape=out_shape,
        scratch_shapes=[pltpu.VMEM((8, CONV_WIDTH), F32)],
        compiler_params=_params("arbitrary", "arbitrary"),
        name="in_proj",
    )(x, pos_row, inv_col, g_pre, w_row, w_t, conv_w, g_conv)


def _compress_kernel(rk_ref, rv_ref, pos_ref, inv_ref, pek_ref, w1k_ref, w2k_ref, pev_ref, w1v_ref, w2vt_ref,
                     kc_ref, vct_ref):
    half_in = CMP_STRIDE * HEAD_DIM
    n_slab = kc_ref.shape[2]

    sides = ((rk_ref, pek_ref, w1k_ref), (rv_ref, pev_ref, w1v_ref))
    a = [jnp.zeros((n_slab, CMP_HIDDEN), F32) for _ in sides]
    b = [jnp.zeros((n_slab, CMP_HIDDEN), F32) for _ in sides]
    per_dot = 256 // HEAD_DIM
    for j0 in range(0, CMP_STRIDE, per_dot):
        lo = j0 * HEAD_DIM
        xcat = [jnp.concatenate([x_ref[0, 0, pl.ds(j, n_slab, stride=CMP_STRIDE), :]
                                 for j in range(j0, j0 + per_dot)], axis=-1).astype(BF16)
                for x_ref, _, _ in sides]
        for n, (_, _, w1_ref) in enumerate(sides):
            a[n] = a[n] + _dot(xcat[n], w1_ref[lo:lo + per_dot * HEAD_DIM, :])
            b[n] = b[n] + _dot(xcat[n], w1_ref[half_in + lo:half_in + lo + per_dot * HEAD_DIM, :])
    hidden = []
    for n, (_, pe_ref, w1_ref) in enumerate(sides):
        pe = jnp.broadcast_to(pe_ref[...], (8, 2 * half_in)).astype(BF16)
        h = a[n] + pltpu.roll(b[n], n_slab - 1, axis=0) + _dot(pe, w1_ref[...])[0:1, :]
        hidden.append((h * jax.nn.sigmoid(h)).astype(BF16))

    kc = _dot(hidden[0], w2k_ref[...])
    ang = pos_ref[0].astype(F32) * inv_ref[...]
    cos, sin = jnp.cos(ang), jnp.sin(ang)
    x1, x2 = kc[:, 0:HALF], kc[:, HALF:HEAD_DIM]
    kc_ref[0, 0] = jnp.concatenate([x1 * cos - x2 * sin, x2 * cos + x1 * sin], axis=-1).astype(kc_ref.dtype)
    vct_ref[0, 0] = _dot_nt(w2vt_ref[...], hidden[1]).astype(vct_ref.dtype)


def _compress(kvc, pos_c, inv_row, pek, w1k, w2k, pev, w1v, w2vt):
    b, _, s, _ = kvc.shape
    g, n_slab = KV_GROUPS, s // CMP_STRIDE
    return pl.pallas_call(
        _compress_kernel,
        grid=(b, g),
        in_specs=[
            pl.BlockSpec((1, 1, s, HEAD_DIM), lambda i, j: (i, j, 0, 0)),
            pl.BlockSpec((1, 1, s, HEAD_DIM), lambda i, j: (i, KV_GROUPS + j, 0, 0)),
            pl.BlockSpec((1, n_slab, 1), lambda i, j: (i, 0, 0)),
            _whole((1, HALF)),
            _whole(pek.shape), _whole(w1k.shape), _whole(w2k.shape),
            _whole(pev.shape), _whole(w1v.shape), _whole(w2vt.shape),
        ],
        out_specs=(
            pl.BlockSpec((1, 1, n_slab, HEAD_DIM), lambda i, j: (i, j, 0, 0)),
            pl.BlockSpec((1, 1, HEAD_DIM, n_slab), lambda i, j: (i, j, 0, 0)),
        ),
        out_shape=(
            jax.ShapeDtypeStruct((b, g, n_slab, HEAD_DIM), BF16),
            jax.ShapeDtypeStruct((b, g, HEAD_DIM, n_slab), BF16),
        ),
        compiler_params=_params("arbitrary", "arbitrary"),
        name="compress",
    )(kvc, kvc, pos_c, inv_row, pek, w1k, w2k, pev, w1v, w2vt)


def _select_kernel(qt_ref, kc_ref, vct_ref, gt_ref, map_ref, ocmp_ref, sel_ref, cs_ref, imp_ref, psum_ref):
    n_cmp = kc_ref.shape[2]
    n_slc = map_ref.shape[0]
    step = pl.program_id(1)
    n_sub = qt_ref.shape[2] // TQ
    subs = [slice(c * TQ, (c + 1) * TQ) for c in range(n_sub)]
    row_of = lambda h: slice(h * HEAD_DIM, (h + 1) * HEAD_DIM)
    heads_of = lambda g: range(g * Q_PER_KV, (g + 1) * Q_PER_KV)
    t_of = [(step * n_sub + c) * TQ + lax.broadcasted_iota(jnp.int32, (1, TQ), 1) for c in range(n_sub)]

    def gate_row(h):
        row = (h // Q_PER_KV) * GATE_ROWS + h % Q_PER_KV
        return gt_ref[0, row:row + 1, :]

    def compressed_branch(n_live):
        rows = min(n_live * (SLC_BLOCK // CMP_STRIDE), n_cmp)
        for c, lanes in enumerate(subs):
            for g in range(KV_GROUPS):
                kc = kc_ref[0, g, 0:rows, :]
                for h in heads_of(g):
                    cs_ref[c, h, 0:rows, :] = _dot(kc, qt_ref[0, row_of(h), lanes])

        cmp_end = lax.broadcasted_iota(jnp.int32, (rows, TQ), 0) * CMP_STRIDE + (CMP_BLOCK - 1)
        for c, lanes in enumerate(subs):
            cbias = jnp.where(cmp_end <= t_of[c], 0.0, NEG)
            has_cmp = t_of[c] >= CMP_BLOCK - 1
            for g in range(KV_GROUPS):
                vct = vct_ref[0, g, :, 0:rows]
                acc = jnp.zeros((rows, TQ), F32)
                for h in heads_of(g):
                    s = cs_ref[c, h, 0:rows, :] + cbias
                    e = jnp.exp2(s - jnp.max(s, axis=0, keepdims=True))
                    inv = jnp.where(has_cmp, 1.0 / jnp.maximum(jnp.sum(e, axis=0, keepdims=True), 1e-30), 0.0)
                    acc = acc + e * inv
                    ocmp_ref[0, row_of(h), lanes] = (gate_row(h)[:, lanes] * inv) * _dot(vct, e.astype(BF16))
                for lt in range(TQ // 128):
                    psum_ref[c, g, lt, 0:rows, :] = acc[:, lt * 128:(lt + 1) * 128]

        blk = lax.broadcasted_iota(jnp.int32, (n_live, TQ), 0)
        for c in range(n_sub):
            cur = jnp.right_shift(t_of[c], SLC_BLOCK.bit_length() - 1)
            forced = (blk == 0) | (blk == cur) | (blk == cur - 1)
            for g in range(KV_GROUPS):
                every4 = lambda k: jnp.concatenate(
                    [psum_ref[c, g, lt, pl.ds(k, n_live, stride=SLC_BLOCK // CMP_STRIDE), :] for lt in range(TQ // 128)],
                    axis=-1)
                before = jnp.where(blk == 0, 0.0, pltpu.roll(every4(3), 1, axis=0))
                imp = every4(0) + every4(1) + every4(2) + 0.5 * (every4(3) + before)
                imp_ref[c, g, 0:n_live, :] = jnp.where(forced, FORCE, jnp.where(blk * SLC_BLOCK > t_of[c], -FORCE, imp))

    def select_blocks(n_live, c, g):
        n_grp = n_live // 8
        imp_g = [imp_ref[c, g, 8 * a:8 * a + 8, :] for a in range(n_grp)]
        rank_g = [jnp.zeros((8, TQ), jnp.int32) for _ in range(n_grp)]
        sub = lax.broadcasted_iota(jnp.int32, (8, TQ), 0)
        for j in range(n_live):
            vj = jnp.broadcast_to(imp_ref[c, g, j:j + 1, :], (8, TQ))
            for a in range(n_grp):
                if 8 * a > j:
                    beats = jnp.where(vj >= imp_g[a], 1, 0)
                elif 8 * a + 7 < j:
                    beats = jnp.where(vj > imp_g[a], 1, 0)
                else:
                    beats = jnp.where(sub > j - 8 * a, jnp.where(vj >= imp_g[a], 1, 0), jnp.where(vj > imp_g[a], 1, 0))
                rank_g[a] = rank_g[a] + beats
        sel = jnp.where(jnp.concatenate(rank_g, axis=0) < min(N_SELECT, n_slc), 0.0, NEG).astype(BF16)
        dead = K_AUG - HEAD_DIM - n_live
        if dead:
            sel = jnp.concatenate([sel, jnp.full((dead, TQ), NEG, BF16)], axis=0)
        sel_ref[0, g, :, subs[c]] = sel

    def select_all(n_live):
        compressed_branch(n_live)
        for c in range(n_sub):
            for g in range(KV_GROUPS):
                select_blocks(n_live, c, g)

    blocks_per_step = qt_ref.shape[2] // SLC_BLOCK
    live_sizes = list(range(_RANK_STEP, n_slc, _RANK_STEP)) + [n_slc]
    for v, n_live in enumerate(live_sizes):
        lo = v * _RANK_STEP // blocks_per_step
        hi = n_live // blocks_per_step if v + 1 < len(live_sizes) else n_slc // blocks_per_step
        pl.when((step >= lo) & (step < hi))(lambda n_live=n_live: select_all(n_live))


def _select(qt, kcmp, vcmpt, gt, slc_map_t):
    b, _, s = qt.shape
    tqs = min(TQ_SELECT, s)
    n_cmp = kcmp.shape[2]
    n_slc = s // SLC_BLOCK
    assert tqs % TQ == 0 and s % tqs == 0 and n_slc % 8 == 0 and n_slc <= K_AUG - HEAD_DIM
    assert _RANK_STEP % 8 == 0 and _RANK_STEP % (tqs // SLC_BLOCK) == 0 and n_slc % (tqs // SLC_BLOCK) == 0
    per_batch = lambda shape: pl.BlockSpec((1,) + shape, lambda i, k: (i,) + (0,) * len(shape))
    return pl.pallas_call(
        _select_kernel,
        grid=(b, s // tqs),
        in_specs=[
            pl.BlockSpec((1, ATT_WIDTH, tqs), lambda i, k: (i, 0, k)),
            per_batch((KV_GROUPS, n_cmp, HEAD_DIM)),
            per_batch((KV_GROUPS, HEAD_DIM, n_cmp)),
            pl.BlockSpec((1, KV_GROUPS * GATE_ROWS, tqs), lambda i, k: (i, 0, k)),
            _whole((n_slc, n_cmp)),
        ],
        out_specs=(
            pl.BlockSpec((1, ATT_WIDTH, tqs), lambda i, k: (i, 0, k)),
            pl.BlockSpec((1, KV_GROUPS, K_AUG - HEAD_DIM, tqs), lambda i, k: (i, 0, 0, k)),
        ),
        out_shape=(
            jax.ShapeDtypeStruct((b, ATT_WIDTH, s), F32),
            jax.ShapeDtypeStruct((b, KV_GROUPS, K_AUG - HEAD_DIM, s), BF16),
        ),
        scratch_shapes=[
            pltpu.VMEM((tqs // TQ, ATT_HEADS, n_cmp, TQ), F32),
            pltpu.VMEM((tqs // TQ, KV_GROUPS, n_slc, TQ), F32),
            pltpu.VMEM((tqs // TQ, KV_GROUPS, TQ // 128, n_cmp, 128), F32),
        ],
        compiler_params=_params("arbitrary", "arbitrary"),
        name="nsa_select",
    )(qt, kcmp, vcmpt, gt, slc_map_t)


def _nsa_kernel(qt_ref, ocmp_ref, sel_ref, kaug_ref, vaug_ref, gt_ref, tri_ref,
                out_ref, rhs_ref, s_ref, m_ref, acc_ref):
    tq = qt_ref.shape[2]
    i = pl.program_id(1)
    row_of = lambda h: slice(h * HEAD_DIM, (h + 1) * HEAD_DIM)
    heads_of = lambda g: range(g * Q_PER_KV, (g + 1) * Q_PER_KV)

    def issue_scores(slot, br, tile):
        for g in range(KV_GROUPS):
            k_t = kaug_ref[0, g, br, pl.ds(pl.multiple_of(tile * TK, TK), TK), :]
            for h in heads_of(g):
                s_ref[slot, h] = _dot(k_t, rhs_ref[br, h]).astype(s_ref.dtype)

    def softmax_step(slot, br, tile, bias_idx=None, first=False):
        for g in range(KV_GROUPS):
            v_t = vaug_ref[0, g, br, :, pl.ds(pl.multiple_of(tile * TK, TK), TK)]
            for h in heads_of(g):
                s = s_ref[slot, h]
                if bias_idx is not None:
                    s = s + tri_ref[bias_idx]
                if first:
                    m_new = jnp.max(s, axis=0, keepdims=True).astype(F32)
                    acc_ref[h] = _dot(v_t, jnp.exp2(s - m_new.astype(s.dtype)))
                else:
                    m_old = m_ref[h]
                    m_new = jnp.maximum(m_old, jnp.max(s, axis=0, keepdims=True).astype(F32))
                    p = jnp.exp2(s - m_new.astype(s.dtype))
                    acc_ref[h] = jnp.exp2(m_old - m_new) * acc_ref[h] + _dot(v_t, p)
                m_ref[h] = m_new

    def gate_row(branch, h):
        row = (h // Q_PER_KV) * GATE_ROWS + branch * Q_PER_KV + h % Q_PER_KV
        return gt_ref[0, row:row + 1, :]

    def flush_state(branch):
        for h in range(ATT_HEADS):
            acc = acc_ref[h]
            o = acc[0:HEAD_DIM, :] * (1.0 / jnp.maximum(acc[HEAD_DIM:HEAD_DIM + 1, :], 1e-30))
            out_ref[0, row_of(h), :] = out_ref[0, row_of(h), :] + gate_row(branch, h) * o

    zeros_half = jnp.zeros((K_AUG - HEAD_DIM, tq), BF16)
    for h in range(ATT_HEADS):
        rhs_ref[WIN, h, 0:HEAD_DIM, :] = qt_ref[0, row_of(h), :]
        rhs_ref[WIN, h, HEAD_DIM:K_AUG, :] = zeros_half
        rhs_ref[SEL, h, 0:HEAD_DIM, :] = qt_ref[0, row_of(h), :]
        rhs_ref[SEL, h, HEAD_DIM:K_AUG, :] = sel_ref[0, h // Q_PER_KV]
    n_win = WINDOW // TK + 1
    win_tiles = [i - (n_win - 1) + k for k in range(n_win)]
    def window_step(k):
        exists = win_tiles[k] >= 0
        bias_idx = _BIAS_CAUSAL if k == n_win - 1 else jnp.where(exists, _BIAS_WINDOW_START if k == 0 else _BIAS_NONE, _BIAS_MASKED)
        softmax_step(_SLOT_WIN + k, WIN, jnp.maximum(win_tiles[k], 0), bias_idx, first=(k == 0))

    for k in range(n_win):
        issue_scores(_SLOT_WIN + k, WIN, jnp.maximum(win_tiles[k], 0))
    issue_scores(0, SEL, 0)
    for h in range(ATT_HEADS):
        out_ref[0, row_of(h), :] = ocmp_ref[0, row_of(h), :]
    for k in range(n_win):
        window_step(k)
    flush_state(2)

    for h in range(ATT_HEADS):
        m_ref[h] = jnp.full((1, tq), NEG, F32)
        acc_ref[h] = jnp.zeros((V_AUG, tq), F32)

    def pair_steps(a):
        issue_scores(1, SEL, a + 1)
        softmax_step(0, SEL, a)
        issue_scores(0, SEL, a + 2)
        softmax_step(1, SEL, a + 1)

    def body(quad, carry):
        pair_steps(4 * quad)
        pair_steps(4 * quad + 2)
        return carry

    n_plain = jnp.right_shift(i, 1)
    n_quad = jnp.right_shift(n_plain, 1)
    lax.fori_loop(0, n_quad, body, 0)

    @pl.when(n_plain != 2 * n_quad)
    def _():
        pair_steps(4 * n_quad)

    @pl.when(i == 2 * n_plain)
    def _():
        softmax_step(0, SEL, i, _BIAS_CAUSAL)

    @pl.when(i != 2 * n_plain)
    def _():
        issue_scores(1, SEL, i)
        softmax_step(0, SEL, i - 1)
        softmax_step(1, SEL, i, _BIAS_CAUSAL)

    flush_state(1)


def _nsa(qt, ocmp, sel, kaug, vaug, gt, tri):
    b, _, s = qt.shape
    tq = min(TQ, s)
    assert tq == TK and s % tq == 0 and WINDOW % TK == 0 and _SLOT_WIN + WINDOW // TK + 1 <= _N_SLOTS
    per_batch = lambda shape: pl.BlockSpec((1,) + shape, lambda i, k: (i,) + (0,) * len(shape))
    return pl.pallas_call(
        _nsa_kernel,
        grid=(b, s // tq),
        in_specs=[
            pl.BlockSpec((1, ATT_WIDTH, tq), lambda i, k: (i, 0, k)),
            pl.BlockSpec((1, ATT_WIDTH, tq), lambda i, k: (i, 0, k)),
            pl.BlockSpec((1, KV_GROUPS, K_AUG - HEAD_DIM, tq), lambda i, k: (i, 0, 0, k)),
            per_batch((KV_GROUPS, 2, s, K_AUG)),
            per_batch((KV_GROUPS, 2, V_AUG, s)),
            pl.BlockSpec((1, KV_GROUPS * GATE_ROWS, tq), lambda i, k: (i, 0, k)),
            _whole((4, TK, tq)),
        ],
        out_specs=pl.BlockSpec((1, ATT_WIDTH, tq), lambda i, k: (i, 0, k)),
        out_shape=jax.ShapeDtypeStruct((b, ATT_WIDTH, s), F32),
        scratch_shapes=[
            pltpu.VMEM((2, ATT_HEADS, K_AUG, tq), BF16),
            pltpu.VMEM((_N_SLOTS, ATT_HEADS, TK, tq), BF16),
            pltpu.VMEM((ATT_HEADS, 1, tq), F32),
            pltpu.VMEM((ATT_HEADS, V_AUG, tq), F32),
        ],
        compiler_params=_params("arbitrary", "arbitrary"),
        name="nsa_attention",
    )(qt, ocmp, sel, kaug, vaug, gt, tri)


def _mix_xattn_kernel(x_ref, yconv_ref, yatt_ref, gatt_ref, woa_ref, wob_ref, gpost_ref, gxpre_ref,
                      wqt_ref, mem_ref, gmem_ref, wk_ref, wvt_ref, wo_ref, gxpost_ref, out_ref, kx_ref, vxt_ref):
    @pl.when(pl.program_id(1) == 0)
    def _():
        mn = _rms_rows(mem_ref[0], gmem_ref[...]).astype(BF16)
        kx_ref[...] = _dot(mn, wk_ref[...]).astype(BF16)
        vxt_ref[...] = _dot_nt(wvt_ref[...], mn).astype(BF16)

    subs = [slice(c0, c0 + SUB_ROWS) for c0 in range(0, x_ref.shape[1], SUB_ROWS)]
    head = lambda h: slice(h * X_HEAD_DIM, (h + 1) * X_HEAD_DIM)

    def yatt_normed(rows):
        ya = yatt_ref[0, :, rows]
        return (ya * lax.rsqrt(jnp.mean(ya * ya, axis=0, keepdims=True) + EPS) * gatt_ref[...]).astype(BF16)

    yan = [yatt_normed(rows) for rows in subs]
    mixed = [_dot(yconv_ref[0, rows, :], woa_ref[...]) + _dot_tn(y, wob_ref[...]) for rows, y in zip(subs, yan)]
    h1 = [x_ref[0, rows, :] + _rms_rows(m, gpost_ref[...]) for rows, m in zip(subs, mixed)]
    hn = [_rms_rows(h, gxpre_ref[...]).astype(BF16) for h in h1]
    qxt = [(_dot_nt(wqt_ref[...], h) * (X_HEAD_DIM ** -0.5 * math.log2(math.e))).astype(BF16) for h in hn]
    scores = [[_dot(kx_ref[:, head(h)], q[head(h), :]) for h in range(X_HEADS)] for q in qxt]

    def attend(sc):
        heads = []
        for h in range(X_HEADS):
            e = jnp.exp2(sc[h] - jnp.max(sc[h], axis=0, keepdims=True))
            o = _dot(vxt_ref[head(h), :], e.astype(BF16))
            heads.append(o * (1.0 / jnp.sum(e, axis=0, keepdims=True)))
        return jnp.concatenate(heads, axis=0).astype(BF16)

    oxt = [attend(sc) for sc in scores]
    proj = [_dot_tn(o, wo_ref[...]) for o in oxt]
    for rows, h, p in zip(subs, h1, proj):
        out_ref[0, rows, :] = h + _rms_rows(p, gxpost_ref[...])


def _mix_xattn(x, yconv, yatt_t, g_att, w_out_a, w_out_b, g_post, g_xpre, w_qt, mem, g_mem, w_k, w_vt, w_o, g_xpost):
    b, s, _ = x.shape
    m = mem.shape[1]
    tm = min(TM_MIX, s)
    return pl.pallas_call(
        _mix_xattn_kernel,
        grid=(b, s // tm),
        in_specs=[
            pl.BlockSpec((1, tm, D_MODEL), lambda i, j: (i, j, 0)),
            pl.BlockSpec((1, tm, CONV_WIDTH), lambda i, j: (i, j, 0)),
            pl.BlockSpec((1, ATT_WIDTH, tm), lambda i, j: (i, 0, j)),
            _whole((ATT_WIDTH, 1)),
            _whole((CONV_WIDTH, D_MODEL)), _whole((ATT_WIDTH, D_MODEL)),
            _whole((1, D_MODEL)), _whole((1, D_MODEL)),
            _whole((D_MODEL, D_MODEL)),
            pl.BlockSpec((1, m, D_MODEL), lambda i, j: (i, 0, 0)),
            _whole((1, D_MODEL)), _whole((D_MODEL, D_MODEL)), _whole((D_MODEL, D_MODEL)),
            _whole((D_MODEL, D_MODEL)), _whole((1, D_MODEL)),
        ],
        out_specs=pl.BlockSpec((1, tm, D_MODEL), lambda i, j: (i, j, 0)),
        out_shape=jax.ShapeDtypeStruct((b, s, D_MODEL), F32),
        scratch_shapes=[
            pltpu.VMEM((m, D_MODEL), BF16),
            pltpu.VMEM((D_MODEL, m), BF16),
        ],
        compiler_params=_params("arbitrary", "arbitrary"),
        name="mix_xattn",
    )(x, yconv, yatt_t, g_att, w_out_a, w_out_b, g_post, g_xpre, w_qt, mem, g_mem, w_k, w_vt, w_o, g_xpost)


def _ffn_kernel(h_ref, gpre_ref, wgu_ref, wdown_ref, gpost_ref, out_ref):
    subs = [slice(c0, c0 + TM_FFN_SUB) for c0 in range(0, h_ref.shape[0], TM_FFN_SUB)]
    hn = [_rms_rows(h_ref[rows, :], gpre_ref[...]).astype(BF16) for rows in subs]
    gu = [_dot(v, wgu_ref[...]) for v in hn]
    act = [(v[:, 0:D_FF] * jax.nn.sigmoid(v[:, 0:D_FF]) * v[:, D_FF:2 * D_FF]).astype(BF16) for v in gu]
    down = [_dot(a, wdown_ref[...]) for a in act]
    for rows, d in zip(subs, down):
        out_ref[rows, :] = h_ref[rows, :] + _rms_rows(d, gpost_ref[...])


def _ffn(h, g_pre, w_gu, w_down, g_post):
    n, _ = h.shape
    tm = min(TM_FFN, n)
    return pl.pallas_call(
        _ffn_kernel,
        grid=(n // tm,),
        in_specs=[pl.BlockSpec((tm, D_MODEL), lambda i: (i, 0)), _whole((1, D_MODEL)),
                  _whole((D_MODEL, 2 * D_FF)), _whole((D_FF, D_MODEL)), _whole((1, D_MODEL))],
        out_specs=pl.BlockSpec((tm, D_MODEL), lambda i: (i, 0)),
        out_shape=jax.ShapeDtypeStruct((n, D_MODEL), F32),
        compiler_params=_params("arbitrary"),
        name="ffn",
    )(h, g_pre, w_gu, w_down, g_post)


def _slc_map_t(s):
    n_slab = s // CMP_STRIDE
    ci = np.arange(n_slab)[None, :] * CMP_STRIDE
    sj = np.arange(s // SLC_BLOCK)[:, None] * SLC_BLOCK
    ov = np.clip(np.minimum(ci + CMP_BLOCK, sj + SLC_BLOCK) - np.maximum(ci, sj), 0, None)
    ov[:, n_slab - 1] = 0
    return jnp.asarray(ov / CMP_BLOCK, dtype=F32)


def _tile_biases(tq):
    key = np.arange(TK)[:, None]
    qry = np.arange(tq)[None, :]
    none = np.zeros((TK, tq), np.float32)
    causal = np.where(key <= qry, 0.0, NEG).astype(np.float32)
    window_start = np.where(key > qry, 0.0, NEG).astype(np.float32)
    masked = np.full((TK, tq), NEG, np.float32)
    return jnp.asarray(np.stack([none, causal, window_start, masked]), dtype=BF16)


def kernel(x, mem, positions, norm_mix_pre, w_in, conv_w, pe_kc, w1_kc, w2_kc, pe_vc, w1_vc, w2_vc,
           norm_conv_out, norm_attn_out, w_out, norm_mix_post, norm_x_pre, norm_mem, w_q_x, w_kv_x,
           w_o_x, norm_x_post, norm_ffn_pre, w_gate_up, w_down, norm_ffn_post):
    b, s, _ = x.shape
    row = lambda v: v.reshape(1, -1).astype(F32)

    cuts = np.cumsum([0, CONV_WIDTH, CONV_WIDTH, CONV_WIDTH, ATT_WIDTH] + [KV_WIDTH] * 6 + [3 * ATT_HEADS])
    w_in = w_in.astype(BF16)
    col = lambda k: w_in[:, cuts[k]:cuts[k + 1]]
    w_row = jnp.concatenate([col(0), col(1), col(2), col(4), col(5)], axis=1)
    gate_cols = np.zeros((KV_GROUPS * GATE_ROWS,), np.int32)
    gate_live = np.zeros((KV_GROUPS * GATE_ROWS,), np.float32)
    for g in range(KV_GROUPS):
        for c in range(3):
            for r in range(Q_PER_KV):
                gate_cols[g * GATE_ROWS + c * Q_PER_KV + r] = (g * Q_PER_KV + r) * 3 + c
                gate_live[g * GATE_ROWS + c * Q_PER_KV + r] = 1.0
    w_gate = col(10)[:, gate_cols] * jnp.asarray(gate_live, BF16)[None, :]
    w_t = jnp.concatenate([col(3), col(8), col(6), col(9), col(7), w_gate], axis=1).T

    inv = ROPE_THETA ** (-jnp.arange(HALF, dtype=F32) / HALF)
    yconv, kvc, qt, kaug, vaug, gt = _in_proj(
        x, positions.reshape(b, 1, s), inv.reshape(HALF, 1), row(norm_mix_pre), w_row, w_t,
        conv_w.astype(F32), row(norm_conv_out))

    n_slab = s // CMP_STRIDE
    pos_c = positions[:, np.minimum(np.arange(n_slab) * CMP_STRIDE + CMP_BLOCK - 1, s - 1)]
    kcmp, vcmpt = _compress(
        kvc, pos_c.reshape(b, n_slab, 1), inv.reshape(1, HALF),
        pe_kc.reshape(1, -1), w1_kc.astype(BF16), w2_kc.astype(BF16),
        pe_vc.reshape(1, -1), w1_vc.astype(BF16), w2_vc.astype(BF16).T)

    ocmp, sel = _select(qt, kcmp, vcmpt, gt, _slc_map_t(s))
    yatt_t = _nsa(qt, ocmp, sel, kaug, vaug, gt, _tile_biases(min(TQ, s)))

    w_kv_x = w_kv_x.astype(BF16)
    h2 = _mix_xattn(
        x, yconv, yatt_t, norm_attn_out.reshape(-1, 1).astype(F32),
        w_out[:CONV_WIDTH].astype(BF16), w_out[CONV_WIDTH:].astype(BF16),
        row(norm_mix_post), row(norm_x_pre), w_q_x.astype(BF16).T,
        mem, row(norm_mem), w_kv_x[:, :D_MODEL], w_kv_x[:, D_MODEL:].T, w_o_x.astype(BF16), row(norm_x_post))

    out = _ffn(h2.reshape(b * s, D_MODEL), row(norm_ffn_pre), w_gate_up.astype(BF16), w_down.astype(BF16),
               row(norm_ffn_post))
    return out.reshape(b, s, D_MODEL)
```
